```python
import math
import jax, jax.numpy as jnp
from jax import lax
import numpy as np

D_MODEL = 2048
BATCH = 2
SEQ = 16384
DEPTH = 2

GRID_W = 64
CTX_LEN = 256
CHUNK = 64
CONV_W = 5
EPS = 1e-6
ML_DV = 256
ML_DK = 128
ML_HEADS = (D_MODEL // 2) // ML_DV
DN_DK = 128
DN_DV = 128
DN_HEADS = (D_MODEL // 2) // DN_DV
RG_WIDTH = D_MODEL // 2
RG_BLOCKS = 8
RG_C = 8.0
GLA_DV = 256
GLA_DK = 128
GLA_HEADS = (D_MODEL // 2) // GLA_DV
GLA_RANK = 16
GLA_TAU = 16.0
FFN_HIDDEN = ((8 * D_MODEL + 3 * 256 - 1) // (3 * 256)) * 256
L0_SPLITS = (ML_HEADS * ML_DK, ML_HEADS * ML_DK, ML_HEADS * ML_DV, ML_HEADS * ML_DV, 2 * ML_HEADS, 2 * ML_HEADS,
             DN_HEADS * DN_DK, DN_HEADS * DN_DK, DN_HEADS * DN_DV, DN_HEADS * DN_DV, 2 * DN_HEADS, 2 * DN_HEADS)
L0_COLS = sum(L0_SPLITS)
L0_OUT = ML_HEADS * ML_DV + DN_HEADS * DN_DV
L1_SPLITS = (RG_WIDTH, RG_WIDTH, GLA_HEADS * GLA_DK, GLA_HEADS * GLA_DK, GLA_HEADS * GLA_DV, GLA_HEADS * GLA_DV,
             2 * GLA_RANK)
L1_COLS = sum(L1_SPLITS)
L1_OUT = RG_WIDTH + GLA_HEADS * GLA_DV

kernel_name = 'hybrid_mlstm_deltanet_rglru_gla_dit'


def rmsnorm(x, g):
    xf = x.astype(jnp.float32)
    y = xf * lax.rsqrt(jnp.mean(xf * xf, axis=-1, keepdims=True) + EPS)
    return (y * g.astype(jnp.float32)).astype(x.dtype)


def head_rmsnorm(o, g):
    B, H, T, d = o.shape
    o = jnp.transpose(o, (0, 2, 1, 3))
    o = o * lax.rsqrt(jnp.mean(o * o, axis=-1, keepdims=True) + EPS)
    return o.reshape(B, T, H * d) * g


def l2norm(x):
    return x * lax.rsqrt(jnp.sum(x * x, axis=-1, keepdims=True) + EPS)


def split_cols(y, sizes):
    idx = [int(v) for v in np.cumsum(sizes)[:-1]]
    return jnp.split(y, idx, axis=-1)


def to_heads(x, h):
    B, T, _ = x.shape
    return jnp.transpose(x.reshape(B, T, h, -1), (0, 2, 1, 3))


def dir_gates(z, h):
    B, T, _ = z.shape
    return jnp.transpose(z.reshape(B, T, 2, h), (2, 0, 3, 1))


def adaln(cond, w_mod, b_mod):
    m = jax.nn.silu(cond) @ w_mod + b_mod
    return [t[..., None, :] for t in jnp.split(m, 6, axis=-1)]


def swiglu(h, w_gu, w_down):
    g, u = jnp.split(h @ w_gu, 2, axis=-1)
    return (jax.nn.silu(g) * u) @ w_down


def dwconv_seq(x, w):
    T = x.shape[1]
    pad = CONV_W // 2
    xp = jnp.pad(x, ((0, 0), (pad, pad), (0, 0)))
    out = xp[:, 0:T] * w[0]
    for j in range(1, CONV_W):
        out = out + xp[:, j:j + T] * w[j]
    return out


def dwconv_grid(x, w):
    B, T, C = x.shape
    rows = T // GRID_W
    return dwconv_seq(x.reshape(B * rows, GRID_W, C), w).reshape(B, T, C)


def mlstm_chunked(q, k, v, ig, lf, state):
    B, H, T, dk = q.shape
    dv = v.shape[-1]
    N, L = T // CHUNK, CHUNK
    qc = (q * dk ** -0.5).reshape(B, H, N, L, dk)
    kc = k.reshape(B, H, N, L, dk)
    vc = v.reshape(B, H, N, L, dv)
    igc = ig.reshape(B, H, N, L)
    b = jnp.cumsum(lf.reshape(B, H, N, L), axis=-1)
    w_log = b[..., -1:] - b + igc
    m_loc = jnp.max(w_log, axis=-1)
    w = jnp.exp(w_log - m_loc[..., None])

    def step(carry, inp):
        C, n, m = carry
        q_i, w_i, v_i, k_i, m_i, bL_i = inp
        qC = jnp.einsum('bhld,bhed->bhle', q_i, C)
        qn = jnp.einsum('bhld,bhd->bhl', q_i, n)
        m_new = jnp.maximum(bL_i + m, m_i)
        a = jnp.exp(bL_i + m - m_new)
        s = jnp.exp(m_i - m_new)
        C = a[..., None, None] * C + s[..., None, None] * jnp.einsum('bhl,bhle,bhld->bhed', w_i, v_i, k_i)
        n = a[..., None] * n + s[..., None] * jnp.einsum('bhl,bhld->bhd', w_i, k_i)
        return (C, n, m_new), (qC, qn, m)

    xs = tuple(jnp.moveaxis(t, 2, 0) for t in (qc, w, vc, kc, m_loc, b[..., -1]))
    final, (qC, qn, m_in) = lax.scan(step, state, xs)
    qC = jnp.moveaxis(qC, 0, 2)
    qn = jnp.moveaxis(qn, 0, 2)
    m_in = jnp.moveaxis(m_in, 0, 2)
    tril = jnp.tril(jnp.ones((L, L), bool))
    inter_log = b + m_in[..., None]
    d_log = jnp.where(tril, b[..., :, None] - b[..., None, :] + igc[..., None, :], -jnp.inf)
    m_out = jnp.maximum(jnp.max(d_log, axis=-1), inter_log)
    s = jnp.einsum('bhnld,bhnsd->bhnls', qc, kc) * jnp.exp(d_log - m_out[..., None])
    e_inter = jnp.exp(inter_log - m_out)
    num = jnp.einsum('bhnls,bhnse->bhnle', s, vc) + e_inter[..., None] * qC
    den = jnp.sum(s, axis=-1) + e_inter * qn
    h = num / jnp.maximum(jnp.abs(den), jnp.exp(-m_out))[..., None]
    return h.reshape(B, H, T, dv), final


def gated_delta_chunked(q, k, v, g, beta, state):
    B, H, T, dk = q.shape
    dv = v.shape[-1]
    N, L = T // CHUNK, CHUNK
    qc = (q * dk ** -0.5).reshape(B, H, N, L, dk)
    kc = k.reshape(B, H, N, L, dk)
    vc = v.reshape(B, H, N, L, dv)
    bc = beta.reshape(B, H, N, L)
    gc = jnp.cumsum(g.reshape(B, H, N, L), axis=-1)
    tril = jnp.tril(jnp.ones((L, L), bool))
    strict = jnp.tril(jnp.ones((L, L), bool), -1)
    decay = jnp.exp(jnp.where(tril, gc[..., :, None] - gc[..., None, :], -jnp.inf))
    kb = kc * bc[..., None]
    a = jnp.where(strict, jnp.einsum('bhnld,bhnsd->bhnls', kb, kc) * decay, 0.0)
    eye = jnp.eye(L, dtype=jnp.float32)
    tm = lax.linalg.triangular_solve(eye + a, jnp.broadcast_to(eye, a.shape), left_side=True, lower=True,
                                     unit_diagonal=True)
    u = tm @ (vc * bc[..., None])
    wk = tm @ (kb * jnp.exp(gc)[..., None])
    qk = jnp.einsum('bhnld,bhnsd->bhnls', qc, kc) * decay
    qg = qc * jnp.exp(gc)[..., None]
    kd = kc * jnp.exp(gc[..., -1:] - gc)[..., None]

    def step(S, inp):
        qg_i, qk_i, u_i, w_i, kd_i, gL_i = inp
        v_new = u_i - w_i @ S
        o = qg_i @ S + qk_i @ v_new
        S = jnp.exp(gL_i)[..., None, None] * S + jnp.einsum('bhld,bhle->bhde', kd_i, v_new)
        return S, o

    xs = tuple(jnp.moveaxis(t, 2, 0) for t in (qg, qk, u, wk, kd, gc[..., -1]))
    S, o = lax.scan(step, state, xs)
    return jnp.moveaxis(o, 0, 2).reshape(B, H, T, dv), S


def gla_chunked(q, k, v, lg, state):
    B, H, T, dk = q.shape
    dv = v.shape[-1]
    N, L = T // CHUNK, CHUNK
    qc = (q * dk ** -0.5).reshape(B, H, N, L, dk)
    kc = k.reshape(B, H, N, L, dk)
    vc = v.reshape(B, H, N, L, dv)
    b = jnp.cumsum(lg.reshape(B, H, N, L, dk), axis=-2)
    qe = qc * jnp.exp(b)
    ke = kc * jnp.exp(-b)
    kd = kc * jnp.exp(b[..., -1:, :] - b)
    tril = jnp.tril(jnp.ones((L, L), bool))
    att = jnp.where(tril, jnp.einsum('bhnld,bhnsd->bhnls', qe, ke), 0.0)
    intra = att @ vc

    def step(S, inp):
        qe_i, kd_i, v_i, bL_i = inp
        o = qe_i @ S
        S = jnp.exp(bL_i)[..., None] * S + jnp.einsum('bhld,bhle->bhde', kd_i, v_i)
        return S, o

    xs = tuple(jnp.moveaxis(t, 2, 0) for t in (qe, kd, vc, b[..., -1, :]))
    S, inter = lax.scan(step, state, xs)
    o = intra + jnp.moveaxis(inter, 0, 2)
    return o.reshape(B, H, T, dv), S


def lru_scan(a, bx, h0):
    def comb(l, r):
        return l[0] * r[0], r[0] * l[1] + r[1]
    acum, hcum = lax.associative_scan(comb, (a, bx), axis=1)
    h = hcum + acum * h0[:, None, :]
    return h, h[:, -1]


def bidirectional(run, ctx_shared, ctx_dir, lat_shared, lat_dir, state0, axis):
    fl = lambda t: jnp.flip(t, axis)
    oc_f, s_f = run(*ctx_shared, *[t[0] for t in ctx_dir], state0)
    ol_f, _ = run(*lat_shared, *[t[0] for t in lat_dir], s_f)
    oc_b, s_b = run(*[fl(t) for t in ctx_shared], *[fl(t[1]) for t in ctx_dir], state0)
    ol_b, _ = run(*[fl(t) for t in lat_shared], *[fl(t[1]) for t in lat_dir], s_b)
    return oc_f + fl(oc_b), ol_f + fl(ol_b)


def mix_mlstm_deltanet(hc, hl, w_in, w_out, ml_ig_b, ml_fg_b, ml_norm_g, dn_conv_w, dn_a_log, dn_dt_bias,
                       dn_norm_g, need_ctx):
    def prep(h, conv):
        mq, mk, mv, mo, mi, mf, dq, dk, dv, dz, da, db = split_cols((h @ w_in).astype(jnp.float32), L0_SPLITS)
        ml = (to_heads(mq, ML_HEADS), to_heads(mk, ML_HEADS), to_heads(mv, ML_HEADS))
        ml_dir = (dir_gates(mi, ML_HEADS) + ml_ig_b[:, None, :, None],
                  jax.nn.log_sigmoid(dir_gates(mf, ML_HEADS) + ml_fg_b[:, None, :, None]))
        qkv = jax.nn.silu(conv(jnp.concatenate([dq, dk, dv], axis=-1), dn_conv_w))
        dq, dk, dv = split_cols(qkv, (DN_HEADS * DN_DK, DN_HEADS * DN_DK, DN_HEADS * DN_DV))
        dn = (l2norm(to_heads(dq, DN_HEADS)), l2norm(to_heads(dk, DN_HEADS)), to_heads(dv, DN_HEADS))
        dt = jax.nn.softplus(dir_gates(da, DN_HEADS) + dn_dt_bias[:, None, :, None])
        dn_dir = (-jnp.exp(dn_a_log)[:, None, :, None] * dt, jax.nn.sigmoid(dir_gates(db, DN_HEADS)))
        return ml, ml_dir, mo, dn, dn_dir, dz

    pc = prep(hc, dwconv_seq)
    pl = prep(hl, dwconv_grid)
    B = hc.shape[0]
    ml0 = (jnp.zeros((B, ML_HEADS, ML_DV, ML_DK), jnp.float32), jnp.zeros((B, ML_HEADS, ML_DK), jnp.float32),
           jnp.zeros((B, ML_HEADS), jnp.float32))
    ml_c, ml_l = bidirectional(mlstm_chunked, pc[0], pc[1], pl[0], pl[1], ml0, 2)
    dn0 = jnp.zeros((B, DN_HEADS, DN_DK, DN_DV), jnp.float32)
    dn_c, dn_l = bidirectional(gated_delta_chunked, pc[3], pc[4], pl[3], pl[4], dn0, 2)

    def merge(ml, mo, dn, dz):
        y = jnp.concatenate([head_rmsnorm(ml, ml_norm_g) * jax.nn.sigmoid(mo),
                             head_rmsnorm(dn, dn_norm_g) * jax.nn.silu(dz)], axis=-1)
        return y.astype(w_out.dtype) @ w_out

    yl = merge(ml_l, pl[2], dn_l, pl[5])
    yc = merge(ml_c, pc[2], dn_c, pc[5]) if need_ctx else None
    return yc, yl


def mix_rglru_gla(hc, hl, w_in, w_out, rg_conv_w, rg_conv_b, rg_w_r, rg_b_r, rg_w_i, rg_b_i, rg_lambda,
                  gla_w_up, gla_b_up, gla_norm_g, need_ctx):
    def prep(h, conv):
        B, T, _ = h.shape
        xr, gr, q, k, v, og, lr = split_cols((h @ w_in).astype(jnp.float32), L1_SPLITS)
        xc = conv(xr, rg_conv_w) + rg_conv_b
        xb = xc.reshape(B, T, RG_BLOCKS, RG_WIDTH // RG_BLOCKS)
        r = jax.nn.sigmoid(jnp.einsum('btgi,zgij->zbtgj', xb, rg_w_r).reshape(2, B, T, RG_WIDTH)
                           + rg_b_r[:, None, None, :])
        i = jax.nn.sigmoid(jnp.einsum('btgi,zgij->zbtgj', xb, rg_w_i).reshape(2, B, T, RG_WIDTH)
                           + rg_b_i[:, None, None, :])
        log_a = -RG_C * r * jax.nn.softplus(-rg_lambda)[:, None, None, :]
        rg_dir = (jnp.exp(log_a), jnp.sqrt(-jnp.expm1(2.0 * log_a)) * (i * xc))
        z = jnp.einsum('btzr,zrk->zbtk', lr.reshape(B, T, 2, GLA_RANK), gla_w_up) + gla_b_up[:, None, None, :]
        lg = jnp.transpose((jax.nn.log_sigmoid(z) / GLA_TAU).reshape(2, B, T, GLA_HEADS, GLA_DK), (0, 1, 3, 2, 4))
        gla = (to_heads(q, GLA_HEADS), to_heads(k, GLA_HEADS), to_heads(v, GLA_HEADS))
        return rg_dir, gr, gla, (lg,), og

    pc = prep(hc, dwconv_seq)
    pl = prep(hl, dwconv_grid)
    B = hc.shape[0]
    rg_c, rg_l = bidirectional(lru_scan, (), pc[0], (), pl[0], jnp.zeros((B, RG_WIDTH), jnp.float32), 1)
    gla0 = jnp.zeros((B, GLA_HEADS, GLA_DK, GLA_DV), jnp.float32)
    gla_c, gla_l = bidirectional(gla_chunked, pc[2], pc[3], pl[2], pl[3], gla0, 2)

    def merge(rg, gr, gla, og):
        y = jnp.concatenate([jax.nn.gelu(gr) * rg, head_rmsnorm(gla, gla_norm_g) * jax.nn.silu(og)], axis=-1)
        return y.astype(w_out.dtype) @ w_out

    yl = merge(rg_l, pl[1], gla_l, pl[4])
    yc = merge(rg_c, pc[1], gla_c, pc[4]) if need_ctx else None
    return yc, yl


def setup_inputs(seed: int = 0) -> dict:
    key = jax.random.key(seed)
    kit = iter(jax.random.split(key, 48))
    D, F = D_MODEL, FFN_HIDDEN
    NE, NO = (DEPTH + 1) // 2, DEPTH // 2

    def nrm(shape, scale):
        return jax.random.normal(next(kit), shape, jnp.float32) * scale

    def uni(shape, lo, hi):
        return jax.random.uniform(next(kit), shape, jnp.float32, lo, hi)

    def gain(shape):
        return 1.0 + nrm(shape, 0.02)

    x = nrm((BATCH, SEQ, D), 1.0)
    c = nrm((BATCH, D), 1.0)
    ctx = nrm((BATCH, CTX_LEN, D), 1.0)
    c_ctx = nrm((D,), 1.0)
    mod_w = nrm((DEPTH, D, 6 * D), 0.5 * D ** -0.5)
    mod_b = nrm((DEPTH, 6 * D), 0.02)
    norm_mix_g = gain((DEPTH, D))
    norm_ffn_g = gain((DEPTH, D))
    ffn_w_gu = nrm((DEPTH, D, 2 * F), D ** -0.5)
    ffn_w_down = nrm((DEPTH, F, D), F ** -0.5)
    final_norm_g = gain((D,))
    ab_w_in = nrm((NE, D, L0_COLS), D ** -0.5)
    ab_w_out = nrm((NE, L0_OUT, D), L0_OUT ** -0.5)
    ml_ig_b = nrm((NE, 2, ML_HEADS), 0.1)
    ml_fg_b = uni((NE, 2, ML_HEADS), 3.0, 6.0)
    ml_norm_g = gain((NE, ML_HEADS * ML_DV))
    dn_conv_w = nrm((NE, CONV_W, DN_HEADS * (2 * DN_DK + DN_DV)), CONV_W ** -0.5)
    dn_a_log = jnp.log(uni((NE, 2, DN_HEADS), 1.0, 16.0))
    dt = jnp.exp(uni((NE, 2, DN_HEADS), math.log(1e-3), math.log(1e-1)))
    dn_dt_bias = dt + jnp.log(-jnp.expm1(-dt))
    dn_norm_g = gain((NE, DN_HEADS * DN_DV))
    cd_w_in = nrm((NO, D, L1_COLS), D ** -0.5)
    cd_w_out = nrm((NO, L1_OUT, D), L1_OUT ** -0.5)
    rg_conv_w = nrm((NO, CONV_W, RG_WIDTH), CONV_W ** -0.5)
    rg_conv_b = nrm((NO, RG_WIDTH), 0.02)
    bw = RG_WIDTH // RG_BLOCKS
    rg_w_r = nrm((NO, 2, RG_BLOCKS, bw, bw), bw ** -0.5)
    rg_b_r = nrm((NO, 2, RG_WIDTH), 0.1)
    rg_w_i = nrm((NO, 2, RG_BLOCKS, bw, bw), bw ** -0.5)
    rg_b_i = nrm((NO, 2, RG_WIDTH), 0.1)
    sig = uni((NO, 2, RG_WIDTH), 0.9, 0.999) ** (1.0 / RG_C)
    rg_lambda = jnp.log(sig) - jnp.log1p(-sig)
    gla_w_up = nrm((NO, 2, GLA_RANK, GLA_HEADS * GLA_DK), GLA_RANK ** -0.5)
    gla_b_up = nrm((NO, 2, GLA_HEADS * GLA_DK), 0.1)
    gla_norm_g = gain((NO, GLA_HEADS * GLA_DV))
    return {'x': x, 'c': c, 'ctx': ctx, 'c_ctx': c_ctx, 'mod_w': mod_w, 'mod_b': mod_b,
            'norm_mix_g': norm_mix_g, 'norm_ffn_g': norm_ffn_g, 'ffn_w_gu': ffn_w_gu, 'ffn_w_down': ffn_w_down,
            'final_norm_g': final_norm_g, 'ab_w_in': ab_w_in, 'ab_w_out': ab_w_out, 'ml_ig_b': ml_ig_b,
            'ml_fg_b': ml_fg_b, 'ml_norm_g': ml_norm_g, 'dn_conv_w': dn_conv_w, 'dn_a_log': dn_a_log,
            'dn_dt_bias': dn_dt_bias, 'dn_norm_g': dn_norm_g, 'cd_w_in': cd_w_in, 'cd_w_out': cd_w_out,
            'rg_conv_w': rg_conv_w, 'rg_conv_b': rg_conv_b, 'rg_w_r': rg_w_r, 'rg_b_r': rg_b_r,
            'rg_w_i': rg_w_i, 'rg_b_i': rg_b_i, 'rg_lambda': rg_lambda, 'gla_w_up': gla_w_up,
            'gla_b_up': gla_b_up, 'gla_norm_g': gla_norm_g}


def reference(x, c, ctx, c_ctx, mod_w, mod_b, norm_mix_g, norm_ffn_g, ffn_w_gu, ffn_w_down, final_norm_g,
              ab_w_in, ab_w_out, ml_ig_b, ml_fg_b, ml_norm_g, dn_conv_w, dn_a_log, dn_dt_bias, dn_norm_g,
              cd_w_in, cd_w_out, rg_conv_w, rg_conv_b, rg_w_r, rg_b_r, rg_w_i, rg_b_i, rg_lambda,
              gla_w_up, gla_b_up, gla_norm_g):
    xc, xl = ctx, x
    for layer in range(DEPTH):
        last = layer == DEPTH - 1
        j = layer // 2
        sh1, sc1, g1, sh2, sc2, g2 = adaln(c, mod_w[layer], mod_b[layer])
        csh1, csc1, cg1, csh2, csc2, cg2 = adaln(c_ctx, mod_w[layer], mod_b[layer])
        hl = rmsnorm(xl, norm_mix_g[layer]) * (1.0 + sc1) + sh1
        hc = rmsnorm(xc, norm_mix_g[layer]) * (1.0 + csc1) + csh1
        if layer % 2 == 0:
            yc, yl = mix_mlstm_deltanet(hc, hl, ab_w_in[j], ab_w_out[j], ml_ig_b[j], ml_fg_b[j], ml_norm_g[j],
                                        dn_conv_w[j], dn_a_log[j], dn_dt_bias[j], dn_norm_g[j], not last)
        else:
            yc, yl = mix_rglru_gla(hc, hl, cd_w_in[j], cd_w_out[j], rg_conv_w[j], rg_conv_b[j], rg_w_r[j],
                                   rg_b_r[j], rg_w_i[j], rg_b_i[j], rg_lambda[j], gla_w_up[j], gla_b_up[j],
                                   gla_norm_g[j], not last)
        xl = xl + g1 * yl
        xl = xl + g2 * swiglu(rmsnorm(xl, norm_ffn_g[layer]) * (1.0 + sc2) + sh2, ffn_w_gu[layer],
                              ffn_w_down[layer])
        if not last:
            xc = xc + cg1 * yc
            xc = xc + cg2 * swiglu(rmsnorm(xc, norm_ffn_g[layer]) * (1.0 + csc2) + csh2, ffn_w_gu[layer],
                                   ffn_w_down[layer])
    return rmsnorm(xl, final_norm_g)
```

```python
import functools

import jax
import jax.numpy as jnp
import numpy as np
from jax import lax
from jax.experimental import pallas as pl
from jax.experimental.pallas import tpu as pltpu

F32 = jnp.float32
BF16 = jnp.bfloat16

GRID_W = 64
CHUNK = 64
EPS = 1e-6
ML_DK, ML_DV = 128, 256
DN_DK, DN_DV = 128, 128
GLA_DK, GLA_DV = 128, 256
RG_C = 8.0
GLA_TAU = 16.0

LANES = 128
VMEM_LIMIT = 48 * 1024 * 1024


def _tile(n, pref):
    t = min(n, pref)
    while n % t:
        t //= 2
    return t


def _params(sem):
    return pltpu.CompilerParams(dimension_semantics=sem, vmem_limit_bytes=VMEM_LIMIT)


def _modulated_norm(x, g, sc, sh):
    y = x * lax.rsqrt(jnp.mean(x * x, axis=-1, keepdims=True) + EPS) * g
    return y * (1.0 + sc) + sh


def _adaln_kernel(s_ref, w_ref, b_ref, o_ref):
    s = s_ref[...]
    s = (s * jax.nn.sigmoid(s)).astype(BF16)
    o_ref[...] = jnp.dot(s, w_ref[...].astype(BF16), preferred_element_type=F32) + b_ref[...]


def adaln_all(cond, w_mod, b_mod):
    R, D = cond.shape
    C = w_mod.shape[1]
    tn = _tile(C, 1024)
    return pl.pallas_call(
        _adaln_kernel,
        grid=(C // tn,),
        in_specs=[pl.BlockSpec((R, D), lambda j: (0, 0)),
                  pl.BlockSpec((D, tn), lambda j: (0, j)),
                  pl.BlockSpec((1, tn), lambda j: (0, j))],
        out_specs=pl.BlockSpec((R, tn), lambda j: (0, j)),
        out_shape=jax.ShapeDtypeStruct((R, C), F32),
        compiler_params=_params(("parallel",)),
    )(cond, w_mod, b_mod.reshape(1, C))


def _inproj_kernel(x_ref, g_ref, sc_ref, sh_ref, w_ref, ws_ref, o_ref, os_ref, h_ref):
    @pl.when(pl.program_id(2) == 0)
    def _():
        h = _modulated_norm(x_ref[0], g_ref[...], sc_ref[0], sh_ref[0]).astype(BF16)
        h_ref[...] = h
        os_ref[0] = jnp.dot(h, ws_ref[...], preferred_element_type=F32)

    o_ref[0] = jnp.dot(h_ref[...], w_ref[...], preferred_element_type=F32)


def inproj(x, g, sc, sh, w_main, w_small):
    B, T, D = x.shape
    C = w_main.shape[1]
    Cs = w_small.shape[1]
    tm, tn = _tile(T, 512), _tile(C, 1024)
    mod = (lambda b, i, j: (b, 0, 0)) if sc.shape[0] == B else (lambda b, i, j: (0, 0, 0))
    return pl.pallas_call(
        _inproj_kernel,
        grid=(B, T // tm, C // tn),
        in_specs=[pl.BlockSpec((1, tm, D), lambda b, i, j: (b, i, 0)),
                  pl.BlockSpec((1, D), lambda b, i, j: (0, 0)),
                  pl.BlockSpec((1, 1, D), mod),
                  pl.BlockSpec((1, 1, D), mod),
                  pl.BlockSpec((D, tn), lambda b, i, j: (0, j)),
                  pl.BlockSpec((D, Cs), lambda b, i, j: (0, 0))],
        out_specs=[pl.BlockSpec((1, tm, tn), lambda b, i, j: (b, i, j)),
                   pl.BlockSpec((1, tm, Cs), lambda b, i, j: (b, i, 0))],
        out_shape=[jax.ShapeDtypeStruct((B, T, C), F32), jax.ShapeDtypeStruct((B, T, Cs), F32)],
        scratch_shapes=[pltpu.VMEM((tm, D), BF16)],
        compiler_params=_params(("parallel", "parallel", "arbitrary")),
    )(x, g.reshape(1, D), sc, sh, w_main, w_small)


def _outproj_kernel(y_ref, x_ref, gate_ref, w_ref, o_ref):
    y = jnp.dot(y_ref[0].astype(BF16), w_ref[...], preferred_element_type=F32)
    o_ref[0] = x_ref[0] + gate_ref[0] * y


def outproj(y, x, gate, w):
    B, T, D = x.shape
    K = y.shape[-1]
    tm = _tile(T, 256)
    mod = (lambda b, i: (b, 0, 0)) if gate.shape[0] == B else (lambda b, i: (0, 0, 0))
    return pl.pallas_call(
        _outproj_kernel,
        grid=(B, T // tm),
        in_specs=[pl.BlockSpec((1, tm, K), lambda b, i: (b, i, 0)),
                  pl.BlockSpec((1, tm, D), lambda b, i: (b, i, 0)),
                  pl.BlockSpec((1, 1, D), mod),
                  pl.BlockSpec((K, D), lambda b, i: (0, 0))],
        out_specs=pl.BlockSpec((1, tm, D), lambda b, i: (b, i, 0)),
        out_shape=jax.ShapeDtypeStruct((B, T, D), F32),
        compiler_params=_params(("parallel", "parallel")),
    )(y, x, gate, w)


def _ffn_kernel(x_ref, g_ref, sc_ref, sh_ref, gate_ref, wg_ref, wu_ref, wd_ref, fg_ref, o_ref, h_ref,
                *, final_norm):
    k = pl.program_id(2)

    @pl.when(k == 0)
    def _():
        h_ref[...] = _modulated_norm(x_ref[0], g_ref[...], sc_ref[0], sh_ref[0]).astype(BF16)

    h = h_ref[...]
    a = jnp.dot(h, wg_ref[...], preferred_element_type=F32)
    u = jnp.dot(h, wu_ref[...], preferred_element_type=F32)
    act = (a * jax.nn.sigmoid(a) * u).astype(BF16)
    part = jnp.dot(act, wd_ref[...], preferred_element_type=F32)

    @pl.when(k == 0)
    def _():
        o_ref[0] = part

    @pl.when(k > 0)
    def _():
        o_ref[0] += part

    @pl.when(k == pl.num_programs(2) - 1)
    def _():
        r = x_ref[0] + gate_ref[0] * o_ref[0]
        if final_norm:
            r = r * lax.rsqrt(jnp.mean(r * r, axis=-1, keepdims=True) + EPS) * fg_ref[...]
        o_ref[0] = r


def ffn(x, g, sc, sh, gate, w_gu, w_down, final_g=None):
    B, T, D = x.shape
    Fh = w_down.shape[0]
    tm, tf = _tile(T, 512), _tile(Fh, 512)
    nf = Fh // tf
    mod = (lambda b, i, k: (b, 0, 0)) if sc.shape[0] == B else (lambda b, i, k: (0, 0, 0))
    fg = jnp.ones((1, D), F32) if final_g is None else final_g.reshape(1, D)
    return pl.pallas_call(
        functools.partial(_ffn_kernel, final_norm=final_g is not None),
        grid=(B, T // tm, nf),
        in_specs=[pl.BlockSpec((1, tm, D), lambda b, i, k: (b, i, 0)),
                  pl.BlockSpec((1, D), lambda b, i, k: (0, 0)),
                  pl.BlockSpec((1, 1, D), mod),
                  pl.BlockSpec((1, 1, D), mod),
                  pl.BlockSpec((1, 1, D), mod),
                  pl.BlockSpec((D, tf), lambda b, i, k: (0, k)),
                  pl.BlockSpec((D, tf), lambda b, i, k: (0, k + nf)),
                  pl.BlockSpec((tf, D), lambda b, i, k: (k, 0)),
                  pl.BlockSpec((1, D), lambda b, i, k: (0, 0))],
        out_specs=pl.BlockSpec((1, tm, D), lambda b, i, k: (b, i, 0)),
        out_shape=jax.ShapeDtypeStruct((B, T, D), F32),
        scratch_shapes=[pltpu.VMEM((tm, D), BF16)],
        compiler_params=_params(("parallel", "parallel", "arbitrary")),
    )(x, g.reshape(1, D), sc, sh, gate, w_gu, w_gu, w_down, fg)


def _to_heads(x, h):
    B, T, _ = x.shape
    return jnp.transpose(x.reshape(B, T, h, -1), (0, 2, 1, 3))


def _dir_gates(z, h):
    B, T, _ = z.shape
    return jnp.transpose(z.reshape(B, T, 2, h), (2, 0, 3, 1))


def _head_rmsnorm(o, g):
    B, H, T, d = o.shape
    o = jnp.transpose(o, (0, 2, 1, 3))
    o = o * lax.rsqrt(jnp.mean(o * o, axis=-1, keepdims=True) + EPS)
    return o.reshape(B, T, H * d) * g


def _l2norm(x):
    return x * lax.rsqrt(jnp.sum(x * x, axis=-1, keepdims=True) + EPS)


def _dwconv_seq(x, w):
    T = x.shape[1]
    cw = w.shape[0]
    pad = cw // 2
    xp = jnp.pad(x, ((0, 0), (pad, pad), (0, 0)))
    out = xp[:, 0:T] * w[0]
    for j in range(1, cw):
        out = out + xp[:, j:j + T] * w[j]
    return out


def _dwconv_grid(x, w):
    B, T, C = x.shape
    return _dwconv_seq(x.reshape(B * (T // GRID_W), GRID_W, C), w).reshape(B, T, C)


def _mlstm_chunked(q, k, v, ig, lf, state):
    B, H, T, dk = q.shape
    dv = v.shape[-1]
    N, L = T // CHUNK, CHUNK
    qc = (q * dk ** -0.5).reshape(B, H, N, L, dk)
    kc = k.reshape(B, H, N, L, dk)
    vc = v.reshape(B, H, N, L, dv)
    igc = ig.reshape(B, H, N, L)
    b = jnp.cumsum(lf.reshape(B, H, N, L), axis=-1)
    w_log = b[..., -1:] - b + igc
    m_loc = jnp.max(w_log, axis=-1)
    w = jnp.exp(w_log - m_loc[..., None])

    def step(carry, inp):
        C, n, m = carry
        q_i, w_i, v_i, k_i, m_i, bL_i = inp
        qC = jnp.einsum('bhld,bhed->bhle', q_i, C)
        qn = jnp.einsum('bhld,bhd->bhl', q_i, n)
        m_new = jnp.maximum(bL_i + m, m_i)
        a = jnp.exp(bL_i + m - m_new)
        s = jnp.exp(m_i - m_new)
        C = a[..., None, None] * C + s[..., None, None] * jnp.einsum('bhl,bhle,bhld->bhed', w_i, v_i, k_i)
        n = a[..., None] * n + s[..., None] * jnp.einsum('bhl,bhld->bhd', w_i, k_i)
        return (C, n, m_new), (qC, qn, m)

    xs = tuple(jnp.moveaxis(t, 2, 0) for t in (qc, w, vc, kc, m_loc, b[..., -1]))
    final, (qC, qn, m_in) = lax.scan(step, state, xs)
    qC = jnp.moveaxis(qC, 0, 2)
    qn = jnp.moveaxis(qn, 0, 2)
    m_in = jnp.moveaxis(m_in, 0, 2)
    tril = jnp.tril(jnp.ones((L, L), bool))
    inter_log = b + m_in[..., None]
    d_log = jnp.where(tril, b[..., :, None] - b[..., None, :] + igc[..., None, :], -jnp.inf)
    m_out = jnp.maximum(jnp.max(d_log, axis=-1), inter_log)
    s = jnp.einsum('bhnld,bhnsd->bhnls', qc, kc) * jnp.exp(d_log - m_out[..., None])
    e_inter = jnp.exp(inter_log - m_out)
    num = jnp.einsum('bhnls,bhnse->bhnle', s, vc) + e_inter[..., None] * qC
    den = jnp.sum(s, axis=-1) + e_inter * qn
    h = num / jnp.maximum(jnp.abs(den), jnp.exp(-m_out))[..., None]
    return h.reshape(B, H, T, dv), final


def _gated_delta_chunked(q, k, v, g, beta, state):
    B, H, T, dk = q.shape
    dv = v.shape[-1]
    N, L = T // CHUNK, CHUNK
    qc = (q * dk ** -0.5).reshape(B, H, N, L, dk)
    kc = k.reshape(B, H, N, L, dk)
    vc = v.reshape(B, H, N, L, dv)
    bc = beta.reshape(B, H, N, L)
    gc = jnp.cumsum(g.reshape(B, H, N, L), axis=-1)
    tril = jnp.tril(jnp.ones((L, L), bool))
    strict = jnp.tril(jnp.ones((L, L), bool), -1)
    decay = jnp.exp(jnp.where(tril, gc[..., :, None] - gc[..., None, :], -jnp.inf))
    kb = kc * bc[..., None]
    a = jnp.where(strict, jnp.einsum('bhnld,bhnsd->bhnls', kb, kc) * decay, 0.0)
    eye = jnp.eye(L, dtype=jnp.float32)
    tm = lax.linalg.triangular_solve(eye + a, jnp.broadcast_to(eye, a.shape), left_side=True, lower=True,
                                     unit_diagonal=True)
    u = tm @ (vc * bc[..., None])
    wk = tm @ (kb * jnp.exp(gc)[..., None])
    qk = jnp.einsum('bhnld,bhnsd->bhnls', qc, kc) * decay
    qg = qc * jnp.exp(gc)[..., None]
    kd = kc * jnp.exp(gc[..., -1:] - gc)[..., None]

    def step(S, inp):
        qg_i, qk_i, u_i, w_i, kd_i, gL_i = inp
        v_new = u_i - w_i @ S
        o = qg_i @ S + qk_i @ v_new
        S = jnp.exp(gL_i)[..., None, None] * S + jnp.einsum('bhld,bhle->bhde', kd_i, v_new)
        return S, o

    xs = tuple(jnp.moveaxis(t, 2, 0) for t in (qg, qk, u, wk, kd, gc[..., -1]))
    S, o = lax.scan(step, state, xs)
    return jnp.moveaxis(o, 0, 2).reshape(B, H, T, dv), S


def _gla_chunked(q, k, v, lg, state):
    B, H, T, dk = q.shape
    dv = v.shape[-1]
    N, L = T // CHUNK, CHUNK
    qc = (q * dk ** -0.5).reshape(B, H, N, L, dk)
    kc = k.reshape(B, H, N, L, dk)
    vc = v.reshape(B, H, N, L, dv)
    b = jnp.cumsum(lg.reshape(B, H, N, L, dk), axis=-2)
    qe = qc * jnp.exp(b)
    ke = kc * jnp.exp(-b)
    kd = kc * jnp.exp(b[..., -1:, :] - b)
    tril = jnp.tril(jnp.ones((L, L), bool))
    att = jnp.where(tril, jnp.einsum('bhnld,bhnsd->bhnls', qe, ke), 0.0)
    intra = att @ vc

    def step(S, inp):
        qe_i, kd_i, v_i, bL_i = inp
        o = qe_i @ S
        S = jnp.exp(bL_i)[..., None] * S + jnp.einsum('bhld,bhle->bhde', kd_i, v_i)
        return S, o

    xs = tuple(jnp.moveaxis(t, 2, 0) for t in (qe, kd, vc, b[..., -1, :]))
    S, inter = lax.scan(step, state, xs)
    o = intra + jnp.moveaxis(inter, 0, 2)
    return o.reshape(B, H, T, dv), S


def _lru_scan(a, bx, h0):
    def comb(l, r):
        return l[0] * r[0], r[0] * l[1] + r[1]
    acum, hcum = lax.associative_scan(comb, (a, bx), axis=1)
    h = hcum + acum * h0[:, None, :]
    return h, h[:, -1]


def _bidirectional(run, ctx_shared, ctx_dir, lat_shared, lat_dir, state0, axis):
    fl = lambda t: jnp.flip(t, axis)
    oc_f, s_f = run(*ctx_shared, *[t[0] for t in ctx_dir], state0)
    ol_f, _ = run(*lat_shared, *[t[0] for t in lat_dir], s_f)
    oc_b, s_b = run(*[fl(t) for t in ctx_shared], *[fl(t[1]) for t in ctx_dir], state0)
    ol_b, _ = run(*[fl(t) for t in lat_shared], *[fl(t[1]) for t in lat_dir], s_b)
    return oc_f + fl(oc_b), ol_f + fl(ol_b)


def _split(y, sizes):
    idx = [int(v) for v in np.cumsum(sizes)[:-1]]
    return jnp.split(y, idx, axis=-1)


def _mix0(pm_c, ps_c, pm_l, ps_l, p, need_ctx):
    mlh, dnh = p['ml_ig_b'].shape[-1], p['dn_a_log'].shape[-1]
    main_sizes = (mlh * ML_DK, mlh * ML_DK, mlh * ML_DV, mlh * ML_DV,
                  dnh * DN_DK, dnh * DN_DK, dnh * DN_DV, dnh * DN_DV)
    small_sizes = (2 * mlh, 2 * mlh, 2 * dnh, 2 * dnh)

    def prep(pm, ps, conv):
        mq, mk, mv, mo, dq, dk, dv, dz = _split(pm, main_sizes)
        mi, mf, da, db = _split(ps[..., :sum(small_sizes)], small_sizes)
        ml = (_to_heads(mq, mlh), _to_heads(mk, mlh), _to_heads(mv, mlh))
        ml_dir = (_dir_gates(mi, mlh) + p['ml_ig_b'][:, None, :, None],
                  jax.nn.log_sigmoid(_dir_gates(mf, mlh) + p['ml_fg_b'][:, None, :, None]))
        qkv = jax.nn.silu(conv(jnp.concatenate([dq, dk, dv], axis=-1), p['dn_conv_w']))
        dq, dk, dv = _split(qkv, (dnh * DN_DK, dnh * DN_DK, dnh * DN_DV))
        dn = (_l2norm(_to_heads(dq, dnh)), _l2norm(_to_heads(dk, dnh)), _to_heads(dv, dnh))
        dt = jax.nn.softplus(_dir_gates(da, dnh) + p['dn_dt_bias'][:, None, :, None])
        dn_dir = (-jnp.exp(p['dn_a_log'])[:, None, :, None] * dt, jax.nn.sigmoid(_dir_gates(db, dnh)))
        return ml, ml_dir, mo, dn, dn_dir, dz

    pc = prep(pm_c, ps_c, _dwconv_seq)
    pq = prep(pm_l, ps_l, _dwconv_grid)
    B = pm_c.shape[0]
    ml0 = (jnp.zeros((B, mlh, ML_DV, ML_DK), F32), jnp.zeros((B, mlh, ML_DK), F32), jnp.zeros((B, mlh), F32))
    ml_c, ml_l = _bidirectional(_mlstm_chunked, pc[0], pc[1], pq[0], pq[1], ml0, 2)
    dn0 = jnp.zeros((B, dnh, DN_DK, DN_DV), F32)
    dn_c, dn_l = _bidirectional(_gated_delta_chunked, pc[3], pc[4], pq[3], pq[4], dn0, 2)

    def merge(ml, mo, dn, dz):
        return jnp.concatenate([_head_rmsnorm(ml, p['ml_norm_g']) * jax.nn.sigmoid(mo),
                                _head_rmsnorm(dn, p['dn_norm_g']) * jax.nn.silu(dz)], axis=-1)

    yl = merge(ml_l, pq[2], dn_l, pq[5])
    yc = merge(ml_c, pc[2], dn_c, pc[5]) if need_ctx else None
    return yc, yl


def _mix1(pm_c, ps_c, pm_l, ps_l, p, need_ctx):
    rgw = p['rg_conv_b'].shape[-1]
    nblk = p['rg_w_r'].shape[1]
    rank = p['gla_w_up'].shape[1]
    glh = p['gla_w_up'].shape[-1] // GLA_DK
    main_sizes = (rgw, rgw, glh * GLA_DK, glh * GLA_DK, glh * GLA_DV, glh * GLA_DV)

    def prep(pm, ps, conv):
        B, T, _ = pm.shape
        xr, gr, q, k, v, og = _split(pm, main_sizes)
        lr = ps[..., :2 * rank]
        xc = conv(xr, p['rg_conv_w']) + p['rg_conv_b']
        xb = xc.reshape(B, T, nblk, rgw // nblk)
        r = jax.nn.sigmoid(jnp.einsum('btgi,zgij->zbtgj', xb, p['rg_w_r']).reshape(2, B, T, rgw)
                           + p['rg_b_r'][:, None, None, :])
        i = jax.nn.sigmoid(jnp.einsum('btgi,zgij->zbtgj', xb, p['rg_w_i']).reshape(2, B, T, rgw)
                           + p['rg_b_i'][:, None, None, :])
        log_a = -RG_C * r * jax.nn.softplus(-p['rg_lambda'])[:, None, None, :]
        rg_dir = (jnp.exp(log_a), jnp.sqrt(-jnp.expm1(2.0 * log_a)) * (i * xc))
        z = jnp.einsum('btzr,zrk->zbtk', lr.reshape(B, T, 2, rank), p['gla_w_up']) + p['gla_b_up'][:, None, None, :]
        lg = jnp.transpose((jax.nn.log_sigmoid(z) / GLA_TAU).reshape(2, B, T, glh, GLA_DK), (0, 1, 3, 2, 4))
        gla = (_to_heads(q, glh), _to_heads(k, glh), _to_heads(v, glh))
        return rg_dir, gr, gla, (lg,), og

    pc = prep(pm_c, ps_c, _dwconv_seq)
    pq = prep(pm_l, ps_l, _dwconv_grid)
    B = pm_c.shape[0]
    rg_c, rg_l = _bidirectional(_lru_scan, (), pc[0], (), pq[0], jnp.zeros((B, rgw), F32), 1)
    gla0 = jnp.zeros((B, glh, GLA_DK, GLA_DV), F32)
    gla_c, gla_l = _bidirectional(_gla_chunked, pc[2], pc[3], pq[2], pq[3], gla0, 2)

    def merge(rg, gr, gla, og):
        return jnp.concatenate([jax.nn.gelu(gr) * rg, _head_rmsnorm(gla, p['gla_norm_g']) * jax.nn.silu(og)],
                               axis=-1)

    yl = merge(rg_l, pq[1], gla_l, pq[4])
    yc = merge(rg_c, pc[1], gla_c, pc[4]) if need_ctx else None
    return yc, yl


def _pad_cols(w, mult):
    pad = (-w.shape[-1]) % mult
    return jnp.pad(w, ((0, 0), (0, pad)))


def kernel(x, c, ctx, c_ctx, mod_w, mod_b, norm_mix_g, norm_ffn_g, ffn_w_gu, ffn_w_down, final_norm_g,
           ab_w_in, ab_w_out, ml_ig_b, ml_fg_b, ml_norm_g, dn_conv_w, dn_a_log, dn_dt_bias, dn_norm_g,
           cd_w_in, cd_w_out, rg_conv_w, rg_conv_b, rg_w_r, rg_b_r, rg_w_i, rg_b_i, rg_lambda,
           gla_w_up, gla_b_up, gla_norm_g):
    B, T, D = x.shape
    depth = mod_w.shape[0]
    mlh, dnh = ml_ig_b.shape[-1], dn_a_log.shape[-1]
    rgw = rg_conv_b.shape[-1]
    rank = gla_w_up.shape[2]
    glh = gla_w_up.shape[-1] // GLA_DK

    cond = jnp.concatenate([c, c_ctx[None, :], jnp.zeros((8 - B - 1, D), F32)], axis=0)

    xc, xl = ctx, x
    for layer in range(depth):
        last = layer == depth - 1
        j = layer // 2
        m = adaln_all(cond, mod_w[layer], mod_b[layer])
        lat = [t[:, None, :] for t in jnp.split(m[:B], 6, axis=-1)]
        cx = [t[:, None, :] for t in jnp.split(m[B:B + 1], 6, axis=-1)]

        if layer % 2 == 0:
            w_in = ab_w_in[j]
            s = np.cumsum((0, mlh * ML_DK, mlh * ML_DK, mlh * ML_DV, mlh * ML_DV, 2 * mlh, 2 * mlh,
                           dnh * DN_DK, dnh * DN_DK, dnh * DN_DV, dnh * DN_DV, 2 * dnh, 2 * dnh))
            w_main = jnp.concatenate([w_in[:, s[0]:s[4]], w_in[:, s[6]:s[10]]], axis=1).astype(BF16)
            w_small = _pad_cols(jnp.concatenate([w_in[:, s[4]:s[6]], w_in[:, s[10]:s[12]]], axis=1), LANES).astype(BF16)
            w_out = ab_w_out[j].astype(BF16)
            p = dict(ml_ig_b=ml_ig_b[j], ml_fg_b=ml_fg_b[j], ml_norm_g=ml_norm_g[j], dn_conv_w=dn_conv_w[j],
                     dn_a_log=dn_a_log[j], dn_dt_bias=dn_dt_bias[j], dn_norm_g=dn_norm_g[j])
            mix = _mix0
        else:
            w_in = cd_w_in[j]
            nmain = 2 * rgw + 2 * glh * GLA_DK + 2 * glh * GLA_DV
            w_main = w_in[:, :nmain].astype(BF16)
            w_small = _pad_cols(w_in[:, nmain:], LANES).astype(BF16)
            w_out = cd_w_out[j].astype(BF16)
            p = dict(rg_conv_w=rg_conv_w[j], rg_conv_b=rg_conv_b[j], rg_w_r=rg_w_r[j], rg_b_r=rg_b_r[j],
                     rg_w_i=rg_w_i[j], rg_b_i=rg_b_i[j], rg_lambda=rg_lambda[j], gla_w_up=gla_w_up[j],
                     gla_b_up=gla_b_up[j], gla_norm_g=gla_norm_g[j])
            mix = _mix1

        pm_l, ps_l = inproj(xl, norm_mix_g[layer], lat[1], lat[0], w_main, w_small)
        pm_c, ps_c = inproj(xc, norm_mix_g[layer], cx[1], cx[0], w_main, w_small)
        yc, yl = mix(pm_c, ps_c, pm_l, ps_l, p, not last)

        w_gu = ffn_w_gu[layer].astype(BF16)
        w_dn = ffn_w_down[layer].astype(BF16)
        xl = outproj(yl, xl, lat[2], w_out)
        xl = ffn(xl, norm_ffn_g[layer], lat[4], lat[3], lat[5], w_gu, w_dn, final_norm_g if last else None)
        if not last:
            xc = outproj(yc, xc, cx[2], w_out)
            xc = ffn(xc, norm_ffn_g[layer], cx[4], cx[3], cx[5], w_gu, w_dn)
    return xl
```

```python
import functools

import jax
import jax.numpy as jnp
import numpy as np
from jax import lax
from jax.experimental import pallas as pl
from jax.experimental.pallas import tpu as pltpu

F32 = jnp.float32
BF16 = jnp.bfloat16

GRID_W = 64
CHUNK = 64
EPS = 1e-6
ML_DK, ML_DV = 128, 256
DN_DK, DN_DV = 128, 128
GLA_DK, GLA_DV = 128, 256
RG_C = 8.0
GLA_TAU = 16.0

LANES = 128
VMEM_LIMIT = 48 * 1024 * 1024

NT = (((1,), (1,)), ((), ()))
TN = (((0,), (0,)), ((), ()))


def _tile(n, pref):
    t = min(n, pref)
    while n % t:
        t //= 2
    return t


def _params(sem):
    return pltpu.CompilerParams(dimension_semantics=sem, vmem_limit_bytes=VMEM_LIMIT)


def _dot(a, b, dims=None):
    if dims is None:
        return jnp.dot(a, b, preferred_element_type=F32)
    return lax.dot_general(a, b, dims, preferred_element_type=F32)


def _modulated_norm(x, g, sc, sh):
    y = x * lax.rsqrt(jnp.mean(x * x, axis=-1, keepdims=True) + EPS) * g
    return y * (1.0 + sc) + sh


def _adaln_kernel(s_ref, w_ref, b_ref, o_ref):
    s = s_ref[...]
    s = (s * jax.nn.sigmoid(s)).astype(BF16)
    o_ref[...] = _dot(s, w_ref[...].astype(BF16)) + b_ref[...]


def adaln_all(cond, w_mod, b_mod):
    R, D = cond.shape
    C = w_mod.shape[1]
    tn = _tile(C, 1024)
    return pl.pallas_call(
        _adaln_kernel,
        grid=(C // tn,),
        in_specs=[pl.BlockSpec((R, D), lambda j: (0, 0)),
                  pl.BlockSpec((D, tn), lambda j: (0, j)),
                  pl.BlockSpec((1, tn), lambda j: (0, j))],
        out_specs=pl.BlockSpec((R, tn), lambda j: (0, j)),
        out_shape=jax.ShapeDtypeStruct((R, C), F32),
        name="adaln_all",
        compiler_params=_params(("parallel",)),
    )(cond, w_mod, b_mod.reshape(1, C))


def _inproj_kernel(x_ref, g_ref, sc_ref, sh_ref, w_ref, ws_ref, o_ref, os_ref, h_ref):
    @pl.when(pl.program_id(2) == 0)
    def _():
        h = _modulated_norm(x_ref[0], g_ref[...], sc_ref[0], sh_ref[0]).astype(BF16)
        h_ref[...] = h
        os_ref[0] = _dot(h, ws_ref[...])

    o_ref[0] = _dot(h_ref[...], w_ref[...])


def inproj(x, g, sc, sh, w_main, w_small):
    B, T, D = x.shape
    C = w_main.shape[1]
    Cs = w_small.shape[1]
    tm, tn = _tile(T, 512), _tile(C, 1024)
    mod = (lambda b, i, j: (b, 0, 0)) if sc.shape[0] == B else (lambda b, i, j: (0, 0, 0))
    return pl.pallas_call(
        _inproj_kernel,
        grid=(B, T // tm, C // tn),
        in_specs=[pl.BlockSpec((1, tm, D), lambda b, i, j: (b, i, 0)),
                  pl.BlockSpec((1, D), lambda b, i, j: (0, 0)),
                  pl.BlockSpec((1, 1, D), mod),
                  pl.BlockSpec((1, 1, D), mod),
                  pl.BlockSpec((D, tn), lambda b, i, j: (0, j)),
                  pl.BlockSpec((D, Cs), lambda b, i, j: (0, 0))],
        out_specs=[pl.BlockSpec((1, tm, tn), lambda b, i, j: (b, i, j)),
                   pl.BlockSpec((1, tm, Cs), lambda b, i, j: (b, i, 0))],
        out_shape=[jax.ShapeDtypeStruct((B, T, C), F32), jax.ShapeDtypeStruct((B, T, Cs), F32)],
        scratch_shapes=[pltpu.VMEM((tm, D), BF16)],
        name="inproj",
        compiler_params=_params(("parallel", "parallel", "arbitrary")),
    )(x, g.reshape(1, D), sc, sh, w_main, w_small)


def _head_norm(o, d, g):
    parts = []
    for h in range(o.shape[-1] // d):
        oh = o[:, h * d:(h + 1) * d]
        parts.append(oh * lax.rsqrt(jnp.mean(oh * oh, axis=-1, keepdims=True) + EPS))
    return jnp.concatenate(parts, axis=-1) * g


def _merge_kernel(af_ref, ab_ref, bf_ref, bb_ref, ga_ref, gb_ref, na_ref, nb_ref, x_ref, gate_ref, w_ref, o_ref,
                  *, kind, da, db):
    a = af_ref[0] + ab_ref[0]
    b = bf_ref[0] + bb_ref[0]
    if kind == 0:
        ya = _head_norm(a, da, na_ref[...]) * jax.nn.sigmoid(ga_ref[0])
    else:
        ya = jax.nn.gelu(ga_ref[0]) * a
    gb = gb_ref[0]
    yb = _head_norm(b, db, nb_ref[...]) * (gb * jax.nn.sigmoid(gb))
    wa = a.shape[-1]
    y = _dot(ya.astype(BF16), w_ref[0:wa, :]) + _dot(yb.astype(BF16), w_ref[wa:, :])
    o_ref[0] = x_ref[0] + gate_ref[0] * y


def merge_outproj(kind, a_dirs, b_dirs, pm, ga_off, gb_off, na, nb, da, db, x, gate, w):
    B, T, D = x.shape
    wa, wb = a_dirs[0].shape[-1], b_dirs[0].shape[-1]
    assert ga_off % wa == 0 and gb_off % wb == 0
    tm = _tile(T, 256)
    mod = (lambda b, i: (b, 0, 0)) if gate.shape[0] == B else (lambda b, i: (0, 0, 0))
    tok = lambda width, col: pl.BlockSpec((1, tm, width), lambda b, i: (b, i, col))
    return pl.pallas_call(
        functools.partial(_merge_kernel, kind=kind, da=da, db=db),
        grid=(B, T // tm),
        in_specs=[tok(wa, 0), tok(wa, 0), tok(wb, 0), tok(wb, 0), tok(wa, ga_off // wa), tok(wb, gb_off // wb),
                  pl.BlockSpec((1, wa), lambda b, i: (0, 0)),
                  pl.BlockSpec((1, wb), lambda b, i: (0, 0)),
                  tok(D, 0),
                  pl.BlockSpec((1, 1, D), mod),
                  pl.BlockSpec((wa + wb, D), lambda b, i: (0, 0))],
        out_specs=tok(D, 0),
        out_shape=jax.ShapeDtypeStruct((B, T, D), F32),
        name="merge_outproj",
        compiler_params=_params(("parallel", "parallel")),
    )(a_dirs[0], a_dirs[1], b_dirs[0], b_dirs[1], pm, pm, na.reshape(1, wa), nb.reshape(1, wb), x, gate, w)


def _ffn_kernel(x_ref, g_ref, sc_ref, sh_ref, gate_ref, wg_ref, wu_ref, wd_ref, fg_ref, o_ref, h_ref,
                *, final_norm):
    k = pl.program_id(2)

    @pl.when(k == 0)
    def _():
        h_ref[...] = _modulated_norm(x_ref[0], g_ref[...], sc_ref[0], sh_ref[0]).astype(BF16)

    h = h_ref[...]
    a = _dot(h, wg_ref[...])
    u = _dot(h, wu_ref[...])
    act = (a * jax.nn.sigmoid(a) * u).astype(BF16)
    part = _dot(act, wd_ref[...])

    @pl.when(k == 0)
    def _():
        o_ref[0] = part

    @pl.when(k > 0)
    def _():
        o_ref[0] += part

    @pl.when(k == pl.num_programs(2) - 1)
    def _():
        r = x_ref[0] + gate_ref[0] * o_ref[0]
        if final_norm:
            r = r * lax.rsqrt(jnp.mean(r * r, axis=-1, keepdims=True) + EPS) * fg_ref[...]
        o_ref[0] = r


def ffn(x, g, sc, sh, gate, w_gu, w_down, final_g=None):
    B, T, D = x.shape
    Fh = w_down.shape[0]
    tm, tf = _tile(T, 512), _tile(Fh, 512)
    nf = Fh // tf
    mod = (lambda b, i, k: (b, 0, 0)) if sc.shape[0] == B else (lambda b, i, k: (0, 0, 0))
    fg = jnp.ones((1, D), F32) if final_g is None else final_g.reshape(1, D)
    return pl.pallas_call(
        functools.partial(_ffn_kernel, final_norm=final_g is not None),
        grid=(B, T // tm, nf),
        in_specs=[pl.BlockSpec((1, tm, D), lambda b, i, k: (b, i, 0)),
                  pl.BlockSpec((1, D), lambda b, i, k: (0, 0)),
                  pl.BlockSpec((1, 1, D), mod),
                  pl.BlockSpec((1, 1, D), mod),
                  pl.BlockSpec((1, 1, D), mod),
                  pl.BlockSpec((D, tf), lambda b, i, k: (0, k)),
                  pl.BlockSpec((D, tf), lambda b, i, k: (0, k + nf)),
                  pl.BlockSpec((tf, D), lambda b, i, k: (k, 0)),
                  pl.BlockSpec((1, D), lambda b, i, k: (0, 0))],
        out_specs=pl.BlockSpec((1, tm, D), lambda b, i, k: (b, i, 0)),
        out_shape=jax.ShapeDtypeStruct((B, T, D), F32),
        scratch_shapes=[pltpu.VMEM((tm, D), BF16)],
        name="ffn",
        compiler_params=_params(("parallel", "parallel", "arbitrary")),
    )(x, g.reshape(1, D), sc, sh, gate, w_gu, w_gu, w_down, fg)


def _incl_mask(rev):
    r = lax.broadcasted_iota(jnp.int32, (CHUNK, CHUNK), 0)
    c = lax.broadcasted_iota(jnp.int32, (CHUNK, CHUNK), 1)
    return (r <= c) if rev else (r >= c)


def _split_bf16(x):
    hi = x.astype(BF16)
    return hi, (x - hi.astype(F32)).astype(BF16)


def _chunk_cumsum(mask_bf16, x):
    hi, lo = _split_bf16(x)
    return _dot(mask_bf16, hi) + _dot(mask_bf16, lo)


def _log_sigmoid(z):
    return jnp.minimum(z, 0.0) - jnp.log1p(jnp.exp(-jnp.abs(z)))


def _softplus(x):
    return jnp.maximum(x, 0.0) + jnp.log1p(jnp.exp(-jnp.abs(x)))


def _time_block(T):
    return _tile(T, 512)


def _dwconv(x, w_ref, seg):
    n = x.shape[0]
    taps = w_ref.shape[0]
    pos = lax.broadcasted_iota(jnp.int32, (n, 1), 0) % seg
    out = None
    for j in range(taps):
        sh = j - taps // 2
        xs = x if sh == 0 else pltpu.roll(x, (-sh) % n, axis=0)
        ok = (pos + sh >= 0) & (pos + sh < seg)
        term = jnp.where(ok, xs, 0.0) * w_ref[j:j + 1, :]
        out = term if out is None else out + term
    return out


def _gla_kernel(q_ref, k_ref, v_ref, ps_ref, wup_ref, bup_ref, s0_ref, o_ref, st_ref, lg_ref, *, rev, heads, nc):
    @pl.when(pl.program_id(1) == 0)
    def _():
        st_ref[...] = s0_ref[...]

    mask = _incl_mask(rev)
    mask_bf = jnp.where(mask, 1.0, 0.0).astype(BF16)
    z = _dot(ps_ref[0].astype(BF16), wup_ref[...]) + bup_ref[...]
    lg_ref[...] = _log_sigmoid(z) * (1.0 / GLA_TAU)

    def body(ci, carry):
        c = (nc - 1 - ci) if rev else ci
        rows = pl.ds(pl.multiple_of(c * CHUNK, CHUNK), CHUNK)
        lg = lg_ref[rows, :]
        b = _chunk_cumsum(mask_bf, lg)
        b_end = jnp.sum(lg, axis=0, keepdims=True)
        k = k_ref[0, rows, :]
        qe = (q_ref[0, rows, :] * (GLA_DK ** -0.5) * jnp.exp(b)).astype(BF16)
        ke = (k * jnp.exp(-b)).astype(BF16)
        kd = (k * jnp.exp(b_end - b)).astype(BF16)
        for h in range(heads):
            ks = slice(h * GLA_DK, (h + 1) * GLA_DK)
            vs = slice(h * GLA_DV, (h + 1) * GLA_DV)
            v = v_ref[0, rows, vs].astype(BF16)
            att = jnp.where(mask, _dot(qe[:, ks], ke[:, ks], NT), 0.0).astype(BF16)
            st = st_ref[0, h]
            o_ref[0, rows, vs] = _dot(att, v) + _dot(qe[:, ks], st.astype(BF16), NT)
            st_ref[0, h] = st * jnp.exp(b_end[:, ks]) + _dot(v, kd[:, ks], TN)
        return carry

    lax.fori_loop(0, nc, body, 0)


def gla_scan(pm, ps, wup, bup, s0, rev, q_off, k_off, v_off, heads):
    B, T, _ = pm.shape
    Tb = _time_block(T)
    NB = T // Tb
    kw, vw = heads * GLA_DK, heads * GLA_DV
    assert q_off % kw == 0 and k_off % kw == 0 and v_off % vw == 0
    blk = (lambda i: NB - 1 - i) if rev else (lambda i: i)
    return pl.pallas_call(
        functools.partial(_gla_kernel, rev=rev, heads=heads, nc=Tb // CHUNK),
        grid=(B, NB),
        in_specs=[pl.BlockSpec((1, Tb, kw), lambda b, i: (b, blk(i), q_off // kw)),
                  pl.BlockSpec((1, Tb, kw), lambda b, i: (b, blk(i), k_off // kw)),
                  pl.BlockSpec((1, Tb, vw), lambda b, i: (b, blk(i), v_off // vw)),
                  pl.BlockSpec((1, Tb, ps.shape[-1]), lambda b, i: (b, blk(i), 0)),
                  pl.BlockSpec(wup.shape, lambda b, i: (0, 0)),
                  pl.BlockSpec(bup.shape, lambda b, i: (0, 0)),
                  pl.BlockSpec((1,) + s0.shape[1:], lambda b, i: (b, 0, 0, 0))],
        out_specs=[pl.BlockSpec((1, Tb, vw), lambda b, i: (b, blk(i), 0)),
                   pl.BlockSpec((1,) + s0.shape[1:], lambda b, i: (b, 0, 0, 0))],
        out_shape=[jax.ShapeDtypeStruct((B, T, vw), F32), jax.ShapeDtypeStruct(s0.shape, F32)],
        scratch_shapes=[pltpu.VMEM((Tb, kw), F32)],
        name="gla_scan",
        compiler_params=_params(("parallel", "arbitrary")),
    )(pm, pm, pm, ps, wup, bup, s0)


def gla_bidir(pm_c, ps_c, pm_l, ps_l, w_up, b_up, q_off, k_off, v_off):
    B = pm_l.shape[0]
    rank, hk = w_up.shape[1], w_up.shape[2]
    heads = hk // GLA_DK
    out = []
    for d in (0, 1):
        wup = jnp.zeros((ps_l.shape[-1], hk), F32).at[d * rank:(d + 1) * rank].set(w_up[d]).astype(BF16)
        bup = b_up[d].reshape(1, hk)
        s0 = jnp.zeros((B, heads, GLA_DV, GLA_DK), F32)
        oc, s1 = gla_scan(pm_c, ps_c, wup, bup, s0, bool(d), q_off, k_off, v_off, heads)
        ol, _ = gla_scan(pm_l, ps_l, wup, bup, s1, bool(d), q_off, k_off, v_off, heads)
        out.append((oc, ol))
    return out


def _mlstm_kernel(q_ref, k_ref, v_ref, ps_ref, gb_ref, c0_ref, nm0_ref, o_ref, c_ref, nm_ref,
                  *, rev, heads, nc, ig_col, fg_col):
    @pl.when(pl.program_id(1) == 0)
    def _():
        c_ref[...] = c0_ref[...]
        nm_ref[...] = nm0_ref[...]

    mask = _incl_mask(rev)
    mask_bf = jnp.where(mask, 1.0, 0.0).astype(BF16)

    def body(ci, carry):
        c = (nc - 1 - ci) if rev else ci
        rows = pl.ds(pl.multiple_of(c * CHUNK, CHUNK), CHUNK)
        g = ps_ref[0, rows, :] + gb_ref[...]
        lf = _log_sigmoid(g)
        bcum = _chunk_cumsum(mask_bf, lf)
        b_end = jnp.sum(lf, axis=0, keepdims=True)
        bcum_t, g_t = bcum.T, g.T
        for h in range(heads):
            ic, fc = ig_col + h, fg_col + h
            ks = slice(h * ML_DK, (h + 1) * ML_DK)
            vs = slice(h * ML_DV, (h + 1) * ML_DV)
            b_col, ig_col_v = bcum[:, fc:fc + 1], g[:, ic:ic + 1]
            b_row, ig_row = bcum_t[fc:fc + 1, :], g_t[ic:ic + 1, :]
            b_l = b_end[:, fc:fc + 1]
            q = q_ref[0, rows, ks] * (ML_DK ** -0.5)
            k = k_ref[0, rows, ks]
            v = v_ref[0, rows, vs]
            qb, kb = q.astype(BF16), k.astype(BF16)
            cst = c_ref[0, h]
            n = nm_ref[0, h, 0:1, :]
            m = nm_ref[0, h, 1:2, 0:1]
            q_c = _dot(qb, cst.astype(BF16), NT)
            q_n = jnp.sum(q * n, axis=-1, keepdims=True)
            inter_log = b_col + m
            d_log = jnp.where(mask, b_col - b_row + ig_row, -jnp.inf)
            m_out = jnp.maximum(jnp.max(d_log, axis=-1, keepdims=True), inter_log)
            s = _dot(qb, kb, NT) * jnp.exp(d_log - m_out)
            e_inter = jnp.exp(inter_log - m_out)
            num = _dot(s.astype(BF16), v.astype(BF16)) + e_inter * q_c
            den = jnp.sum(s, axis=-1, keepdims=True) + e_inter * q_n
            o_ref[0, rows, vs] = num / jnp.maximum(jnp.abs(den), jnp.exp(-m_out))
            w_log = b_l - b_col + ig_col_v
            m_loc = jnp.max(w_log, axis=0, keepdims=True)
            w = jnp.exp(w_log - m_loc)
            m_new = jnp.maximum(b_l + m, m_loc)
            a = jnp.exp(b_l + m - m_new)
            sc = jnp.exp(m_loc - m_new)
            c_ref[0, h] = a * cst + sc * _dot((w * v).astype(BF16), kb, TN)
            nm_ref[0, h, 0:1, :] = a * n + sc * jnp.sum(w * k, axis=0, keepdims=True)
            nm_ref[0, h, 1:2, :] = jnp.broadcast_to(m_new, (1, ML_DK))
        return carry

    lax.fori_loop(0, nc, body, 0)


def mlstm_scan(pm, ps, gbias, state, rev, q_off, k_off, v_off, heads, ig_col, fg_col):
    B, T, _ = pm.shape
    Tb = _time_block(T)
    NB = T // Tb
    kw, vw = heads * ML_DK, heads * ML_DV
    assert q_off % kw == 0 and k_off % kw == 0 and v_off % vw == 0
    c0, nm0 = state
    blk = (lambda i: NB - 1 - i) if rev else (lambda i: i)
    st_spec = lambda a: pl.BlockSpec((1,) + a.shape[1:], lambda b, i: (b, 0, 0, 0))
    o, c1, nm1 = pl.pallas_call(
        functools.partial(_mlstm_kernel, rev=rev, heads=heads, nc=Tb // CHUNK, ig_col=ig_col, fg_col=fg_col),
        grid=(B, NB),
        in_specs=[pl.BlockSpec((1, Tb, kw), lambda b, i: (b, blk(i), q_off // kw)),
                  pl.BlockSpec((1, Tb, kw), lambda b, i: (b, blk(i), k_off // kw)),
                  pl.BlockSpec((1, Tb, vw), lambda b, i: (b, blk(i), v_off // vw)),
                  pl.BlockSpec((1, Tb, ps.shape[-1]), lambda b, i: (b, blk(i), 0)),
                  pl.BlockSpec(gbias.shape, lambda b, i: (0, 0)),
                  st_spec(c0), st_spec(nm0)],
        out_specs=[pl.BlockSpec((1, Tb, vw), lambda b, i: (b, blk(i), 0)), st_spec(c0), st_spec(nm0)],
        out_shape=[jax.ShapeDtypeStruct((B, T, vw), F32), jax.ShapeDtypeStruct(c0.shape, F32),
                   jax.ShapeDtypeStruct(nm0.shape, F32)],
        name="mlstm_scan",
        compiler_params=_params(("parallel", "arbitrary")),
    )(pm, pm, pm, ps, gbias, c0, nm0)
    return o, (c1, nm1)


def mlstm_bidir(pm_c, ps_c, pm_l, ps_l, ig_b, fg_b, q_off, k_off, v_off):
    B = pm_l.shape[0]
    heads = ig_b.shape[-1]
    ncol = ps_l.shape[-1]
    gbias = jnp.zeros((1, ncol), F32).at[0, :4 * heads].set(jnp.concatenate([ig_b.reshape(-1), fg_b.reshape(-1)]))
    out = []
    for d in (0, 1):
        st = (jnp.zeros((B, heads, ML_DV, ML_DK), F32), jnp.zeros((B, heads, 8, ML_DK), F32))
        args = (bool(d), q_off, k_off, v_off, heads, d * heads, 2 * heads + d * heads)
        oc, st = mlstm_scan(pm_c, ps_c, gbias, st, *args)
        ol, _ = mlstm_scan(pm_l, ps_l, gbias, st, *args)
        out.append((oc, ol))
    return out


def _dot_split(a, b):
    ah, al = _split_bf16(a)
    bh, bl = _split_bf16(b)
    return _dot(ah, bh) + (_dot(ah, bl) + _dot(al, bh))


def _unit_tri_inverse(a, strict):
    r = lax.broadcasted_iota(jnp.int32, (CHUNK, CHUNK), 0)
    c = lax.broadcasted_iota(jnp.int32, (CHUNK, CHUNK), 1)
    eye = jnp.where(r == c, 1.0, 0.0)
    t = eye - jnp.where(strict & (r // 2 == c // 2), a, 0.0)
    s = 2
    while s < CHUNK:
        a_off = jnp.where((r // (2 * s) == c // (2 * s)) & (r // s != c // s), a, 0.0)
        t = t - _dot_split(_dot_split(t, a_off), t)
        s *= 2
    return t


def _delta_kernel(q_ref, k_ref, v_ref, ps_ref, gp_ref, cw_ref, s0_ref, o_ref, st_ref, qc_ref, kc_ref, vc_ref,
                  *, rev, heads, nc, seg, g_col, beta_col):
    @pl.when(pl.program_id(1) == 0)
    def _():
        st_ref[...] = s0_ref[...]

    hw = heads * DN_DK
    conv = lambda ref, lo: jax.nn.silu(_dwconv(ref[0], cw_ref.at[:, lo:lo + ref.shape[-1]], seg))
    qa, ka = conv(q_ref, 0), conv(k_ref, hw)
    vc_ref[...] = conv(v_ref, 2 * hw)
    for h in range(heads):
        ks = slice(h * DN_DK, (h + 1) * DN_DK)
        qh, kh = qa[:, ks], ka[:, ks]
        qc_ref[:, ks] = qh * lax.rsqrt(jnp.sum(qh * qh, axis=-1, keepdims=True) + EPS) * (DN_DK ** -0.5)
        kc_ref[:, ks] = kh * lax.rsqrt(jnp.sum(kh * kh, axis=-1, keepdims=True) + EPS)

    mask = _incl_mask(rev)
    mask_bf = jnp.where(mask, 1.0, 0.0).astype(BF16)
    r = lax.broadcasted_iota(jnp.int32, (CHUNK, CHUNK), 0)
    c = lax.broadcasted_iota(jnp.int32, (CHUNK, CHUNK), 1)
    strict = mask & (r != c)

    def body(ci, carry):
        cidx = (nc - 1 - ci) if rev else ci
        rows = pl.ds(pl.multiple_of(cidx * CHUNK, CHUNK), CHUNK)
        pre = ps_ref[0, rows, :]
        g = gp_ref[1:2, :] * _softplus(pre + gp_ref[0:1, :])
        beta = jax.nn.sigmoid(pre)
        gcum = _chunk_cumsum(mask_bf, g)
        g_end = jnp.sum(g, axis=0, keepdims=True)
        gcum_t = gcum.T
        for h in range(heads):
            gc, bc = g_col + h, beta_col + h
            ks = slice(h * DN_DK, (h + 1) * DN_DK)
            vs = slice(h * DN_DV, (h + 1) * DN_DV)
            g_c, g_r = gcum[:, gc:gc + 1], gcum_t[gc:gc + 1, :]
            b_c = beta[:, bc:bc + 1]
            g_l = g_end[:, gc:gc + 1]
            q, k, v = qc_ref[rows, ks], kc_ref[rows, ks], vc_ref[rows, vs]
            decay = jnp.exp(jnp.where(mask, g_c - g_r, -jnp.inf))
            kbeta = k * b_c
            qb, kb = q.astype(BF16), k.astype(BF16)
            a = jnp.where(strict, _dot(kbeta.astype(BF16), kb, NT) * decay, 0.0)
            t = _unit_tri_inverse(a, strict).astype(BF16)
            u = _dot(t, (v * b_c).astype(BF16))
            wk = _dot(t, (kbeta * jnp.exp(g_c)).astype(BF16))
            qk = _dot(qb, kb, NT) * decay
            st = st_ref[0, h]
            stb = st.astype(BF16)
            v_new = (u - _dot(wk.astype(BF16), stb)).astype(BF16)
            o_ref[0, rows, vs] = _dot((q * jnp.exp(g_c)).astype(BF16), stb) + _dot(qk.astype(BF16), v_new)
            kd = (k * jnp.exp(g_l - g_c)).astype(BF16)
            st_ref[0, h] = jnp.exp(g_l) * st + _dot(kd, v_new, TN)
        return carry

    lax.fori_loop(0, nc, body, 0)


def delta_scan(pm, ps, gparams, conv_w, s0, rev, seg, q_off, k_off, v_off, heads, g_col, beta_col):
    B, T, _ = pm.shape
    Tb = _time_block(T)
    NB = T // Tb
    assert Tb % seg == 0 or (NB == 1 and seg == T)
    kw, vw = heads * DN_DK, heads * DN_DV
    assert q_off % kw == 0 and k_off % kw == 0 and v_off % vw == 0
    blk = (lambda i: NB - 1 - i) if rev else (lambda i: i)
    st_spec = pl.BlockSpec((1,) + s0.shape[1:], lambda b, i: (b, 0, 0, 0))
    return pl.pallas_call(
        functools.partial(_delta_kernel, rev=rev, heads=heads, nc=Tb // CHUNK, seg=seg, g_col=g_col,
                          beta_col=beta_col),
        grid=(B, NB),
        in_specs=[pl.BlockSpec((1, Tb, kw), lambda b, i: (b, blk(i), q_off // kw)),
                  pl.BlockSpec((1, Tb, kw), lambda b, i: (b, blk(i), k_off // kw)),
                  pl.BlockSpec((1, Tb, vw), lambda b, i: (b, blk(i), v_off // vw)),
                  pl.BlockSpec((1, Tb, ps.shape[-1]), lambda b, i: (b, blk(i), 0)),
                  pl.BlockSpec(gparams.shape, lambda b, i: (0, 0)),
                  pl.BlockSpec(conv_w.shape, lambda b, i: (0, 0)),
                  st_spec],
        out_specs=[pl.BlockSpec((1, Tb, vw), lambda b, i: (b, blk(i), 0)), st_spec],
        out_shape=[jax.ShapeDtypeStruct((B, T, vw), F32), jax.ShapeDtypeStruct(s0.shape, F32)],
        scratch_shapes=[pltpu.VMEM((Tb, kw), F32), pltpu.VMEM((Tb, kw), F32), pltpu.VMEM((Tb, vw), F32)],
        name="delta_scan",
        compiler_params=_params(("parallel", "arbitrary")),
    )(pm, pm, pm, ps, gparams, conv_w, s0)


def delta_bidir(pm_c, ps_c, pm_l, ps_l, conv_w, a_log, dt_bias, q_off, k_off, v_off, col0):
    B = pm_l.shape[0]
    heads = a_log.shape[-1]
    ncol = ps_l.shape[-1]
    gparams = jnp.zeros((8, ncol), F32)
    gparams = gparams.at[0, col0:col0 + 2 * heads].set(dt_bias.reshape(-1))
    gparams = gparams.at[1, col0:col0 + 2 * heads].set(-jnp.exp(a_log).reshape(-1))
    out = []
    for d in (0, 1):
        s0 = jnp.zeros((B, heads, DN_DK, DN_DV), F32)
        args = (q_off, k_off, v_off, heads, col0 + d * heads, col0 + 2 * heads + d * heads)
        oc, s1 = delta_scan(pm_c, ps_c, gparams, conv_w, s0, bool(d), pm_c.shape[1], *args)
        ol, _ = delta_scan(pm_l, ps_l, gparams, conv_w, s1, bool(d), GRID_W, *args)
        out.append((oc, ol))
    return out


def _rglru_kernel(x_ref, cw_ref, cb_ref, wr_ref, br_ref, wi_ref, bi_ref, lam_ref, h0_ref, o_ref, hT_ref,
                  *, rev, nc, seg):
    @pl.when(pl.program_id(2) == 0)
    def _():
        hT_ref[...] = h0_ref[...]

    n = x_ref.shape[1]
    xc = _dwconv(x_ref[0], cw_ref, seg) + cb_ref[...]
    xb = xc.astype(BF16)
    r = jax.nn.sigmoid(_dot(xb, wr_ref[0].astype(BF16)) + br_ref[...])
    i = jax.nn.sigmoid(_dot(xb, wi_ref[0].astype(BF16)) + bi_ref[...])
    log_a = -RG_C * r * _softplus(-lam_ref[...])
    a = jnp.exp(log_a)
    bx = jnp.sqrt(1.0 - jnp.exp(2.0 * log_a)) * (i * xc)

    pos = lax.broadcasted_iota(jnp.int32, (n, 1), 0) % CHUNK
    s = 1
    while s < CHUNK:
        ok = (pos < CHUNK - s) if rev else (pos >= s)
        shift = (n - s) if rev else s
        a_sh = pltpu.roll(a, shift, axis=0)
        b_sh = pltpu.roll(bx, shift, axis=0)
        bx = jnp.where(ok, a * b_sh + bx, bx)
        a = jnp.where(ok, a * a_sh, a)
        s *= 2

    h = hT_ref[0]
    for ci in range(nc):
        c = (nc - 1 - ci) if rev else ci
        rows = slice(c * CHUNK, (c + 1) * CHUNK)
        hc = bx[rows] + a[rows] * h
        o_ref[0, rows, :] = hc
        h = hc[0:1] if rev else hc[CHUNK - 1:CHUNK]
    hT_ref[0] = h


def rglru_scan(pm, conv_w, conv_b, w_r, b_r, w_i, b_i, lam, h0, rev, seg, width):
    B, T, _ = pm.shape
    nblk, bw = w_r.shape[0], w_r.shape[1]
    assert bw % LANES == 0 and nblk * bw == width
    Tb = _tile(T, 1024)
    NB = T // Tb
    assert Tb % seg == 0 or (NB == 1 and seg == T)
    blk = (lambda i: NB - 1 - i) if rev else (lambda i: i)
    row = pl.BlockSpec((1, bw), lambda b, g, i: (0, g))
    return pl.pallas_call(
        functools.partial(_rglru_kernel, rev=rev, nc=Tb // CHUNK, seg=seg),
        grid=(B, nblk, NB),
        in_specs=[pl.BlockSpec((1, Tb, bw), lambda b, g, i: (b, blk(i), g)),
                  pl.BlockSpec((conv_w.shape[0], bw), lambda b, g, i: (0, g)),
                  row,
                  pl.BlockSpec((1, bw, bw), lambda b, g, i: (g, 0, 0)), row,
                  pl.BlockSpec((1, bw, bw), lambda b, g, i: (g, 0, 0)), row,
                  row,
                  pl.BlockSpec((1, 1, bw), lambda b, g, i: (b, 0, g))],
        out_specs=[pl.BlockSpec((1, Tb, bw), lambda b, g, i: (b, blk(i), g)),
                   pl.BlockSpec((1, 1, bw), lambda b, g, i: (b, 0, g))],
        out_shape=[jax.ShapeDtypeStruct((B, T, width), F32), jax.ShapeDtypeStruct(h0.shape, F32)],
        name="rglru_scan",
        compiler_params=_params(("parallel", "parallel", "arbitrary")),
    )(pm, conv_w, conv_b, w_r, b_r, w_i, b_i, lam, h0)


def rglru_bidir(pm_c, pm_l, conv_w, conv_b, w_r, b_r, w_i, b_i, lam):
    B = pm_l.shape[0]
    width = conv_b.shape[-1]
    out = []
    for d in (0, 1):
        h0 = jnp.zeros((B, 1, width), F32)
        args = (conv_w, conv_b.reshape(1, width), w_r[d], b_r[d].reshape(1, width), w_i[d],
                b_i[d].reshape(1, width), lam[d].reshape(1, width))
        oc, h1 = rglru_scan(pm_c, *args, h0, bool(d), pm_c.shape[1], width)
        ol, _ = rglru_scan(pm_l, *args, h1, bool(d), GRID_W, width)
        out.append((oc, ol))
    return out


def _pad_cols(w, mult):
    return jnp.pad(w, ((0, 0), (0, (-w.shape[-1]) % mult)))


def kernel(x, c, ctx, c_ctx, mod_w, mod_b, norm_mix_g, norm_ffn_g, ffn_w_gu, ffn_w_down, final_norm_g,
           ab_w_in, ab_w_out, ml_ig_b, ml_fg_b, ml_norm_g, dn_conv_w, dn_a_log, dn_dt_bias, dn_norm_g,
           cd_w_in, cd_w_out, rg_conv_w, rg_conv_b, rg_w_r, rg_b_r, rg_w_i, rg_b_i, rg_lambda,
           gla_w_up, gla_b_up, gla_norm_g):
    B, T, D = x.shape
    depth = mod_w.shape[0]
    mlh, dnh = ml_ig_b.shape[-1], dn_a_log.shape[-1]
    rgw = rg_conv_b.shape[-1]
    glh = gla_w_up.shape[-1] // GLA_DK

    cond = jnp.concatenate([c, c_ctx[None, :], jnp.zeros((8 - B - 1, D), F32)], axis=0)

    xc, xl = ctx, x
    for layer in range(depth):
        last = layer == depth - 1
        j = layer // 2
        m = adaln_all(cond, mod_w[layer], mod_b[layer])
        lat = [t[:, None, :] for t in jnp.split(m[:B], 6, axis=-1)]
        cx = [t[:, None, :] for t in jnp.split(m[B:B + 1], 6, axis=-1)]

        if layer % 2 == 0:
            w_in = ab_w_in[j]
            s = np.cumsum((0, mlh * ML_DK, mlh * ML_DK, mlh * ML_DV, mlh * ML_DV, 2 * mlh, 2 * mlh,
                           dnh * DN_DK, dnh * DN_DK, dnh * DN_DV, dnh * DN_DV, 2 * dnh, 2 * dnh))
            w_main = jnp.concatenate([w_in[:, s[0]:s[4]], w_in[:, s[6]:s[10]]], axis=1)
            w_small = jnp.concatenate([w_in[:, s[4]:s[6]], w_in[:, s[10]:s[12]]], axis=1)
            w_out = ab_w_out[j]
            o = np.cumsum((0, mlh * ML_DK, mlh * ML_DK, mlh * ML_DV, mlh * ML_DV,
                           dnh * DN_DK, dnh * DN_DK, dnh * DN_DV))
        else:
            w_in = cd_w_in[j]
            o = np.cumsum((0, rgw, rgw, glh * GLA_DK, glh * GLA_DK, glh * GLA_DV))
            nmain = int(o[5]) + glh * GLA_DV
            w_main, w_small = w_in[:, :nmain], w_in[:, nmain:]
            w_out = cd_w_out[j]
        o = [int(v) for v in o]
        w_main = w_main.astype(BF16)
        w_small = _pad_cols(w_small, LANES).astype(BF16)
        w_out = w_out.astype(BF16)

        pm_l, ps_l = inproj(xl, norm_mix_g[layer], lat[1], lat[0], w_main, w_small)
        pm_c, ps_c = inproj(xc, norm_mix_g[layer], cx[1], cx[0], w_main, w_small)

        if layer % 2 == 0:
            mix_a = mlstm_bidir(pm_c, ps_c, pm_l, ps_l, ml_ig_b[j], ml_fg_b[j], o[0], o[1], o[2])
            mix_b = delta_bidir(pm_c, ps_c, pm_l, ps_l, dn_conv_w[j], dn_a_log[j], dn_dt_bias[j],
                                o[4], o[5], o[6], 4 * mlh)
            margs = (0, o[3], o[7], ml_norm_g[j], dn_norm_g[j], ML_DV, DN_DV)
        else:
            mix_a = rglru_bidir(pm_c, pm_l, rg_conv_w[j], rg_conv_b[j], rg_w_r[j], rg_b_r[j], rg_w_i[j],
                                rg_b_i[j], rg_lambda[j])
            mix_b = gla_bidir(pm_c, ps_c, pm_l, ps_l, gla_w_up[j], gla_b_up[j], o[2], o[3], o[4])
            margs = (1, o[1], o[5], jnp.ones((rgw,), F32), gla_norm_g[j], rgw, GLA_DV)

        def merged(which, pm, xres, gate):
            kind, ga_off, gb_off, na, nb, da, db = margs
            a_dirs = (mix_a[0][which], mix_a[1][which])
            b_dirs = (mix_b[0][which], mix_b[1][which])
            return merge_outproj(kind, a_dirs, b_dirs, pm, ga_off, gb_off, na, nb, da, db, xres, gate, w_out)

        w_gu = ffn_w_gu[layer].astype(BF16)
        w_dn = ffn_w_down[layer].astype(BF16)
        xl = merged(1, pm_l, xl, lat[2])
        xl = ffn(xl, norm_ffn_g[layer], lat[4], lat[3], lat[5], w_gu, w_dn, final_norm_g if last else None)
        if not last:
            xc = merged(0, pm_c, xc, cx[2])
            xc = ffn(xc, norm_ffn_g[layer], cx[4], cx[3], cx[5], w_gu, w_dn)
    return xl
```

```python
import functools

import jax
import jax.numpy as jnp
import numpy as np
from jax import lax
from jax.experimental import pallas as pl
from jax.experimental.pallas import tpu as pltpu

F32 = jnp.float32
BF16 = jnp.bfloat16

GRID_W = 64
CHUNK = 64
EPS = 1e-6
ML_DK, ML_DV = 128, 256
DN_DK, DN_DV = 128, 128
GLA_DK, GLA_DV = 128, 256
RG_C = 8.0
GLA_TAU = 16.0

LANES = 128
FFN_SUB = 256
VMEM_LIMIT = 48 * 1024 * 1024

NT = (((1,), (1,)), ((), ()))
TN = (((0,), (0,)), ((), ()))


def _tile(n, pref):
    t = min(n, pref)
    while n % t:
        t //= 2
    return t


def _params(sem):
    return pltpu.CompilerParams(dimension_semantics=sem, vmem_limit_bytes=VMEM_LIMIT)


def _dot(a, b, dims=None):
    if dims is None:
        return jnp.dot(a, b, preferred_element_type=F32)
    return lax.dot_general(a, b, dims, preferred_element_type=F32)


def _modulated_norm(x, g, sc, sh):
    y = x * lax.rsqrt(jnp.mean(x * x, axis=-1, keepdims=True) + EPS) * g
    return y * (1.0 + sc) + sh


def _adaln_kernel(s_ref, w_ref, b_ref, o_ref):
    s = s_ref[...]
    s = (s * jax.nn.sigmoid(s)).astype(BF16)
    o_ref[...] = _dot(s, w_ref[...].astype(BF16)) + b_ref[...]


def adaln_all(cond, w_mod, b_mod):
    R, D = cond.shape
    C = w_mod.shape[1]
    tn = _tile(C, 1024)
    return pl.pallas_call(
        _adaln_kernel,
        grid=(C // tn,),
        in_specs=[pl.BlockSpec((R, D), lambda j: (0, 0)),
                  pl.BlockSpec((D, tn), lambda j: (0, j)),
                  pl.BlockSpec((1, tn), lambda j: (0, j))],
        out_specs=pl.BlockSpec((R, tn), lambda j: (0, j)),
        out_shape=jax.ShapeDtypeStruct((R, C), F32),
        name="adaln_all",
        compiler_params=_params(("parallel",)),
    )(cond, w_mod, b_mod.reshape(1, C))


def _inproj_kernel(x_ref, g_ref, sc_ref, sh_ref, w_ref, ws_ref, o_ref, os_ref, h_ref):
    @pl.when(pl.program_id(2) == 0)
    def _():
        h = _modulated_norm(x_ref[0], g_ref[...], sc_ref[0], sh_ref[0]).astype(BF16)
        h_ref[...] = h
        os_ref[0] = _dot(h, ws_ref[...])

    o_ref[0] = _dot(h_ref[...], w_ref[...])


def inproj(x, g, sc, sh, w_main, w_small):
    B, T, D = x.shape
    C = w_main.shape[1]
    Cs = w_small.shape[1]
    tm, tn = _tile(T, 1024), _tile(C, 1024)
    mod = (lambda b, i, j: (b, 0, 0)) if sc.shape[0] == B else (lambda b, i, j: (0, 0, 0))
    return pl.pallas_call(
        _inproj_kernel,
        grid=(B, T // tm, C // tn),
        in_specs=[pl.BlockSpec((1, tm, D), lambda b, i, j: (b, i, 0)),
                  pl.BlockSpec((1, D), lambda b, i, j: (0, 0)),
                  pl.BlockSpec((1, 1, D), mod),
                  pl.BlockSpec((1, 1, D), mod),
                  pl.BlockSpec((D, tn), lambda b, i, j: (0, j)),
                  pl.BlockSpec((D, Cs), lambda b, i, j: (0, 0))],
        out_specs=[pl.BlockSpec((1, tm, tn), lambda b, i, j: (b, i, j)),
                   pl.BlockSpec((1, tm, Cs), lambda b, i, j: (b, i, 0))],
        out_shape=[jax.ShapeDtypeStruct((B, T, C), F32), jax.ShapeDtypeStruct((B, T, Cs), F32)],
        scratch_shapes=[pltpu.VMEM((tm, D), BF16)],
        name="inproj",
        compiler_params=_params(("parallel", "parallel", "arbitrary")),
    )(x, g.reshape(1, D), sc, sh, w_main, w_small)


def _head_norm(o, d, g):
    parts = []
    for h in range(o.shape[-1] // d):
        oh = o[:, h * d:(h + 1) * d]
        parts.append(oh * lax.rsqrt(jnp.mean(oh * oh, axis=-1, keepdims=True) + EPS))
    return jnp.concatenate(parts, axis=-1) * g


def _merge_kernel(af_ref, ab_ref, bf_ref, bb_ref, ga_ref, gb_ref, na_ref, nb_ref, x_ref, gate_ref, w_ref, o_ref,
                  *, kind, da, db):
    a = af_ref[0] + ab_ref[0]
    b = bf_ref[0] + bb_ref[0]
    if kind == 0:
        ya = _head_norm(a, da, na_ref[...]) * jax.nn.sigmoid(ga_ref[0])
    else:
        ya = jax.nn.gelu(ga_ref[0]) * a
    gb = gb_ref[0]
    yb = _head_norm(b, db, nb_ref[...]) * (gb * jax.nn.sigmoid(gb))
    wa = a.shape[-1]
    y = _dot(ya.astype(BF16), w_ref[0:wa, :]) + _dot(yb.astype(BF16), w_ref[wa:, :])
    o_ref[0] = x_ref[0] + gate_ref[0] * y


def merge_outproj(kind, a_dirs, b_dirs, pm, ga_off, gb_off, na, nb, da, db, x, gate, w):
    B, T, D = x.shape
    wa, wb = a_dirs[0].shape[-1], b_dirs[0].shape[-1]
    assert ga_off % wa == 0 and gb_off % wb == 0
    tm = _tile(T, 256)
    mod = (lambda b, i: (b, 0, 0)) if gate.shape[0] == B else (lambda b, i: (0, 0, 0))
    tok = lambda width, col: pl.BlockSpec((1, tm, width), lambda b, i: (b, i, col))
    return pl.pallas_call(
        functools.partial(_merge_kernel, kind=kind, da=da, db=db),
        grid=(B, T // tm),
        in_specs=[tok(wa, 0), tok(wa, 0), tok(wb, 0), tok(wb, 0), tok(wa, ga_off // wa), tok(wb, gb_off // wb),
                  pl.BlockSpec((1, wa), lambda b, i: (0, 0)),
                  pl.BlockSpec((1, wb), lambda b, i: (0, 0)),
                  tok(D, 0),
                  pl.BlockSpec((1, 1, D), mod),
                  pl.BlockSpec((wa + wb, D), lambda b, i: (0, 0))],
        out_specs=tok(D, 0),
        out_shape=jax.ShapeDtypeStruct((B, T, D), F32),
        name="merge_outproj",
        compiler_params=_params(("parallel", "parallel")),
    )(a_dirs[0], a_dirs[1], b_dirs[0], b_dirs[1], pm, pm, na.reshape(1, wa), nb.reshape(1, wb), x, gate, w)


def _ffn_kernel(x_ref, g_ref, sc_ref, sh_ref, gate_ref, wg_ref, wu_ref, wd_ref, fg_ref, o_ref, h_ref,
                *, final_norm):
    k = pl.program_id(2)

    @pl.when(k == 0)
    def _():
        h_ref[...] = _modulated_norm(x_ref[0], g_ref[...], sc_ref[0], sh_ref[0]).astype(BF16)
        o_ref[...] = jnp.zeros_like(o_ref)

    h = h_ref[...]
    acts = []
    for lo in range(0, wg_ref.shape[1], FFN_SUB):
        a = _dot(h, wg_ref[:, lo:lo + FFN_SUB])
        u = _dot(h, wu_ref[:, lo:lo + FFN_SUB])
        acts.append((a * jax.nn.sigmoid(a) * u).astype(BF16))
    o_ref[0] += _dot(jnp.concatenate(acts, axis=1), wd_ref[...])

    @pl.when(k == pl.num_programs(2) - 1)
    def _():
        r = x_ref[0] + gate_ref[0] * o_ref[0]
        if final_norm:
            r = r * lax.rsqrt(jnp.mean(r * r, axis=-1, keepdims=True) + EPS) * fg_ref[...]
        o_ref[0] = r


def ffn(x, g, sc, sh, gate, w_gu, w_down, final_g=None):
    B, T, D = x.shape
    Fh = w_down.shape[0]
    tm, tf = _tile(T, 512), _tile(Fh, 512)
    nf = Fh // tf
    mod = (lambda b, i, k: (b, 0, 0)) if sc.shape[0] == B else (lambda b, i, k: (0, 0, 0))
    fg = jnp.ones((1, D), F32) if final_g is None else final_g.reshape(1, D)
    return pl.pallas_call(
        functools.partial(_ffn_kernel, final_norm=final_g is not None),
        grid=(B, T // tm, nf),
        in_specs=[pl.BlockSpec((1, tm, D), lambda b, i, k: (b, i, 0)),
                  pl.BlockSpec((1, D), lambda b, i, k: (0, 0)),
                  pl.BlockSpec((1, 1, D), mod),
                  pl.BlockSpec((1, 1, D), mod),
                  pl.BlockSpec((1, 1, D), mod),
                  pl.BlockSpec((D, tf), lambda b, i, k: (0, k)),
                  pl.BlockSpec((D, tf), lambda b, i, k: (0, k + nf)),
                  pl.BlockSpec((tf, D), lambda b, i, k: (k, 0)),
                  pl.BlockSpec((1, D), lambda b, i, k: (0, 0))],
        out_specs=pl.BlockSpec((1, tm, D), lambda b, i, k: (b, i, 0)),
        out_shape=jax.ShapeDtypeStruct((B, T, D), F32),
        scratch_shapes=[pltpu.VMEM((tm, D), BF16)],
        name="ffn",
        compiler_params=_params(("parallel", "parallel", "arbitrary")),
    )(x, g.reshape(1, D), sc, sh, gate, w_gu, w_gu, w_down, fg)


def _incl_mask(rev):
    r = lax.broadcasted_iota(jnp.int32, (CHUNK, CHUNK), 0)
    c = lax.broadcasted_iota(jnp.int32, (CHUNK, CHUNK), 1)
    return (r <= c) if rev else (r >= c)


def _split_bf16(x):
    hi = x.astype(BF16)
    return hi, (x - hi.astype(F32)).astype(BF16)


def _chunk_cumsum(mask_bf16, x):
    hi, lo = _split_bf16(x)
    return _dot(mask_bf16, hi) + _dot(mask_bf16, lo)


def _log_sigmoid(z):
    return jnp.minimum(z, 0.0) - jnp.log1p(jnp.exp(-jnp.abs(z)))


def _softplus(x):
    return jnp.maximum(x, 0.0) + jnp.log1p(jnp.exp(-jnp.abs(x)))


def _time_block(T):
    return _tile(T, 512)


def _dwconv(x, w_ref, seg):
    n = x.shape[0]
    taps = w_ref.shape[0]
    pos = lax.broadcasted_iota(jnp.int32, (n, 1), 0) % seg
    out = None
    for j in range(taps):
        sh = j - taps // 2
        xs = x if sh == 0 else pltpu.roll(x, (-sh) % n, axis=0)
        ok = (pos + sh >= 0) & (pos + sh < seg)
        term = jnp.where(ok, xs, 0.0) * w_ref[j:j + 1, :]
        out = term if out is None else out + term
    return out


def _gla_kernel(q_ref, k_ref, v_ref, ps_ref, wup_ref, bup_ref, s0_ref, o_ref, st_ref, lg_ref, *, rev, heads, nc):
    @pl.when(pl.program_id(1) == 0)
    def _():
        st_ref[...] = s0_ref[...]

    mask = _incl_mask(rev)
    mask_bf = jnp.where(mask, 1.0, 0.0).astype(BF16)
    z = _dot(ps_ref[0].astype(BF16), wup_ref[...]) + bup_ref[...]
    lg_ref[...] = _log_sigmoid(z) * (1.0 / GLA_TAU)

    def body(ci, carry):
        c = (nc - 1 - ci) if rev else ci
        rows = pl.ds(pl.multiple_of(c * CHUNK, CHUNK), CHUNK)
        lg = lg_ref[rows, :]
        b = _chunk_cumsum(mask_bf, lg)
        b_end = jnp.sum(lg, axis=0, keepdims=True)
        k = k_ref[0, rows, :]
        qe = (q_ref[0, rows, :] * (GLA_DK ** -0.5) * jnp.exp(b)).astype(BF16)
        ke = (k * jnp.exp(-b)).astype(BF16)
        kd = (k * jnp.exp(b_end - b)).astype(BF16)
        for h in range(heads):
            ks = slice(h * GLA_DK, (h + 1) * GLA_DK)
            vs = slice(h * GLA_DV, (h + 1) * GLA_DV)
            v = v_ref[0, rows, vs].astype(BF16)
            att = jnp.where(mask, _dot(qe[:, ks], ke[:, ks], NT), 0.0).astype(BF16)
            st = st_ref[0, h]
            o_ref[0, rows, vs] = _dot(att, v) + _dot(qe[:, ks], st.astype(BF16), NT)
            st_ref[0, h] = st * jnp.exp(b_end[:, ks]) + _dot(v, kd[:, ks], TN)
        return carry

    lax.fori_loop(0, nc, body, 0)


def gla_scan(pm, ps, wup, bup, s0, rev, q_off, k_off, v_off, heads):
    B, T, _ = pm.shape
    Tb = _time_block(T)
    NB = T // Tb
    kw, vw = heads * GLA_DK, heads * GLA_DV
    assert q_off % kw == 0 and k_off % kw == 0 and v_off % vw == 0
    blk = (lambda i: NB - 1 - i) if rev else (lambda i: i)
    return pl.pallas_call(
        functools.partial(_gla_kernel, rev=rev, heads=heads, nc=Tb // CHUNK),
        grid=(B, NB),
        in_specs=[pl.BlockSpec((1, Tb, kw), lambda b, i: (b, blk(i), q_off // kw)),
                  pl.BlockSpec((1, Tb, kw), lambda b, i: (b, blk(i), k_off // kw)),
                  pl.BlockSpec((1, Tb, vw), lambda b, i: (b, blk(i), v_off // vw)),
                  pl.BlockSpec((1, Tb, ps.shape[-1]), lambda b, i: (b, blk(i), 0)),
                  pl.BlockSpec(wup.shape, lambda b, i: (0, 0)),
                  pl.BlockSpec(bup.shape, lambda b, i: (0, 0)),
                  pl.BlockSpec((1,) + s0.shape[1:], lambda b, i: (b, 0, 0, 0))],
        out_specs=[pl.BlockSpec((1, Tb, vw), lambda b, i: (b, blk(i), 0)),
                   pl.BlockSpec((1,) + s0.shape[1:], lambda b, i: (b, 0, 0, 0))],
        out_shape=[jax.ShapeDtypeStruct((B, T, vw), F32), jax.ShapeDtypeStruct(s0.shape, F32)],
        scratch_shapes=[pltpu.VMEM((Tb, kw), F32)],
        name="gla_scan",
        compiler_params=_params(("parallel", "arbitrary")),
    )(pm, pm, pm, ps, wup, bup, s0)


def gla_bidir(pm_c, ps_c, pm_l, ps_l, w_up, b_up, q_off, k_off, v_off):
    B = pm_l.shape[0]
    rank, hk = w_up.shape[1], w_up.shape[2]
    heads = hk // GLA_DK
    out = []
    for d in (0, 1):
        wup = jnp.zeros((ps_l.shape[-1], hk), F32).at[d * rank:(d + 1) * rank].set(w_up[d]).astype(BF16)
        bup = b_up[d].reshape(1, hk)
        s0 = jnp.zeros((B, heads, GLA_DV, GLA_DK), F32)
        oc, s1 = gla_scan(pm_c, ps_c, wup, bup, s0, bool(d), q_off, k_off, v_off, heads)
        ol, _ = gla_scan(pm_l, ps_l, wup, bup, s1, bool(d), q_off, k_off, v_off, heads)
        out.append((oc, ol))
    return out


def _mlstm_kernel(q_ref, k_ref, v_ref, ps_ref, gb_ref, c0_ref, nm0_ref, o_ref, c_ref, nm_ref,
                  *, rev, heads, nc, ig_col, fg_col):
    @pl.when(pl.program_id(1) == 0)
    def _():
        c_ref[...] = c0_ref[...]
        nm_ref[...] = nm0_ref[...]

    mask = _incl_mask(rev)
    mask_bf = jnp.where(mask, 1.0, 0.0).astype(BF16)

    def body(ci, carry):
        c = (nc - 1 - ci) if rev else ci
        rows = pl.ds(pl.multiple_of(c * CHUNK, CHUNK), CHUNK)
        g = ps_ref[0, rows, :] + gb_ref[...]
        lf = _log_sigmoid(g)
        bcum = _chunk_cumsum(mask_bf, lf)
        b_end = jnp.sum(lf, axis=0, keepdims=True)
        bcum_t, g_t = bcum.T, g.T
        for h in range(heads):
            ic, fc = ig_col + h, fg_col + h
            ks = slice(h * ML_DK, (h + 1) * ML_DK)
            vs = slice(h * ML_DV, (h + 1) * ML_DV)
            b_col, ig_col_v = bcum[:, fc:fc + 1], g[:, ic:ic + 1]
            b_row, ig_row = bcum_t[fc:fc + 1, :], g_t[ic:ic + 1, :]
            b_l = b_end[:, fc:fc + 1]
            q = q_ref[0, rows, ks] * (ML_DK ** -0.5)
            k = k_ref[0, rows, ks]
            v = v_ref[0, rows, vs]
            qb, kb = q.astype(BF16), k.astype(BF16)
            cst = c_ref[0, h]
            n = nm_ref[0, h, 0:1, :]
            m = nm_ref[0, h, 1:2, 0:1]
            q_c = _dot(qb, cst.astype(BF16), NT)
            q_n = jnp.sum(q * n, axis=-1, keepdims=True)
            inter_log = b_col + m
            d_log = jnp.where(mask, b_col - b_row + ig_row, -jnp.inf)
            m_out = jnp.maximum(jnp.max(d_log, axis=-1, keepdims=True), inter_log)
            s = _dot(qb, kb, NT) * jnp.exp(d_log - m_out)
            e_inter = jnp.exp(inter_log - m_out)
            num = _dot(s.astype(BF16), v.astype(BF16)) + e_inter * q_c
            den = jnp.sum(s, axis=-1, keepdims=True) + e_inter * q_n
            o_ref[0, rows, vs] = num / jnp.maximum(jnp.abs(den), jnp.exp(-m_out))
            w_log = b_l - b_col + ig_col_v
            m_loc = jnp.max(w_log, axis=0, keepdims=True)
            w = jnp.exp(w_log - m_loc)
            m_new = jnp.maximum(b_l + m, m_loc)
            a = jnp.exp(b_l + m - m_new)
            sc = jnp.exp(m_loc - m_new)
            c_ref[0, h] = a * cst + sc * _dot((w * v).astype(BF16), kb, TN)
            nm_ref[0, h, 0:1, :] = a * n + sc * jnp.sum(w * k, axis=0, keepdims=True)
            nm_ref[0, h, 1:2, :] = jnp.broadcast_to(m_new, (1, ML_DK))
        return carry

    lax.fori_loop(0, nc, body, 0)


def mlstm_scan(pm, ps, gbias, state, rev, q_off, k_off, v_off, heads, ig_col, fg_col):
    B, T, _ = pm.shape
    Tb = _time_block(T)
    NB = T // Tb
    kw, vw = heads * ML_DK, heads * ML_DV
    assert q_off % kw == 0 and k_off % kw == 0 and v_off % vw == 0
    c0, nm0 = state
    blk = (lambda i: NB - 1 - i) if rev else (lambda i: i)
    st_spec = lambda a: pl.BlockSpec((1,) + a.shape[1:], lambda b, i: (b, 0, 0, 0))
    o, c1, nm1 = pl.pallas_call(
        functools.partial(_mlstm_kernel, rev=rev, heads=heads, nc=Tb // CHUNK, ig_col=ig_col, fg_col=fg_col),
        grid=(B, NB),
        in_specs=[pl.BlockSpec((1, Tb, kw), lambda b, i: (b, blk(i), q_off // kw)),
                  pl.BlockSpec((1, Tb, kw), lambda b, i: (b, blk(i), k_off // kw)),
                  pl.BlockSpec((1, Tb, vw), lambda b, i: (b, blk(i), v_off // vw)),
                  pl.BlockSpec((1, Tb, ps.shape[-1]), lambda b, i: (b, blk(i), 0)),
                  pl.BlockSpec(gbias.shape, lambda b, i: (0, 0)),
                  st_spec(c0), st_spec(nm0)],
        out_specs=[pl.BlockSpec((1, Tb, vw), lambda b, i: (b, blk(i), 0)), st_spec(c0), st_spec(nm0)],
        out_shape=[jax.ShapeDtypeStruct((B, T, vw), F32), jax.ShapeDtypeStruct(c0.shape, F32),
                   jax.ShapeDtypeStruct(nm0.shape, F32)],
        name="mlstm_scan",
        compiler_params=_params(("parallel", "arbitrary")),
    )(pm, pm, pm, ps, gbias, c0, nm0)
    return o, (c1, nm1)


def mlstm_bidir(pm_c, ps_c, pm_l, ps_l, ig_b, fg_b, q_off, k_off, v_off):
    B = pm_l.shape[0]
    heads = ig_b.shape[-1]
    ncol = ps_l.shape[-1]
    gbias = jnp.zeros((1, ncol), F32).at[0, :4 * heads].set(jnp.concatenate([ig_b.reshape(-1), fg_b.reshape(-1)]))
    out = []
    for d in (0, 1):
        st = (jnp.zeros((B, heads, ML_DV, ML_DK), F32), jnp.zeros((B, heads, 8, ML_DK), F32))
        args = (bool(d), q_off, k_off, v_off, heads, d * heads, 2 * heads + d * heads)
        oc, st = mlstm_scan(pm_c, ps_c, gbias, st, *args)
        ol, _ = mlstm_scan(pm_l, ps_l, gbias, st, *args)
        out.append((oc, ol))
    return out


def _block_join_masks():
    r = lax.broadcasted_iota(jnp.int32, (CHUNK, CHUNK), 0)
    c = lax.broadcasted_iota(jnp.int32, (CHUNK, CHUNK), 1)
    masks, s = [], 1
    while s < CHUNK:
        masks.append(jnp.where((r // (2 * s) == c // (2 * s)) & (r // s != c // s), 1.0, 0.0))
        s *= 2
    return masks


def _unit_tri_inverses(a_list, join_masks):
    r = lax.broadcasted_iota(jnp.int32, (CHUNK, CHUNK), 0)
    c = lax.broadcasted_iota(jnp.int32, (CHUNK, CHUNK), 1)
    eye = jnp.where(r == c, 1.0, 0.0)
    ts = [eye - a * join_masks[0] for a in a_list]
    for m in join_masks[1:]:
        tb = [t.astype(BF16) for t in ts]
        xs = [_dot(t, (a * m).astype(BF16)) for t, a in zip(tb, a_list)]
        ts = [t - _dot(x.astype(BF16), t_b) for t, x, t_b in zip(ts, xs, tb)]
    return ts


def _delta_kernel(q_ref, k_ref, v_ref, ps_ref, gp_ref, cw_ref, s0_ref, o_ref, st_ref, qc_ref, kc_ref, vc_ref,
                  *, rev, heads, nc, seg, g_col, beta_col):
    @pl.when(pl.program_id(1) == 0)
    def _():
        st_ref[...] = s0_ref[...]

    hw = heads * DN_DK
    conv = lambda ref, lo: jax.nn.silu(_dwconv(ref[0], cw_ref.at[:, lo:lo + ref.shape[-1]], seg))
    qa, ka = conv(q_ref, 0), conv(k_ref, hw)
    vc_ref[...] = conv(v_ref, 2 * hw)
    for h in range(heads):
        ks = slice(h * DN_DK, (h + 1) * DN_DK)
        qh, kh = qa[:, ks], ka[:, ks]
        qc_ref[:, ks] = qh * lax.rsqrt(jnp.sum(qh * qh, axis=-1, keepdims=True) + EPS) * (DN_DK ** -0.5)
        kc_ref[:, ks] = kh * lax.rsqrt(jnp.sum(kh * kh, axis=-1, keepdims=True) + EPS)

    mask = _incl_mask(rev)
    mask_bf = jnp.where(mask, 1.0, 0.0).astype(BF16)
    r = lax.broadcasted_iota(jnp.int32, (CHUNK, CHUNK), 0)
    c = lax.broadcasted_iota(jnp.int32, (CHUNK, CHUNK), 1)
    off_diag = jnp.where(r != c, 1.0, 0.0)
    join_masks = _block_join_masks()
    L = CHUNK

    def body(ci, carry):
        cidx = (nc - 1 - ci) if rev else ci
        rows = pl.ds(pl.multiple_of(cidx * CHUNK, CHUNK), CHUNK)
        pre = ps_ref[0, rows, :]
        g = gp_ref[1:2, :] * _softplus(pre + gp_ref[0:1, :])
        beta = jax.nn.sigmoid(pre)
        gcum = _chunk_cumsum(mask_bf, g)
        g_end = jnp.sum(g, axis=0, keepdims=True)
        gcum_t = gcum.T
        e_cum, e_rest, e_end = jnp.exp(gcum), jnp.exp(g_end - gcum), jnp.exp(g_end)

        q_l, k_l, kbeta_l, vbeta_l, a_l, qk_l = [], [], [], [], [], []
        for h in range(heads):
            gc, bc = g_col + h, beta_col + h
            ks = slice(h * DN_DK, (h + 1) * DN_DK)
            q, k = qc_ref[rows, ks], kc_ref[rows, ks]
            b_c = beta[:, bc:bc + 1]
            decay = jnp.exp(jnp.where(mask, gcum[:, gc:gc + 1] - gcum_t[gc:gc + 1, :], -jnp.inf))
            kbeta = k * b_c
            both = _dot(jnp.concatenate([kbeta, q], axis=0).astype(BF16), k.astype(BF16), NT)
            q_l.append(q)
            k_l.append(k)
            kbeta_l.append(kbeta)
            vbeta_l.append(vc_ref[rows, h * DN_DV:(h + 1) * DN_DV] * b_c)
            a_l.append(both[:L] * decay * off_diag)
            qk_l.append((both[L:] * decay).astype(BF16))
        t_l = _unit_tri_inverses(a_l, join_masks)
        for h in range(heads):
            gc = g_col + h
            vs = slice(h * DN_DV, (h + 1) * DN_DV)
            rhs = jnp.concatenate([vbeta_l[h], kbeta_l[h] * e_cum[:, gc:gc + 1]], axis=1).astype(BF16)
            uw = _dot(t_l[h].astype(BF16), rhs)
            st = st_ref[0, h]
            lhs = jnp.concatenate([uw[:, DN_DV:], q_l[h] * e_cum[:, gc:gc + 1]], axis=0).astype(BF16)
            ws = _dot(lhs, st.astype(BF16))
            v_new = (uw[:, :DN_DV] - ws[:L]).astype(BF16)
            o_ref[0, rows, vs] = ws[L:] + _dot(qk_l[h], v_new)
            kd = (k_l[h] * e_rest[:, gc:gc + 1]).astype(BF16)
            st_ref[0, h] = e_end[:, gc:gc + 1] * st + _dot(kd, v_new, TN)
        return carry

    lax.fori_loop(0, nc, body, 0)


def delta_scan(pm, ps, gparams, conv_w, s0, rev, seg, q_off, k_off, v_off, heads, g_col, beta_col):
    B, T, _ = pm.shape
    Tb = _time_block(T)
    NB = T // Tb
    assert Tb % seg == 0 or (NB == 1 and seg == T)
    kw, vw = heads * DN_DK, heads * DN_DV
    assert q_off % kw == 0 and k_off % kw == 0 and v_off % vw == 0
    blk = (lambda i: NB - 1 - i) if rev else (lambda i: i)
    st_spec = pl.BlockSpec((1,) + s0.shape[1:], lambda b, i: (b, 0, 0, 0))
    return pl.pallas_call(
        functools.partial(_delta_kernel, rev=rev, heads=heads, nc=Tb // CHUNK, seg=seg, g_col=g_col,
                          beta_col=beta_col),
        grid=(B, NB),
        in_specs=[pl.BlockSpec((1, Tb, kw), lambda b, i: (b, blk(i), q_off // kw)),
                  pl.BlockSpec((1, Tb, kw), lambda b, i: (b, blk(i), k_off // kw)),
                  pl.BlockSpec((1, Tb, vw), lambda b, i: (b, blk(i), v_off // vw)),
                  pl.BlockSpec((1, Tb, ps.shape[-1]), lambda b, i: (b, blk(i), 0)),
                  pl.BlockSpec(gparams.shape, lambda b, i: (0, 0)),
                  pl.BlockSpec(conv_w.shape, lambda b, i: (0, 0)),
                  st_spec],
        out_specs=[pl.BlockSpec((1, Tb, vw), lambda b, i: (b, blk(i), 0)), st_spec],
        out_shape=[jax.ShapeDtypeStruct((B, T, vw), F32), jax.ShapeDtypeStruct(s0.shape, F32)],
        scratch_shapes=[pltpu.VMEM((Tb, kw), F32), pltpu.VMEM((Tb, kw), F32), pltpu.VMEM((Tb, vw), F32)],
        name="delta_scan",
        compiler_params=_params(("parallel", "arbitrary")),
    )(pm, pm, pm, ps, gparams, conv_w, s0)


def delta_bidir(pm_c, ps_c, pm_l, ps_l, conv_w, a_log, dt_bias, q_off, k_off, v_off, col0):
    B = pm_l.shape[0]
    heads = a_log.shape[-1]
    ncol = ps_l.shape[-1]
    gparams = jnp.zeros((8, ncol), F32)
    gparams = gparams.at[0, col0:col0 + 2 * heads].set(dt_bias.reshape(-1))
    gparams = gparams.at[1, col0:col0 + 2 * heads].set(-jnp.exp(a_log).reshape(-1))
    out = []
    for d in (0, 1):
        s0 = jnp.zeros((B, heads, DN_DK, DN_DV), F32)
        args = (q_off, k_off, v_off, heads, col0 + d * heads, col0 + 2 * heads + d * heads)
        oc, s1 = delta_scan(pm_c, ps_c, gparams, conv_w, s0, bool(d), pm_c.shape[1], *args)
        ol, _ = delta_scan(pm_l, ps_l, gparams, conv_w, s1, bool(d), GRID_W, *args)
        out.append((oc, ol))
    return out


def _rglru_kernel(x_ref, cw_ref, cb_ref, wr_ref, br_ref, wi_ref, bi_ref, lam_ref, h0_ref, o_ref, hT_ref,
                  *, rev, nc, seg):
    @pl.when(pl.program_id(2) == 0)
    def _():
        hT_ref[...] = h0_ref[...]

    n = x_ref.shape[1]
    xc = _dwconv(x_ref[0], cw_ref, seg) + cb_ref[...]
    xb = xc.astype(BF16)
    r = jax.nn.sigmoid(_dot(xb, wr_ref[0].astype(BF16)) + br_ref[...])
    i = jax.nn.sigmoid(_dot(xb, wi_ref[0].astype(BF16)) + bi_ref[...])
    log_a = -RG_C * r * _softplus(-lam_ref[...])
    a = jnp.exp(log_a)
    bx = jnp.sqrt(1.0 - jnp.exp(2.0 * log_a)) * (i * xc)

    pos = lax.broadcasted_iota(jnp.int32, (n, 1), 0) % CHUNK
    s = 1
    while s < CHUNK:
        ok = (pos < CHUNK - s) if rev else (pos >= s)
        shift = (n - s) if rev else s
        a_sh = pltpu.roll(a, shift, axis=0)
        b_sh = pltpu.roll(bx, shift, axis=0)
        bx = jnp.where(ok, a * b_sh + bx, bx)
        a = jnp.where(ok, a * a_sh, a)
        s *= 2

    h = hT_ref[0]
    for ci in range(nc):
        c = (nc - 1 - ci) if rev else ci
        rows = slice(c * CHUNK, (c + 1) * CHUNK)
        hc = bx[rows] + a[rows] * h
        o_ref[0, rows, :] = hc
        h = hc[0:1] if rev else hc[CHUNK - 1:CHUNK]
    hT_ref[0] = h


def rglru_scan(pm, conv_w, conv_b, w_r, b_r, w_i, b_i, lam, h0, rev, seg, width):
    B, T, _ = pm.shape
    nblk, bw = w_r.shape[0], w_r.shape[1]
    assert bw % LANES == 0 and nblk * bw == width
    Tb = _tile(T, 1024)
    NB = T // Tb
    assert Tb % seg == 0 or (NB == 1 and seg == T)
    blk = (lambda i: NB - 1 - i) if rev else (lambda i: i)
    row = pl.BlockSpec((1, bw), lambda b, g, i: (0, g))
    return pl.pallas_call(
        functools.partial(_rglru_kernel, rev=rev, nc=Tb // CHUNK, seg=seg),
        grid=(B, nblk, NB),
        in_specs=[pl.BlockSpec((1, Tb, bw), lambda b, g, i: (b, blk(i), g)),
                  pl.BlockSpec((conv_w.shape[0], bw), lambda b, g, i: (0, g)),
                  row,
                  pl.BlockSpec((1, bw, bw), lambda b, g, i: (g, 0, 0)), row,
                  pl.BlockSpec((1, bw, bw), lambda b, g, i: (g, 0, 0)), row,
                  row,
                  pl.BlockSpec((1, 1, bw), lambda b, g, i: (b, 0, g))],
        out_specs=[pl.BlockSpec((1, Tb, bw), lambda b, g, i: (b, blk(i), g)),
                   pl.BlockSpec((1, 1, bw), lambda b, g, i: (b, 0, g))],
        out_shape=[jax.ShapeDtypeStruct((B, T, width), F32), jax.ShapeDtypeStruct(h0.shape, F32)],
        name="rglru_scan",
        compiler_params=_params(("parallel", "parallel", "arbitrary")),
    )(pm, conv_w, conv_b, w_r, b_r, w_i, b_i, lam, h0)


def rglru_bidir(pm_c, pm_l, conv_w, conv_b, w_r, b_r, w_i, b_i, lam):
    B = pm_l.shape[0]
    width = conv_b.shape[-1]
    out = []
    for d in (0, 1):
        h0 = jnp.zeros((B, 1, width), F32)
        args = (conv_w, conv_b.reshape(1, width), w_r[d], b_r[d].reshape(1, width), w_i[d],
                b_i[d].reshape(1, width), lam[d].reshape(1, width))
        oc, h1 = rglru_scan(pm_c, *args, h0, bool(d), pm_c.shape[1], width)
        ol, _ = rglru_scan(pm_l, *args, h1, bool(d), GRID_W, width)
        out.append((oc, ol))
    return out


def _pad_cols(w, mult):
    return jnp.pad(w, ((0, 0), (0, (-w.shape[-1]) % mult)))


def kernel(x, c, ctx, c_ctx, mod_w, mod_b, norm_mix_g, norm_ffn_g, ffn_w_gu, ffn_w_down, final_norm_g,
           ab_w_in, ab_w_out, ml_ig_b, ml_fg_b, ml_norm_g, dn_conv_w, dn_a_log, dn_dt_bias, dn_norm_g,
           cd_w_in, cd_w_out, rg_conv_w, rg_conv_b, rg_w_r, rg_b_r, rg_w_i, rg_b_i, rg_lambda,
           gla_w_up, gla_b_up, gla_norm_g):
    B, T, D = x.shape
    depth = mod_w.shape[0]
    mlh, dnh = ml_ig_b.shape[-1], dn_a_log.shape[-1]
    rgw = rg_conv_b.shape[-1]
    glh = gla_w_up.shape[-1] // GLA_DK

    cond = jnp.concatenate([c, c_ctx[None, :], jnp.zeros((8 - B - 1, D), F32)], axis=0)

    xc, xl = ctx, x
    for layer in range(depth):
        last = layer == depth - 1
        j = layer // 2
        m = adaln_all(cond, mod_w[layer], mod_b[layer])
        lat = [t[:, None, :] for t in jnp.split(m[:B], 6, axis=-1)]
        cx = [t[:, None, :] for t in jnp.split(m[B:B + 1], 6, axis=-1)]

        if layer % 2 == 0:
            w_in = ab_w_in[j]
            s = np.cumsum((0, mlh * ML_DK, mlh * ML_DK, mlh * ML_DV, mlh * ML_DV, 2 * mlh, 2 * mlh,
                           dnh * DN_DK, dnh * DN_DK, dnh * DN_DV, dnh * DN_DV, 2 * dnh, 2 * dnh))
            w_main = jnp.concatenate([w_in[:, s[0]:s[4]], w_in[:, s[6]:s[10]]], axis=1)
            w_small = jnp.concatenate([w_in[:, s[4]:s[6]], w_in[:, s[10]:s[12]]], axis=1)
            w_out = ab_w_out[j]
            o = np.cumsum((0, mlh * ML_DK, mlh * ML_DK, mlh * ML_DV, mlh * ML_DV,
                           dnh * DN_DK, dnh * DN_DK, dnh * DN_DV))
        else:
            w_in = cd_w_in[j]
            o = np.cumsum((0, rgw, rgw, glh * GLA_DK, glh * GLA_DK, glh * GLA_DV))
            nmain = int(o[5]) + glh * GLA_DV
            w_main, w_small = w_in[:, :nmain], w_in[:, nmain:]
            w_out = cd_w_out[j]
        o = [int(v) for v in o]
        w_main = w_main.astype(BF16)
        w_small = _pad_cols(w_small, LANES).astype(BF16)
        w_out = w_out.astype(BF16)

        pm_l, ps_l = inproj(xl, norm_mix_g[layer], lat[1], lat[0], w_main, w_small)
        pm_c, ps_c = inproj(xc, norm_mix_g[layer], cx[1], cx[0], w_main, w_small)

        if layer % 2 == 0:
            mix_a = mlstm_bidir(pm_c, ps_c, pm_l, ps_l, ml_ig_b[j], ml_fg_b[j], o[0], o[1], o[2])
            mix_b = delta_bidir(pm_c, ps_c, pm_l, ps_l, dn_conv_w[j], dn_a_log[j], dn_dt_bias[j],
                                o[4], o[5], o[6], 4 * mlh)
            margs = (0, o[3], o[7], ml_norm_g[j], dn_norm_g[j], ML_DV, DN_DV)
        else:
            mix_a = rglru_bidir(pm_c, pm_l, rg_conv_w[j], rg_conv_b[j], rg_w_r[j], rg_b_r[j], rg_w_i[j],
                                rg_b_i[j], rg_lambda[j])
            mix_b = gla_bidir(pm_c, ps_c, pm_l, ps_l, gla_w_up[j], gla_b_up[j], o[2], o[3], o[4])
            margs = (1, o[1], o[5], jnp.ones((rgw,), F32), gla_norm_g[j], rgw, GLA_DV)

        def merged(which, pm, xres, gate):
            kind, ga_off, gb_off, na, nb, da, db = margs
            a_dirs = (mix_a[0][which], mix_a[1][which])
            b_dirs = (mix_b[0][which], mix_b[1][which])
            return merge_outproj(kind, a_dirs, b_dirs, pm, ga_off, gb_off, na, nb, da, db, xres, gate, w_out)

        w_gu = ffn_w_gu[layer].astype(BF16)
        w_dn = ffn_w_down[layer].astype(BF16)
        xl = merged(1, pm_l, xl, lat[2])
        xl = ffn(xl, norm_ffn_g[layer], lat[4], lat[3], lat[5], w_gu, w_dn, final_norm_g if last else None)
        if not last:
            xc = merged(0, pm_c, xc, cx[2])
            xc = ffn(xc, norm_ffn_g[layer], cx[4], cx[3], cx[5], w_gu, w_dn)
    return xl
```

```python
import functools

import jax
import jax.numpy as jnp
import numpy as np
from jax import lax
from jax.experimental import pallas as pl
from jax.experimental.pallas import tpu as pltpu

F32 = jnp.float32
BF16 = jnp.bfloat16

GRID_W = 64
CHUNK = 64
EPS = 1e-6
ML_DK, ML_DV = 128, 256
DN_DK, DN_DV = 128, 128
GLA_DK, GLA_DV = 128, 256
RG_C = 8.0
GLA_TAU = 16.0

LANES = 128
FFN_SUB = 256
VMEM_LIMIT = 48 * 1024 * 1024
FFN_VMEM_LIMIT = 58 * 1024 * 1024

NT = (((1,), (1,)), ((), ()))
TN = (((0,), (0,)), ((), ()))


def _tile(n, pref):
    t = min(n, pref)
    while n % t:
        t //= 2
    return t


def _params(sem, vmem=VMEM_LIMIT):
    return pltpu.CompilerParams(dimension_semantics=sem, vmem_limit_bytes=vmem)


def _dot(a, b, dims=None):
    if dims is None:
        return jnp.dot(a, b, preferred_element_type=F32)
    return lax.dot_general(a, b, dims, preferred_element_type=F32)


def _modulated_norm(x, g, sc, sh):
    y = x * lax.rsqrt(jnp.mean(x * x, axis=-1, keepdims=True) + EPS) * g
    return y * (1.0 + sc) + sh


def _adaln_kernel(s_ref, w_ref, b_ref, o_ref):
    s = s_ref[...]
    s = (s * jax.nn.sigmoid(s)).astype(BF16)
    o_ref[0] = _dot(s, w_ref[0].astype(BF16)) + b_ref[0]


def adaln_all(cond, w_mod, b_mod):
    R, D = cond.shape
    depth, _, C = w_mod.shape
    tn = _tile(C, 1024)
    return pl.pallas_call(
        _adaln_kernel,
        grid=(depth, C // tn),
        in_specs=[pl.BlockSpec((R, D), lambda l, j: (0, 0)),
                  pl.BlockSpec((1, D, tn), lambda l, j: (l, 0, j)),
                  pl.BlockSpec((1, 1, tn), lambda l, j: (l, 0, j))],
        out_specs=pl.BlockSpec((1, R, tn), lambda l, j: (l, 0, j)),
        out_shape=jax.ShapeDtypeStruct((depth, R, C), F32),
        name="adaln_all",
        compiler_params=_params(("parallel", "parallel")),
    )(cond, w_mod, b_mod.reshape(depth, 1, C))


def _inproj_kernel(x_ref, g_ref, sc_ref, sh_ref, w_ref, ws_ref, o_ref, os_ref, h_ref):
    @pl.when(pl.program_id(2) == 0)
    def _():
        h = _modulated_norm(x_ref[0], g_ref[...], sc_ref[0], sh_ref[0]).astype(BF16)
        h_ref[...] = h
        os_ref[0] = _dot(h, ws_ref[...])

    o_ref[0] = _dot(h_ref[...], w_ref[...])


def inproj(x, g, sc, sh, w_main, w_small):
    B, T, D = x.shape
    C = w_main.shape[1]
    Cs = w_small.shape[1]
    tm, tn = _tile(T, 1024), _tile(C, 1024)
    mod = (lambda b, i, j: (b, 0, 0)) if sc.shape[0] == B else (lambda b, i, j: (0, 0, 0))
    return pl.pallas_call(
        _inproj_kernel,
        grid=(B, T // tm, C // tn),
        in_specs=[pl.BlockSpec((1, tm, D), lambda b, i, j: (b, i, 0)),
                  pl.BlockSpec((1, D), lambda b, i, j: (0, 0)),
                  pl.BlockSpec((1, 1, D), mod),
                  pl.BlockSpec((1, 1, D), mod),
                  pl.BlockSpec((D, tn), lambda b, i, j: (0, j)),
                  pl.BlockSpec((D, Cs), lambda b, i, j: (0, 0))],
        out_specs=[pl.BlockSpec((1, tm, tn), lambda b, i, j: (b, i, j)),
                   pl.BlockSpec((1, tm, Cs), lambda b, i, j: (b, i, 0))],
        out_shape=[jax.ShapeDtypeStruct((B, T, C), F32), jax.ShapeDtypeStruct((B, T, Cs), F32)],
        scratch_shapes=[pltpu.VMEM((tm, D), BF16)],
        name="inproj",
        compiler_params=_params(("parallel", "parallel", "arbitrary")),
    )(x, g.reshape(1, D), sc, sh, w_main, w_small)


def _head_norm(o, d, g):
    parts = []
    for h in range(o.shape[-1] // d):
        oh = o[:, h * d:(h + 1) * d]
        parts.append(oh * lax.rsqrt(jnp.mean(oh * oh, axis=-1, keepdims=True) + EPS))
    return jnp.concatenate(parts, axis=-1) * g


def _merge_kernel(af_ref, ab_ref, bf_ref, bb_ref, ga_ref, gb_ref, na_ref, nb_ref, x_ref, gate_ref, w_ref, o_ref,
                  *, kind, da, db):
    a = af_ref[0] + ab_ref[0]
    b = bf_ref[0] + bb_ref[0]
    if kind == 0:
        ya = _head_norm(a, da, na_ref[...]) * jax.nn.sigmoid(ga_ref[0])
    else:
        ya = jax.nn.gelu(ga_ref[0]) * a
    gb = gb_ref[0]
    yb = _head_norm(b, db, nb_ref[...]) * (gb * jax.nn.sigmoid(gb))
    wa = a.shape[-1]
    y = _dot(ya.astype(BF16), w_ref[0:wa, :]) + _dot(yb.astype(BF16), w_ref[wa:, :])
    o_ref[0] = x_ref[0] + gate_ref[0] * y


def merge_outproj(kind, a_dirs, b_dirs, pm, ga_off, gb_off, na, nb, da, db, x, gate, w):
    B, T, D = x.shape
    wa, wb = a_dirs[0].shape[-1], b_dirs[0].shape[-1]
    assert ga_off % wa == 0 and gb_off % wb == 0
    tm = _tile(T, 256)
    mod = (lambda b, i: (b, 0, 0)) if gate.shape[0] == B else (lambda b, i: (0, 0, 0))
    tok = lambda width, col: pl.BlockSpec((1, tm, width), lambda b, i: (b, i, col))
    return pl.pallas_call(
        functools.partial(_merge_kernel, kind=kind, da=da, db=db),
        grid=(B, T // tm),
        in_specs=[tok(wa, 0), tok(wa, 0), tok(wb, 0), tok(wb, 0), tok(wa, ga_off // wa), tok(wb, gb_off // wb),
                  pl.BlockSpec((1, wa), lambda b, i: (0, 0)),
                  pl.BlockSpec((1, wb), lambda b, i: (0, 0)),
                  tok(D, 0),
                  pl.BlockSpec((1, 1, D), mod),
                  pl.BlockSpec((wa + wb, D), lambda b, i: (0, 0))],
        out_specs=tok(D, 0),
        out_shape=jax.ShapeDtypeStruct((B, T, D), F32),
        name="merge_outproj",
        compiler_params=_params(("parallel", "parallel")),
    )(a_dirs[0], a_dirs[1], b_dirs[0], b_dirs[1], pm, pm, na.reshape(1, wa), nb.reshape(1, wb), x, gate, w)


def _ffn_kernel(x_ref, g_ref, sc_ref, sh_ref, gate_ref, wg_ref, wu_ref, wd_ref, fg_ref, o_ref, h_ref,
                *, final_norm):
    k = pl.program_id(2)

    @pl.when(k == 0)
    def _():
        h_ref[...] = _modulated_norm(x_ref[0], g_ref[...], sc_ref[0], sh_ref[0]).astype(BF16)
        o_ref[...] = jnp.zeros_like(o_ref)

    h = h_ref[...]
    acts = []
    for lo in range(0, wg_ref.shape[1], FFN_SUB):
        a = _dot(h, wg_ref[:, lo:lo + FFN_SUB])
        u = _dot(h, wu_ref[:, lo:lo + FFN_SUB])
        acts.append((a * jax.nn.sigmoid(a) * u).astype(BF16))
    o_ref[0] += _dot(jnp.concatenate(acts, axis=1), wd_ref[...])

    @pl.when(k == pl.num_programs(2) - 1)
    def _():
        r = x_ref[0] + gate_ref[0] * o_ref[0]
        if final_norm:
            r = r * lax.rsqrt(jnp.mean(r * r, axis=-1, keepdims=True) + EPS) * fg_ref[...]
        o_ref[0] = r


def ffn(x, g, sc, sh, gate, w_gu, w_down, final_g=None):
    B, T, D = x.shape
    Fh = w_down.shape[0]
    tm, tf = _tile(T, 1024), _tile(Fh, 512)
    nf = Fh // tf
    mod = (lambda b, i, k: (b, 0, 0)) if sc.shape[0] == B else (lambda b, i, k: (0, 0, 0))
    fg = jnp.ones((1, D), F32) if final_g is None else final_g.reshape(1, D)
    return pl.pallas_call(
        functools.partial(_ffn_kernel, final_norm=final_g is not None),
        grid=(B, T // tm, nf),
        in_specs=[pl.BlockSpec((1, tm, D), lambda b, i, k: (b, i, 0), pipeline_mode=pl.Buffered(1)),
                  pl.BlockSpec((1, D), lambda b, i, k: (0, 0)),
                  pl.BlockSpec((1, 1, D), mod),
                  pl.BlockSpec((1, 1, D), mod),
                  pl.BlockSpec((1, 1, D), mod),
                  pl.BlockSpec((D, tf), lambda b, i, k: (0, k)),
                  pl.BlockSpec((D, tf), lambda b, i, k: (0, k + nf)),
                  pl.BlockSpec((tf, D), lambda b, i, k: (k, 0)),
                  pl.BlockSpec((1, D), lambda b, i, k: (0, 0))],
        out_specs=pl.BlockSpec((1, tm, D), lambda b, i, k: (b, i, 0)),
        out_shape=jax.ShapeDtypeStruct((B, T, D), F32),
        scratch_shapes=[pltpu.VMEM((tm, D), BF16)],
        name="ffn",
        compiler_params=_params(("parallel", "parallel", "arbitrary"), FFN_VMEM_LIMIT),
    )(x, g.reshape(1, D), sc, sh, gate, w_gu, w_gu, w_down, fg)


def _incl_mask(rev):
    r = lax.broadcasted_iota(jnp.int32, (CHUNK, CHUNK), 0)
    c = lax.broadcasted_iota(jnp.int32, (CHUNK, CHUNK), 1)
    return (r <= c) if rev else (r >= c)


def _split_bf16(x):
    hi = x.astype(BF16)
    return hi, (x - hi.astype(F32)).astype(BF16)


def _chunk_cumsum(mask_bf16, x):
    hi, lo = _split_bf16(x)
    return _dot(mask_bf16, hi) + _dot(mask_bf16, lo)


def _log_sigmoid(z):
    return jnp.minimum(z, 0.0) - jnp.log1p(jnp.exp(-jnp.abs(z)))


def _softplus(x):
    return jnp.maximum(x, 0.0) + jnp.log1p(jnp.exp(-jnp.abs(x)))


def _time_block(T):
    return _tile(T, 512)


def _dwconv(x, w_ref, seg):
    n = x.shape[0]
    taps = w_ref.shape[0]
    pos = lax.broadcasted_iota(jnp.int32, (n, 1), 0) % seg
    out = None
    for j in range(taps):
        sh = j - taps // 2
        xs = x if sh == 0 else pltpu.roll(x, (-sh) % n, axis=0)
        ok = (pos + sh >= 0) & (pos + sh < seg)
        term = jnp.where(ok, xs, 0.0) * w_ref[j:j + 1, :]
        out = term if out is None else out + term
    return out


def _gla_kernel(q_ref, k_ref, v_ref, ps_ref, wup_ref, bup_ref, s0_ref, o_ref, st_ref, lg_ref, *, rev, heads, nc):
    @pl.when(pl.program_id(1) == 0)
    def _():
        st_ref[...] = s0_ref[...]

    mask = _incl_mask(rev)
    mask_bf = jnp.where(mask, 1.0, 0.0).astype(BF16)
    z = _dot(ps_ref[0].astype(BF16), wup_ref[...]) + bup_ref[...]
    lg_ref[...] = _log_sigmoid(z) * (1.0 / GLA_TAU)

    def body(ci, carry):
        c = (nc - 1 - ci) if rev else ci
        rows = pl.ds(pl.multiple_of(c * CHUNK, CHUNK), CHUNK)
        lg = lg_ref[rows, :]
        b = _chunk_cumsum(mask_bf, lg)
        b_end = jnp.sum(lg, axis=0, keepdims=True)
        k = k_ref[0, rows, :]
        qe = (q_ref[0, rows, :] * (GLA_DK ** -0.5) * jnp.exp(b)).astype(BF16)
        ke = (k * jnp.exp(-b)).astype(BF16)
        kd = (k * jnp.exp(b_end - b)).astype(BF16)
        hs = range(heads)
        ks = [slice(h * GLA_DK, (h + 1) * GLA_DK) for h in hs]
        vs = [slice(h * GLA_DV, (h + 1) * GLA_DV) for h in hs]
        v = [v_ref[0, rows, vs[h]].astype(BF16) for h in hs]
        st = [st_ref[0, h] for h in hs]
        att = [_dot(qe[:, ks[h]], ke[:, ks[h]], NT) for h in hs]
        inter = [_dot(qe[:, ks[h]], st[h].astype(BF16), NT) for h in hs]
        upd = [_dot(v[h], kd[:, ks[h]], TN) for h in hs]
        att = [jnp.where(mask, t, 0.0).astype(BF16) for t in att]
        for h in hs:
            o_ref[0, rows, vs[h]] = _dot(att[h], v[h]) + inter[h]
            st_ref[0, h] = st[h] * jnp.exp(b_end[:, ks[h]]) + upd[h]
        return carry

    lax.fori_loop(0, nc, body, 0)


def gla_scan(pm, ps, wup, bup, s0, rev, q_off, k_off, v_off, heads):
    B, T, _ = pm.shape
    Tb = _time_block(T)
    NB = T // Tb
    kw, vw = heads * GLA_DK, heads * GLA_DV
    assert q_off % kw == 0 and k_off % kw == 0 and v_off % vw == 0
    blk = (lambda i: NB - 1 - i) if rev else (lambda i: i)
    return pl.pallas_call(
        functools.partial(_gla_kernel, rev=rev, heads=heads, nc=Tb // CHUNK),
        grid=(B, NB),
        in_specs=[pl.BlockSpec((1, Tb, kw), lambda b, i: (b, blk(i), q_off // kw)),
                  pl.BlockSpec((1, Tb, kw), lambda b, i: (b, blk(i), k_off // kw)),
                  pl.BlockSpec((1, Tb, vw), lambda b, i: (b, blk(i), v_off // vw)),
                  pl.BlockSpec((1, Tb, ps.shape[-1]), lambda b, i: (b, blk(i), 0)),
                  pl.BlockSpec(wup.shape, lambda b, i: (0, 0)),
                  pl.BlockSpec(bup.shape, lambda b, i: (0, 0)),
                  pl.BlockSpec((1,) + s0.shape[1:], lambda b, i: (b, 0, 0, 0))],
        out_specs=[pl.BlockSpec((1, Tb, vw), lambda b, i: (b, blk(i), 0)),
                   pl.BlockSpec((1,) + s0.shape[1:], lambda b, i: (b, 0, 0, 0))],
        out_shape=[jax.ShapeDtypeStruct((B, T, vw), F32), jax.ShapeDtypeStruct(s0.shape, F32)],
        scratch_shapes=[pltpu.VMEM((Tb, kw), F32)],
        name="gla_scan",
        compiler_params=_params(("parallel", "arbitrary")),
    )(pm, pm, pm, ps, wup, bup, s0)


def gla_bidir(pm_c, ps_c, pm_l, ps_l, w_up, b_up, q_off, k_off, v_off):
    B = pm_l.shape[0]
    rank, hk = w_up.shape[1], w_up.shape[2]
    heads = hk // GLA_DK
    out = []
    for d in (0, 1):
        wup = jnp.zeros((ps_l.shape[-1], hk), F32).at[d * rank:(d + 1) * rank].set(w_up[d]).astype(BF16)
        bup = b_up[d].reshape(1, hk)
        s0 = jnp.zeros((B, heads, GLA_DV, GLA_DK), F32)
        oc, s1 = gla_scan(pm_c, ps_c, wup, bup, s0, bool(d), q_off, k_off, v_off, heads)
        ol, _ = gla_scan(pm_l, ps_l, wup, bup, s1, bool(d), q_off, k_off, v_off, heads)
        out.append((oc, ol))
    return out


def _mlstm_kernel(q_ref, k_ref, v_ref, ps_ref, gb_ref, c0_ref, nm0_ref, o_ref, c_ref, nm_ref,
                  *, rev, heads, nc, ig_col, fg_col):
    @pl.when(pl.program_id(1) == 0)
    def _():
        c_ref[...] = c0_ref[...]
        nm_ref[...] = nm0_ref[...]

    mask = _incl_mask(rev)
    mask_bf = jnp.where(mask, 1.0, 0.0).astype(BF16)

    def body(ci, carry):
        c = (nc - 1 - ci) if rev else ci
        rows = pl.ds(pl.multiple_of(c * CHUNK, CHUNK), CHUNK)
        g = ps_ref[0, rows, :] + gb_ref[...]
        lf = _log_sigmoid(g)
        bcum = _chunk_cumsum(mask_bf, lf)
        b_end = jnp.sum(lf, axis=0, keepdims=True)
        bcum_t, g_t = bcum.T, g.T
        hs = range(heads)
        cols = [(ig_col + h, fg_col + h) for h in hs]
        b_col = [bcum[:, fc:fc + 1] for _, fc in cols]
        b_l = [b_end[:, fc:fc + 1] for _, fc in cols]
        q = [q_ref[0, rows, h * ML_DK:(h + 1) * ML_DK] * (ML_DK ** -0.5) for h in hs]
        k = [k_ref[0, rows, h * ML_DK:(h + 1) * ML_DK] for h in hs]
        v = [v_ref[0, rows, h * ML_DV:(h + 1) * ML_DV] for h in hs]
        qb = [t.astype(BF16) for t in q]
        kb = [t.astype(BF16) for t in k]
        cst = [c_ref[0, h] for h in hs]
        n = [nm_ref[0, h, 0:1, :] for h in hs]
        m = [nm_ref[0, h, 1:2, 0:1] for h in hs]
        qk = [_dot(qb[h], kb[h], NT) for h in hs]
        q_c = [_dot(qb[h], cst[h].astype(BF16), NT) for h in hs]
        q_n = [jnp.sum(q[h] * n[h], axis=-1, keepdims=True) for h in hs]
        d_log = [jnp.where(mask, b_col[h] - bcum_t[fc:fc + 1, :] + g_t[ic:ic + 1, :], -jnp.inf)
                 for h, (ic, fc) in enumerate(cols)]
        inter_log = [b_col[h] + m[h] for h in hs]
        m_out = [jnp.maximum(jnp.max(d_log[h], axis=-1, keepdims=True), inter_log[h]) for h in hs]
        s = [qk[h] * jnp.exp(d_log[h] - m_out[h]) for h in hs]
        e_inter = [jnp.exp(inter_log[h] - m_out[h]) for h in hs]
        num = [_dot(s[h].astype(BF16), v[h].astype(BF16)) + e_inter[h] * q_c[h] for h in hs]
        den = [jnp.sum(s[h], axis=-1, keepdims=True) + e_inter[h] * q_n[h] for h in hs]
        for h in hs:
            o_ref[0, rows, h * ML_DV:(h + 1) * ML_DV] = num[h] / jnp.maximum(jnp.abs(den[h]), jnp.exp(-m_out[h]))
        w_log = [b_l[h] - b_col[h] + g[:, ic:ic + 1] for h, (ic, _) in enumerate(cols)]
        m_loc = [jnp.max(t, axis=0, keepdims=True) for t in w_log]
        w = [jnp.exp(w_log[h] - m_loc[h]) for h in hs]
        m_new = [jnp.maximum(b_l[h] + m[h], m_loc[h]) for h in hs]
        a = [jnp.exp(b_l[h] + m[h] - m_new[h]) for h in hs]
        sc = [jnp.exp(m_loc[h] - m_new[h]) for h in hs]
        upd = [_dot((w[h] * v[h]).astype(BF16), kb[h], TN) for h in hs]
        for h in hs:
            c_ref[0, h] = a[h] * cst[h] + sc[h] * upd[h]
            nm_ref[0, h, 0:1, :] = a[h] * n[h] + sc[h] * jnp.sum(w[h] * k[h], axis=0, keepdims=True)
            nm_ref[0, h, 1:2, :] = jnp.broadcast_to(m_new[h], (1, ML_DK))
        return carry

    lax.fori_loop(0, nc, body, 0)


def mlstm_scan(pm, ps, gbias, state, rev, q_off, k_off, v_off, heads, ig_col, fg_col):
    B, T, _ = pm.shape
    Tb = _time_block(T)
    NB = T // Tb
    kw, vw = heads * ML_DK, heads * ML_DV
    assert q_off % kw == 0 and k_off % kw == 0 and v_off % vw == 0
    c0, nm0 = state
    blk = (lambda i: NB - 1 - i) if rev else (lambda i: i)
    st_spec = lambda a: pl.BlockSpec((1,) + a.shape[1:], lambda b, i: (b, 0, 0, 0))
    o, c1, nm1 = pl.pallas_call(
        functools.partial(_mlstm_kernel, rev=rev, heads=heads, nc=Tb // CHUNK, ig_col=ig_col, fg_col=fg_col),
        grid=(B, NB),
        in_specs=[pl.BlockSpec((1, Tb, kw), lambda b, i: (b, blk(i), q_off // kw)),
                  pl.BlockSpec((1, Tb, kw), lambda b, i: (b, blk(i), k_off // kw)),
                  pl.BlockSpec((1, Tb, vw), lambda b, i: (b, blk(i), v_off // vw)),
                  pl.BlockSpec((1, Tb, ps.shape[-1]), lambda b, i: (b, blk(i), 0)),
                  pl.BlockSpec(gbias.shape, lambda b, i: (0, 0)),
                  st_spec(c0), st_spec(nm0)],
        out_specs=[pl.BlockSpec((1, Tb, vw), lambda b, i: (b, blk(i), 0)), st_spec(c0), st_spec(nm0)],
        out_shape=[jax.ShapeDtypeStruct((B, T, vw), F32), jax.ShapeDtypeStruct(c0.shape, F32),
                   jax.ShapeDtypeStruct(nm0.shape, F32)],
        name="mlstm_scan",
        compiler_params=_params(("parallel", "arbitrary")),
    )(pm, pm, pm, ps, gbias, c0, nm0)
    return o, (c1, nm1)


def mlstm_bidir(pm_c, ps_c, pm_l, ps_l, ig_b, fg_b, q_off, k_off, v_off):
    B = pm_l.shape[0]
    heads = ig_b.shape[-1]
    ncol = ps_l.shape[-1]
    gbias = jnp.zeros((1, ncol), F32).at[0, :4 * heads].set(jnp.concatenate([ig_b.reshape(-1), fg_b.reshape(-1)]))
    out = []
    for d in (0, 1):
        st = (jnp.zeros((B, heads, ML_DV, ML_DK), F32), jnp.zeros((B, heads, 8, ML_DK), F32))
        args = (bool(d), q_off, k_off, v_off, heads, d * heads, 2 * heads + d * heads)
        oc, st = mlstm_scan(pm_c, ps_c, gbias, st, *args)
        ol, _ = mlstm_scan(pm_l, ps_l, gbias, st, *args)
        out.append((oc, ol))
    return out


def _block_join_masks():
    r = lax.broadcasted_iota(jnp.int32, (CHUNK, CHUNK), 0)
    c = lax.broadcasted_iota(jnp.int32, (CHUNK, CHUNK), 1)
    masks, s = [], 1
    while s < CHUNK:
        masks.append(jnp.where((r // (2 * s) == c // (2 * s)) & (r // s != c // s), 1.0, 0.0))
        s *= 2
    return masks


def _unit_tri_inverses(a_list, join_masks):
    r = lax.broadcasted_iota(jnp.int32, (CHUNK, CHUNK), 0)
    c = lax.broadcasted_iota(jnp.int32, (CHUNK, CHUNK), 1)
    eye = jnp.where(r == c, 1.0, 0.0)
    ts = [eye - a * join_masks[0] for a in a_list]
    for m in join_masks[1:]:
        tb = [t.astype(BF16) for t in ts]
        xs = [_dot(t, (a * m).astype(BF16)) for t, a in zip(tb, a_list)]
        ts = [t - _dot(x.astype(BF16), t_b) for t, x, t_b in zip(ts, xs, tb)]
    return ts


def _delta_kernel(q_ref, k_ref, v_ref, ps_ref, gp_ref, cw_ref, s0_ref, o_ref, st_ref, qc_ref, kc_ref, vc_ref,
                  *, rev, heads, nc, seg, g_col, beta_col):
    @pl.when(pl.program_id(1) == 0)
    def _():
        st_ref[...] = s0_ref[...]

    hw = heads * DN_DK
    conv = lambda ref, lo: jax.nn.silu(_dwconv(ref[0], cw_ref.at[:, lo:lo + ref.shape[-1]], seg))
    qa, ka = conv(q_ref, 0), conv(k_ref, hw)
    vc_ref[...] = conv(v_ref, 2 * hw)
    for h in range(heads):
        ks = slice(h * DN_DK, (h + 1) * DN_DK)
        qh, kh = qa[:, ks], ka[:, ks]
        qc_ref[:, ks] = qh * lax.rsqrt(jnp.sum(qh * qh, axis=-1, keepdims=True) + EPS) * (DN_DK ** -0.5)
        kc_ref[:, ks] = kh * lax.rsqrt(jnp.sum(kh * kh, axis=-1, keepdims=True) + EPS)

    mask = _incl_mask(rev)
    mask_bf = jnp.where(mask, 1.0, 0.0).astype(BF16)
    r = lax.broadcasted_iota(jnp.int32, (CHUNK, CHUNK), 0)
    c = lax.broadcasted_iota(jnp.int32, (CHUNK, CHUNK), 1)
    off_diag = jnp.where(r != c, 1.0, 0.0)
    join_masks = _block_join_masks()
    L = CHUNK

    def body(ci, carry):
        cidx = (nc - 1 - ci) if rev else ci
        rows = pl.ds(pl.multiple_of(cidx * CHUNK, CHUNK), CHUNK)
        pre = ps_ref[0, rows, :]
        g = gp_ref[1:2, :] * _softplus(pre + gp_ref[0:1, :])
        beta = jax.nn.sigmoid(pre)
        gcum = _chunk_cumsum(mask_bf, g)
        g_end = jnp.sum(g, axis=0, keepdims=True)
        gcum_t = gcum.T
        e_cum, e_rest, e_end = jnp.exp(gcum), jnp.exp(g_end - gcum), jnp.exp(g_end)

        q_l, k_l, kbeta_l, vbeta_l, a_l, qk_l = [], [], [], [], [], []
        for h in range(heads):
            gc, bc = g_col + h, beta_col + h
            ks = slice(h * DN_DK, (h + 1) * DN_DK)
            q, k = qc_ref[rows, ks], kc_ref[rows, ks]
            b_c = beta[:, bc:bc + 1]
            decay = jnp.exp(jnp.where(mask, gcum[:, gc:gc + 1] - gcum_t[gc:gc + 1, :], -jnp.inf))
            kbeta = k * b_c
            both = _dot(jnp.concatenate([kbeta, q], axis=0).astype(BF16), k.astype(BF16), NT)
            q_l.append(q)
            k_l.append(k)
            kbeta_l.append(kbeta)
            vbeta_l.append(vc_ref[rows, h * DN_DV:(h + 1) * DN_DV] * b_c)
            a_l.append(both[:L] * decay * off_diag)
            qk_l.append((both[L:] * decay).astype(BF16))
        t_l = _unit_tri_inverses(a_l, join_masks)
        for h in range(heads):
            gc = g_col + h
            vs = slice(h * DN_DV, (h + 1) * DN_DV)
            rhs = jnp.concatenate([vbeta_l[h], kbeta_l[h] * e_cum[:, gc:gc + 1]], axis=1).astype(BF16)
            uw = _dot(t_l[h].astype(BF16), rhs)
            st = st_ref[0, h]
            lhs = jnp.concatenate([uw[:, DN_DV:], q_l[h] * e_cum[:, gc:gc + 1]], axis=0).astype(BF16)
            ws = _dot(lhs, st.astype(BF16))
            v_new = (uw[:, :DN_DV] - ws[:L]).astype(BF16)
            o_ref[0, rows, vs] = ws[L:] + _dot(qk_l[h], v_new)
            kd = (k_l[h] * e_rest[:, gc:gc + 1]).astype(BF16)
            st_ref[0, h] = e_end[:, gc:gc + 1] * st + _dot(kd, v_new, TN)
        return carry

    lax.fori_loop(0, nc, body, 0)


def delta_scan(pm, ps, gparams, conv_w, s0, rev, seg, q_off, k_off, v_off, heads, g_col, beta_col):
    B, T, _ = pm.shape
    Tb = _time_block(T)
    NB = T // Tb
    assert Tb % seg == 0 or (NB == 1 and seg == T)
    kw, vw = heads * DN_DK, heads * DN_DV
    assert q_off % kw == 0 and k_off % kw == 0 and v_off % vw == 0
    blk = (lambda i: NB - 1 - i) if rev else (lambda i: i)
    st_spec = pl.BlockSpec((1,) + s0.shape[1:], lambda b, i: (b, 0, 0, 0))
    return pl.pallas_call(
        functools.partial(_delta_kernel, rev=rev, heads=heads, nc=Tb // CHUNK, seg=seg, g_col=g_col,
                          beta_col=beta_col),
        grid=(B, NB),
        in_specs=[pl.BlockSpec((1, Tb, kw), lambda b, i: (b, blk(i), q_off // kw)),
                  pl.BlockSpec((1, Tb, kw), lambda b, i: (b, blk(i), k_off // kw)),
                  pl.BlockSpec((1, Tb, vw), lambda b, i: (b, blk(i), v_off // vw)),
                  pl.BlockSpec((1, Tb, ps.shape[-1]), lambda b, i: (b, blk(i), 0)),
                  pl.BlockSpec(gparams.shape, lambda b, i: (0, 0)),
                  pl.BlockSpec(conv_w.shape, lambda b, i: (0, 0)),
                  st_spec],
        out_specs=[pl.BlockSpec((1, Tb, vw), lambda b, i: (b, blk(i), 0)), st_spec],
        out_shape=[jax.ShapeDtypeStruct((B, T, vw), F32), jax.ShapeDtypeStruct(s0.shape, F32)],
        scratch_shapes=[pltpu.VMEM((Tb, kw), F32), pltpu.VMEM((Tb, kw), F32), pltpu.VMEM((Tb, vw), F32)],
        name="delta_scan",
        compiler_params=_params(("parallel", "arbitrary")),
    )(pm, pm, pm, ps, gparams, conv_w, s0)


def delta_bidir(pm_c, ps_c, pm_l, ps_l, conv_w, a_log, dt_bias, q_off, k_off, v_off, col0):
    B = pm_l.shape[0]
    heads = a_log.shape[-1]
    ncol = ps_l.shape[-1]
    gparams = jnp.zeros((8, ncol), F32)
    gparams = gparams.at[0, col0:col0 + 2 * heads].set(dt_bias.reshape(-1))
    gparams = gparams.at[1, col0:col0 + 2 * heads].set(-jnp.exp(a_log).reshape(-1))
    out = []
    for d in (0, 1):
        s0 = jnp.zeros((B, heads, DN_DK, DN_DV), F32)
        args = (q_off, k_off, v_off, heads, col0 + d * heads, col0 + 2 * heads + d * heads)
        oc, s1 = delta_scan(pm_c, ps_c, gparams, conv_w, s0, bool(d), pm_c.shape[1], *args)
        ol, _ = delta_scan(pm_l, ps_l, gparams, conv_w, s1, bool(d), GRID_W, *args)
        out.append((oc, ol))
    return out


def _rglru_kernel(x_ref, cw_ref, cb_ref, wr_ref, br_ref, wi_ref, bi_ref, lam_ref, h0_ref, o_ref, hT_ref,
                  *, rev, nc, seg):
    @pl.when(pl.program_id(2) == 0)
    def _():
        hT_ref[...] = h0_ref[...]

    n = x_ref.shape[1]
    xc = _dwconv(x_ref[0], cw_ref, seg) + cb_ref[...]
    xb = xc.astype(BF16)
    r = jax.nn.sigmoid(_dot(xb, wr_ref[0].astype(BF16)) + br_ref[...])
    i = jax.nn.sigmoid(_dot(xb, wi_ref[0].astype(BF16)) + bi_ref[...])
    log_a = -RG_C * r * _softplus(-lam_ref[...])
    a = jnp.exp(log_a)
    bx = jnp.sqrt(1.0 - jnp.exp(2.0 * log_a)) * (i * xc)

    pos = lax.broadcasted_iota(jnp.int32, (n, 1), 0) % CHUNK
    s = 1
    while s < CHUNK:
        ok = (pos < CHUNK - s) if rev else (pos >= s)
        shift = (n - s) if rev else s
        a_sh = pltpu.roll(a, shift, axis=0)
        b_sh = pltpu.roll(bx, shift, axis=0)
        bx = jnp.where(ok, a * b_sh + bx, bx)
        a = jnp.where(ok, a * a_sh, a)
        s *= 2

    h = hT_ref[0]
    for ci in range(nc):
        c = (nc - 1 - ci) if rev else ci
        rows = slice(c * CHUNK, (c + 1) * CHUNK)
        hc = bx[rows] + a[rows] * h
        o_ref[0, rows, :] = hc
        h = hc[0:1] if rev else hc[CHUNK - 1:CHUNK]
    hT_ref[0] = h


def rglru_scan(pm, conv_w, conv_b, w_r, b_r, w_i, b_i, lam, h0, rev, seg, width):
    B, T, _ = pm.shape
    nblk, bw = w_r.shape[0], w_r.shape[1]
    assert bw % LANES == 0 and nblk * bw == width
    Tb = _tile(T, 1024)
    NB = T // Tb
    assert Tb % seg == 0 or (NB == 1 and seg == T)
    blk = (lambda i: NB - 1 - i) if rev else (lambda i: i)
    row = pl.BlockSpec((1, bw), lambda b, g, i: (0, g))
    return pl.pallas_call(
        functools.partial(_rglru_kernel, rev=rev, nc=Tb // CHUNK, seg=seg),
        grid=(B, nblk, NB),
        in_specs=[pl.BlockSpec((1, Tb, bw), lambda b, g, i: (b, blk(i), g)),
                  pl.BlockSpec((conv_w.shape[0], bw), lambda b, g, i: (0, g)),
                  row,
                  pl.BlockSpec((1, bw, bw), lambda b, g, i: (g, 0, 0)), row,
                  pl.BlockSpec((1, bw, bw), lambda b, g, i: (g, 0, 0)), row,
                  row,
                  pl.BlockSpec((1, 1, bw), lambda b, g, i: (b, 0, g))],
        out_specs=[pl.BlockSpec((1, Tb, bw), lambda b, g, i: (b, blk(i), g)),
                   pl.BlockSpec((1, 1, bw), lambda b, g, i: (b, 0, g))],
        out_shape=[jax.ShapeDtypeStruct((B, T, width), F32), jax.ShapeDtypeStruct(h0.shape, F32)],
        name="rglru_scan",
        compiler_params=_params(("parallel", "parallel", "arbitrary")),
    )(pm, conv_w, conv_b, w_r, b_r, w_i, b_i, lam, h0)


def rglru_bidir(pm_c, pm_l, conv_w, conv_b, w_r, b_r, w_i, b_i, lam):
    B = pm_l.shape[0]
    width = conv_b.shape[-1]
    out = []
    for d in (0, 1):
        h0 = jnp.zeros((B, 1, width), F32)
        args = (conv_w, conv_b.reshape(1, width), w_r[d], b_r[d].reshape(1, width), w_i[d],
                b_i[d].reshape(1, width), lam[d].reshape(1, width))
        oc, h1 = rglru_scan(pm_c, *args, h0, bool(d), pm_c.shape[1], width)
        ol, _ = rglru_scan(pm_l, *args, h1, bool(d), GRID_W, width)
        out.append((oc, ol))
    return out


def _pad_cols(w, mult):
    return jnp.pad(w, ((0, 0), (0, (-w.shape[-1]) % mult)))


def kernel(x, c, ctx, c_ctx, mod_w, mod_b, norm_mix_g, norm_ffn_g, ffn_w_gu, ffn_w_down, final_norm_g,
           ab_w_in, ab_w_out, ml_ig_b, ml_fg_b, ml_norm_g, dn_conv_w, dn_a_log, dn_dt_bias, dn_norm_g,
           cd_w_in, cd_w_out, rg_conv_w, rg_conv_b, rg_w_r, rg_b_r, rg_w_i, rg_b_i, rg_lambda,
           gla_w_up, gla_b_up, gla_norm_g):
    B, T, D = x.shape
    depth = mod_w.shape[0]
    mlh, dnh = ml_ig_b.shape[-1], dn_a_log.shape[-1]
    rgw = rg_conv_b.shape[-1]
    glh = gla_w_up.shape[-1] // GLA_DK

    cond = jnp.concatenate([c, c_ctx[None, :], jnp.zeros((8 - B - 1, D), F32)], axis=0)

    m_all = adaln_all(cond, mod_w, mod_b)

    xc, xl = ctx, x
    for layer in range(depth):
        last = layer == depth - 1
        j = layer // 2
        m = m_all[layer]
        lat = [t[:, None, :] for t in jnp.split(m[:B], 6, axis=-1)]
        cx = [t[:, None, :] for t in jnp.split(m[B:B + 1], 6, axis=-1)]

        if layer % 2 == 0:
            w_in = ab_w_in[j]
            s = np.cumsum((0, mlh * ML_DK, mlh * ML_DK, mlh * ML_DV, mlh * ML_DV, 2 * mlh, 2 * mlh,
                           dnh * DN_DK, dnh * DN_DK, dnh * DN_DV, dnh * DN_DV, 2 * dnh, 2 * dnh))
            w_main = jnp.concatenate([w_in[:, s[0]:s[4]], w_in[:, s[6]:s[10]]], axis=1)
            w_small = jnp.concatenate([w_in[:, s[4]:s[6]], w_in[:, s[10]:s[12]]], axis=1)
            w_out = ab_w_out[j]
            o = np.cumsum((0, mlh * ML_DK, mlh * ML_DK, mlh * ML_DV, mlh * ML_DV,
                           dnh * DN_DK, dnh * DN_DK, dnh * DN_DV))
        else:
            w_in = cd_w_in[j]
            o = np.cumsum((0, rgw, rgw, glh * GLA_DK, glh * GLA_DK, glh * GLA_DV))
            nmain = int(o[5]) + glh * GLA_DV
            w_main, w_small = w_in[:, :nmain], w_in[:, nmain:]
            w_out = cd_w_out[j]
        o = [int(v) for v in o]
        w_main = w_main.astype(BF16)
        w_small = _pad_cols(w_small, LANES).astype(BF16)
        w_out = w_out.astype(BF16)

        pm_l, ps_l = inproj(xl, norm_mix_g[layer], lat[1], lat[0], w_main, w_small)
        pm_c, ps_c = inproj(xc, norm_mix_g[layer], cx[1], cx[0], w_main, w_small)

        if layer % 2 == 0:
            mix_a = mlstm_bidir(pm_c, ps_c, pm_l, ps_l, ml_ig_b[j], ml_fg_b[j], o[0], o[1], o[2])
            mix_b = delta_bidir(pm_c, ps_c, pm_l, ps_l, dn_conv_w[j], dn_a_log[j], dn_dt_bias[j],
                                o[4], o[5], o[6], 4 * mlh)
            margs = (0, o[3], o[7], ml_norm_g[j], dn_norm_g[j], ML_DV, DN_DV)
        else:
            mix_a = rglru_bidir(pm_c, pm_l, rg_conv_w[j], rg_conv_b[j], rg_w_r[j], rg_b_r[j], rg_w_i[j],
                                rg_b_i[j], rg_lambda[j])
            mix_b = gla_bidir(pm_c, ps_c, pm_l, ps_l, gla_w_up[j], gla_b_up[j], o[2], o[3], o[4])
            margs = (1, o[1], o[5], jnp.ones((rgw,), F32), gla_norm_g[j], rgw, GLA_DV)

        def merged(which, pm, xres, gate):
            kind, ga_off, gb_off, na, nb, da, db = margs
            a_dirs = (mix_a[0][which], mix_a[1][which])
            b_dirs = (mix_b[0][which], mix_b[1][which])
            return merge_outproj(kind, a_dirs, b_dirs, pm, ga_off, gb_off, na, nb, da, db, xres, gate, w_out)

        w_gu = ffn_w_gu[layer].astype(BF16)
        w_dn = ffn_w_down[layer].astype(BF16)
        xl = merged(1, pm_l, xl, lat[2])
        xl = ffn(xl, norm_ffn_g[layer], lat[4], lat[3], lat[5], w_gu, w_dn, final_norm_g if last else None)
        if not last:
            xc = merged(0, pm_c, xc, cx[2])
            xc = ffn(xc, norm_ffn_g[layer], cx[4], cx[3], cx[5], w_gu, w_dn)
    return xl
```

```python
import functools

import jax
import jax.numpy as jnp
import numpy as np
from jax import lax
from jax.experimental import pallas as pl
from jax.experimental.pallas import tpu as pltpu

F32 = jnp.float32
BF16 = jnp.bfloat16

GRID_W = 64
CHUNK = 64
EPS = 1e-6
ML_DK, ML_DV = 128, 256
DN_DK, DN_DV = 128, 128
GLA_DK, GLA_DV = 128, 256
RG_C = 8.0
GLA_TAU = 16.0

LANES = 128
DELTA_GROUP = 4
GLA_GROUP = 4
FFN_SUB = 256
VMEM_LIMIT = 48 * 1024 * 1024

NT = (((1,), (1,)), ((), ()))
TN = (((0,), (0,)), ((), ()))


def _tile(n, pref):
    t = min(n, pref)
    while n % t:
        t //= 2
    return t


def _params(sem, vmem=VMEM_LIMIT):
    return pltpu.CompilerParams(dimension_semantics=sem, vmem_limit_bytes=vmem)


def _dot(a, b, dims=None):
    if dims is None:
        return jnp.dot(a, b, preferred_element_type=F32)
    return lax.dot_general(a, b, dims, preferred_element_type=F32)


def _modulated_norm(x, g, sc, sh):
    y = x * lax.rsqrt(jnp.mean(x * x, axis=-1, keepdims=True) + EPS) * g
    return y * (1.0 + sc) + sh


def _adaln_kernel(s_ref, w_ref, b_ref, o_ref):
    s = s_ref[...]
    s = (s * jax.nn.sigmoid(s)).astype(BF16)
    o_ref[0] = _dot(s, w_ref[0].astype(BF16)) + b_ref[0]


def adaln_all(cond, w_mod, b_mod):
    R, D = cond.shape
    depth, _, C = w_mod.shape
    tn = _tile(C, 1024)
    return pl.pallas_call(
        _adaln_kernel,
        grid=(depth, C // tn),
        in_specs=[pl.BlockSpec((R, D), lambda l, j: (0, 0)),
                  pl.BlockSpec((1, D, tn), lambda l, j: (l, 0, j)),
                  pl.BlockSpec((1, 1, tn), lambda l, j: (l, 0, j))],
        out_specs=pl.BlockSpec((1, R, tn), lambda l, j: (l, 0, j)),
        out_shape=jax.ShapeDtypeStruct((depth, R, C), F32),
        name="adaln_all",
        compiler_params=_params(("parallel", "parallel")),
    )(cond, w_mod, b_mod.reshape(depth, 1, C))


def _inproj_kernel(x_ref, g_ref, sc_ref, sh_ref, w_ref, ws_ref, o_ref, os_ref, h_ref):
    @pl.when(pl.program_id(2) == 0)
    def _():
        h = _modulated_norm(x_ref[0], g_ref[...], sc_ref[0], sh_ref[0]).astype(BF16)
        h_ref[...] = h
        os_ref[0] = _dot(h, ws_ref[...])

    o_ref[0] = _dot(h_ref[...], w_ref[...])


def inproj(x, g, sc, sh, w_main, w_small):
    B, T, D = x.shape
    C = w_main.shape[1]
    Cs = w_small.shape[1]
    tm, tn = _tile(T, 1024), _tile(C, 1024)
    mod = (lambda b, i, j: (b, 0, 0)) if sc.shape[0] == B else (lambda b, i, j: (0, 0, 0))
    return pl.pallas_call(
        _inproj_kernel,
        grid=(B, T // tm, C // tn),
        in_specs=[pl.BlockSpec((1, tm, D), lambda b, i, j: (b, i, 0)),
                  pl.BlockSpec((1, D), lambda b, i, j: (0, 0)),
                  pl.BlockSpec((1, 1, D), mod),
                  pl.BlockSpec((1, 1, D), mod),
                  pl.BlockSpec((D, tn), lambda b, i, j: (0, j)),
                  pl.BlockSpec((D, Cs), lambda b, i, j: (0, 0))],
        out_specs=[pl.BlockSpec((1, tm, tn), lambda b, i, j: (b, i, j)),
                   pl.BlockSpec((1, tm, Cs), lambda b, i, j: (b, i, 0))],
        out_shape=[jax.ShapeDtypeStruct((B, T, C), F32), jax.ShapeDtypeStruct((B, T, Cs), F32)],
        scratch_shapes=[pltpu.VMEM((tm, D), BF16)],
        name="inproj",
        compiler_params=_params(("parallel", "parallel", "arbitrary")),
    )(x, g.reshape(1, D), sc, sh, w_main, w_small)


def _head_norm(o, d, g):
    parts = []
    for h in range(o.shape[-1] // d):
        oh = o[:, h * d:(h + 1) * d]
        parts.append(oh * lax.rsqrt(jnp.mean(oh * oh, axis=-1, keepdims=True) + EPS))
    return jnp.concatenate(parts, axis=-1) * g


def _merge_kernel(af_ref, ab_ref, bf_ref, bb_ref, ga_ref, gb_ref, na_ref, nb_ref, x_ref, gate_ref, w_ref, o_ref,
                  *, kind, da, db):
    a = af_ref[0] + ab_ref[0]
    b = bf_ref[0] + bb_ref[0]
    if kind == 0:
        ya = _head_norm(a, da, na_ref[...]) * jax.nn.sigmoid(ga_ref[0])
    else:
        ya = jax.nn.gelu(ga_ref[0]) * a
    gb = gb_ref[0]
    yb = _head_norm(b, db, nb_ref[...]) * (gb * jax.nn.sigmoid(gb))
    wa = a.shape[-1]
    y = _dot(ya.astype(BF16), w_ref[0:wa, :]) + _dot(yb.astype(BF16), w_ref[wa:, :])
    o_ref[0] = x_ref[0] + gate_ref[0] * y


def merge_outproj(kind, a_dirs, b_dirs, pm, ga_off, gb_off, na, nb, da, db, x, gate, w):
    B, T, D = x.shape
    wa, wb = a_dirs[0].shape[-1], b_dirs[0].shape[-1]
    assert ga_off % wa == 0 and gb_off % wb == 0
    tm = _tile(T, 256)
    mod = (lambda b, i: (b, 0, 0)) if gate.shape[0] == B else (lambda b, i: (0, 0, 0))
    tok = lambda width, col: pl.BlockSpec((1, tm, width), lambda b, i: (b, i, col))
    return pl.pallas_call(
        functools.partial(_merge_kernel, kind=kind, da=da, db=db),
        grid=(B, T // tm),
        in_specs=[tok(wa, 0), tok(wa, 0), tok(wb, 0), tok(wb, 0), tok(wa, ga_off // wa), tok(wb, gb_off // wb),
                  pl.BlockSpec((1, wa), lambda b, i: (0, 0)),
                  pl.BlockSpec((1, wb), lambda b, i: (0, 0)),
                  tok(D, 0),
                  pl.BlockSpec((1, 1, D), mod),
                  pl.BlockSpec((wa + wb, D), lambda b, i: (0, 0))],
        out_specs=tok(D, 0),
        out_shape=jax.ShapeDtypeStruct((B, T, D), F32),
        name="merge_outproj",
        compiler_params=_params(("parallel", "parallel")),
    )(a_dirs[0], a_dirs[1], b_dirs[0], b_dirs[1], pm, pm, na.reshape(1, wa), nb.reshape(1, wb), x, gate, w)


def _ffn_kernel(x_ref, g_ref, sc_ref, sh_ref, gate_ref, wg_ref, wu_ref, wd_ref, fg_ref, o_ref, h_ref,
                *, final_norm):
    k = pl.program_id(2)

    @pl.when(k == 0)
    def _():
        h_ref[...] = _modulated_norm(x_ref[0], g_ref[...], sc_ref[0], sh_ref[0]).astype(BF16)
        o_ref[...] = jnp.zeros_like(o_ref)

    h = h_ref[...]
    acts = []
    for lo in range(0, wg_ref.shape[1], FFN_SUB):
        a = _dot(h, wg_ref[:, lo:lo + FFN_SUB])
        u = _dot(h, wu_ref[:, lo:lo + FFN_SUB])
        acts.append((a * jax.nn.sigmoid(a) * u).astype(BF16))
    o_ref[0] += _dot(jnp.concatenate(acts, axis=1), wd_ref[...])

    @pl.when(k == pl.num_programs(2) - 1)
    def _():
        r = x_ref[0] + gate_ref[0] * o_ref[0]
        if final_norm:
            r = r * lax.rsqrt(jnp.mean(r * r, axis=-1, keepdims=True) + EPS) * fg_ref[...]
        o_ref[0] = r


def ffn(x, g, sc, sh, gate, w_gu, w_down, final_g=None):
    B, T, D = x.shape
    Fh = w_down.shape[0]
    tm, tf = _tile(T, 512), _tile(Fh, 512)
    nf = Fh // tf
    mod = (lambda b, i, k: (b, 0, 0)) if sc.shape[0] == B else (lambda b, i, k: (0, 0, 0))
    fg = jnp.ones((1, D), F32) if final_g is None else final_g.reshape(1, D)
    return pl.pallas_call(
        functools.partial(_ffn_kernel, final_norm=final_g is not None),
        grid=(B, T // tm, nf),
        in_specs=[pl.BlockSpec((1, tm, D), lambda b, i, k: (b, i, 0)),
                  pl.BlockSpec((1, D), lambda b, i, k: (0, 0)),
                  pl.BlockSpec((1, 1, D), mod),
                  pl.BlockSpec((1, 1, D), mod),
                  pl.BlockSpec((1, 1, D), mod),
                  pl.BlockSpec((D, tf), lambda b, i, k: (0, k)),
                  pl.BlockSpec((D, tf), lambda b, i, k: (0, k + nf)),
                  pl.BlockSpec((tf, D), lambda b, i, k: (k, 0)),
                  pl.BlockSpec((1, D), lambda b, i, k: (0, 0))],
        out_specs=pl.BlockSpec((1, tm, D), lambda b, i, k: (b, i, 0)),
        out_shape=jax.ShapeDtypeStruct((B, T, D), F32),
        scratch_shapes=[pltpu.VMEM((tm, D), BF16)],
        name="ffn",
        compiler_params=_params(("parallel", "parallel", "arbitrary")),
    )(x, g.reshape(1, D), sc, sh, gate, w_gu, w_gu, w_down, fg)


def _incl_mask(rev):
    r = lax.broadcasted_iota(jnp.int32, (CHUNK, CHUNK), 0)
    c = lax.broadcasted_iota(jnp.int32, (CHUNK, CHUNK), 1)
    return (r <= c) if rev else (r >= c)


def _split_bf16(x):
    hi = x.astype(BF16)
    return hi, (x - hi.astype(F32)).astype(BF16)


def _chunk_cumsum(mask_bf16, x):
    hi, lo = _split_bf16(x)
    return _dot(mask_bf16, hi) + _dot(mask_bf16, lo)


def _log_sigmoid(z):
    return jnp.minimum(z, 0.0) - jnp.log1p(jnp.exp(-jnp.abs(z)))


def _softplus(x):
    return jnp.maximum(x, 0.0) + jnp.log1p(jnp.exp(-jnp.abs(x)))


def _time_block(T):
    return _tile(T, 512)


def _dwconv(x, w_ref, seg):
    n = x.shape[0]
    taps = w_ref.shape[0]
    pos = lax.broadcasted_iota(jnp.int32, (n, 1), 0) % seg
    out = None
    for j in range(taps):
        sh = j - taps // 2
        xs = x if sh == 0 else pltpu.roll(x, (-sh) % n, axis=0)
        ok = (pos + sh >= 0) & (pos + sh < seg)
        term = jnp.where(ok, xs, 0.0) * w_ref[j:j + 1, :]
        out = term if out is None else out + term
    return out


def _gla_kernel(q_ref, k_ref, v_ref, ps_ref, wup_ref, bup_ref, s0_ref, o_ref, st_ref, lg_ref, *, rev, heads, nc):
    @pl.when(pl.program_id(1) == 0)
    def _():
        st_ref[...] = s0_ref[...]

    mask = _incl_mask(rev)
    mask_bf = jnp.where(mask, 1.0, 0.0).astype(BF16)
    z = _dot(ps_ref[0].astype(BF16), wup_ref[...]) + bup_ref[...]
    lg_ref[...] = _log_sigmoid(z) * (1.0 / GLA_TAU)

    hs = range(heads)
    ks = [slice(h * GLA_DK, (h + 1) * GLA_DK) for h in hs]
    vs = [slice(h * GLA_DV, (h + 1) * GLA_DV) for h in hs]

    def prepare(rows):
        lg = lg_ref[rows, :]
        b = _chunk_cumsum(mask_bf, lg)
        b_end = jnp.sum(lg, axis=0, keepdims=True)
        k = k_ref[0, rows, :]
        qe = (q_ref[0, rows, :] * (GLA_DK ** -0.5) * jnp.exp(b)).astype(BF16)
        ke = (k * jnp.exp(-b)).astype(BF16)
        kd = (k * jnp.exp(b_end - b)).astype(BF16)
        v = [v_ref[0, rows, vs[h]].astype(BF16) for h in hs]
        att = [_dot(qe[:, ks[h]], ke[:, ks[h]], NT) for h in hs]
        upd = [_dot(v[h], kd[:, ks[h]], TN) for h in hs]
        intra = [_dot(jnp.where(mask, att[h], 0.0).astype(BF16), v[h]) for h in hs]
        return qe, intra, upd, jnp.exp(b_end)

    def body(gi, carry):
        rows_l = []
        for j in range(GLA_GROUP):
            ci = gi * GLA_GROUP + j
            c = (nc - 1 - ci) if rev else ci
            rows_l.append(pl.ds(pl.multiple_of(c * CHUNK, CHUNK), CHUNK))
        preps = [prepare(rows) for rows in rows_l]
        for rows, (qe, intra, upd, e_end) in zip(rows_l, preps):
            for h in hs:
                st = st_ref[0, h]
                o_ref[0, rows, vs[h]] = intra[h] + _dot(qe[:, ks[h]], st.astype(BF16), NT)
                st_ref[0, h] = st * e_end[:, ks[h]] + upd[h]
        return carry

    assert nc % GLA_GROUP == 0
    lax.fori_loop(0, nc // GLA_GROUP, body, 0)


def gla_scan(pm, ps, wup, bup, s0, rev, q_off, k_off, v_off, heads):
    B, T, _ = pm.shape
    Tb = _time_block(T)
    NB = T // Tb
    kw, vw = heads * GLA_DK, heads * GLA_DV
    assert q_off % kw == 0 and k_off % kw == 0 and v_off % vw == 0
    blk = (lambda i: NB - 1 - i) if rev else (lambda i: i)
    return pl.pallas_call(
        functools.partial(_gla_kernel, rev=rev, heads=heads, nc=Tb // CHUNK),
        grid=(B, NB),
        in_specs=[pl.BlockSpec((1, Tb, kw), lambda b, i: (b, blk(i), q_off // kw)),
                  pl.BlockSpec((1, Tb, kw), lambda b, i: (b, blk(i), k_off // kw)),
                  pl.BlockSpec((1, Tb, vw), lambda b, i: (b, blk(i), v_off // vw)),
                  pl.BlockSpec((1, Tb, ps.shape[-1]), lambda b, i: (b, blk(i), 0)),
                  pl.BlockSpec(wup.shape, lambda b, i: (0, 0)),
                  pl.BlockSpec(bup.shape, lambda b, i: (0, 0)),
                  pl.BlockSpec((1,) + s0.shape[1:], lambda b, i: (b, 0, 0, 0))],
        out_specs=[pl.BlockSpec((1, Tb, vw), lambda b, i: (b, blk(i), 0)),
                   pl.BlockSpec((1,) + s0.shape[1:], lambda b, i: (b, 0, 0, 0))],
        out_shape=[jax.ShapeDtypeStruct((B, T, vw), F32), jax.ShapeDtypeStruct(s0.shape, F32)],
        scratch_shapes=[pltpu.VMEM((Tb, kw), F32)],
        name="gla_scan",
        compiler_params=_params(("parallel", "arbitrary")),
    )(pm, pm, pm, ps, wup, bup, s0)


def gla_bidir(pm_c, ps_c, pm_l, ps_l, w_up, b_up, q_off, k_off, v_off):
    B = pm_l.shape[0]
    rank, hk = w_up.shape[1], w_up.shape[2]
    heads = hk // GLA_DK
    out = []
    for d in (0, 1):
        wup = jnp.zeros((ps_l.shape[-1], hk), F32).at[d * rank:(d + 1) * rank].set(w_up[d]).astype(BF16)
        bup = b_up[d].reshape(1, hk)
        s0 = jnp.zeros((B, heads, GLA_DV, GLA_DK), F32)
        oc, s1 = gla_scan(pm_c, ps_c, wup, bup, s0, bool(d), q_off, k_off, v_off, heads)
        ol, _ = gla_scan(pm_l, ps_l, wup, bup, s1, bool(d), q_off, k_off, v_off, heads)
        out.append((oc, ol))
    return out


def _mlstm_kernel(q_ref, k_ref, v_ref, ps_ref, gb_ref, c0_ref, nm0_ref, o_ref, c_ref, nm_ref,
                  *, rev, heads, nc, ig_col, fg_col):
    @pl.when(pl.program_id(1) == 0)
    def _():
        c_ref[...] = c0_ref[...]
        nm_ref[...] = nm0_ref[...]

    mask = _incl_mask(rev)
    mask_bf = jnp.where(mask, 1.0, 0.0).astype(BF16)

    def body(ci, carry):
        c = (nc - 1 - ci) if rev else ci
        rows = pl.ds(pl.multiple_of(c * CHUNK, CHUNK), CHUNK)
        g = ps_ref[0, rows, :] + gb_ref[...]
        lf = _log_sigmoid(g)
        bcum = _chunk_cumsum(mask_bf, lf)
        b_end = jnp.sum(lf, axis=0, keepdims=True)
        bcum_t, g_t = bcum.T, g.T
        hs = range(heads)
        cols = [(ig_col + h, fg_col + h) for h in hs]
        b_col = [bcum[:, fc:fc + 1] for _, fc in cols]
        b_l = [b_end[:, fc:fc + 1] for _, fc in cols]
        q = [q_ref[0, rows, h * ML_DK:(h + 1) * ML_DK] * (ML_DK ** -0.5) for h in hs]
        k = [k_ref[0, rows, h * ML_DK:(h + 1) * ML_DK] for h in hs]
        v = [v_ref[0, rows, h * ML_DV:(h + 1) * ML_DV] for h in hs]
        qb = [t.astype(BF16) for t in q]
        kb = [t.astype(BF16) for t in k]
        cst = [c_ref[0, h] for h in hs]
        n = [nm_ref[0, h, 0:1, :] for h in hs]
        m = [nm_ref[0, h, 1:2, 0:1] for h in hs]
        qk = [_dot(qb[h], kb[h], NT) for h in hs]
        q_c = [_dot(qb[h], cst[h].astype(BF16), NT) for h in hs]
        q_n = [jnp.sum(q[h] * n[h], axis=-1, keepdims=True) for h in hs]
        d_log = [jnp.where(mask, b_col[h] - bcum_t[fc:fc + 1, :] + g_t[ic:ic + 1, :], -jnp.inf)
                 for h, (ic, fc) in enumerate(cols)]
        inter_log = [b_col[h] + m[h] for h in hs]
        m_out = [jnp.maximum(jnp.max(d_log[h], axis=-1, keepdims=True), inter_log[h]) for h in hs]
        s = [qk[h] * jnp.exp(d_log[h] - m_out[h]) for h in hs]
        e_inter = [jnp.exp(inter_log[h] - m_out[h]) for h in hs]
        num = [_dot(s[h].astype(BF16), v[h].astype(BF16)) + e_inter[h] * q_c[h] for h in hs]
        den = [jnp.sum(s[h], axis=-1, keepdims=True) + e_inter[h] * q_n[h] for h in hs]
        for h in hs:
            o_ref[0, rows, h * ML_DV:(h + 1) * ML_DV] = num[h] / jnp.maximum(jnp.abs(den[h]), jnp.exp(-m_out[h]))
        w_log = [b_l[h] - b_col[h] + g[:, ic:ic + 1] for h, (ic, _) in enumerate(cols)]
        m_loc = [jnp.max(t, axis=0, keepdims=True) for t in w_log]
        w = [jnp.exp(w_log[h] - m_loc[h]) for h in hs]
        m_new = [jnp.maximum(b_l[h] + m[h], m_loc[h]) for h in hs]
        a = [jnp.exp(b_l[h] + m[h] - m_new[h]) for h in hs]
        sc = [jnp.exp(m_loc[h] - m_new[h]) for h in hs]
        upd = [_dot((w[h] * v[h]).astype(BF16), kb[h], TN) for h in hs]
        for h in hs:
            c_ref[0, h] = a[h] * cst[h] + sc[h] * upd[h]
            nm_ref[0, h, 0:1, :] = a[h] * n[h] + sc[h] * jnp.sum(w[h] * k[h], axis=0, keepdims=True)
            nm_ref[0, h, 1:2, :] = jnp.broadcast_to(m_new[h], (1, ML_DK))
        return carry

    lax.fori_loop(0, nc, body, 0)


def mlstm_scan(pm, ps, gbias, state, rev, q_off, k_off, v_off, heads, ig_col, fg_col):
    B, T, _ = pm.shape
    Tb = _time_block(T)
    NB = T // Tb
    kw, vw = heads * ML_DK, heads * ML_DV
    assert q_off % kw == 0 and k_off % kw == 0 and v_off % vw == 0
    c0, nm0 = state
    blk = (lambda i: NB - 1 - i) if rev else (lambda i: i)
    st_spec = lambda a: pl.BlockSpec((1,) + a.shape[1:], lambda b, i: (b, 0, 0, 0))
    o, c1, nm1 = pl.pallas_call(
        functools.partial(_mlstm_kernel, rev=rev, heads=heads, nc=Tb // CHUNK, ig_col=ig_col, fg_col=fg_col),
        grid=(B, NB),
        in_specs=[pl.BlockSpec((1, Tb, kw), lambda b, i: (b, blk(i), q_off // kw)),
                  pl.BlockSpec((1, Tb, kw), lambda b, i: (b, blk(i), k_off // kw)),
                  pl.BlockSpec((1, Tb, vw), lambda b, i: (b, blk(i), v_off // vw)),
                  pl.BlockSpec((1, Tb, ps.shape[-1]), lambda b, i: (b, blk(i), 0)),
                  pl.BlockSpec(gbias.shape, lambda b, i: (0, 0)),
                  st_spec(c0), st_spec(nm0)],
        out_specs=[pl.BlockSpec((1, Tb, vw), lambda b, i: (b, blk(i), 0)), st_spec(c0), st_spec(nm0)],
        out_shape=[jax.ShapeDtypeStruct((B, T, vw), F32), jax.ShapeDtypeStruct(c0.shape, F32),
                   jax.ShapeDtypeStruct(nm0.shape, F32)],
        name="mlstm_scan",
        compiler_params=_params(("parallel", "arbitrary")),
    )(pm, pm, pm, ps, gbias, c0, nm0)
    return o, (c1, nm1)


def mlstm_bidir(pm_c, ps_c, pm_l, ps_l, ig_b, fg_b, q_off, k_off, v_off):
    B = pm_l.shape[0]
    heads = ig_b.shape[-1]
    ncol = ps_l.shape[-1]
    gbias = jnp.zeros((1, ncol), F32).at[0, :4 * heads].set(jnp.concatenate([ig_b.reshape(-1), fg_b.reshape(-1)]))
    out = []
    for d in (0, 1):
        st = (jnp.zeros((B, heads, ML_DV, ML_DK), F32), jnp.zeros((B, heads, 8, ML_DK), F32))
        args = (bool(d), q_off, k_off, v_off, heads, d * heads, 2 * heads + d * heads)
        oc, st = mlstm_scan(pm_c, ps_c, gbias, st, *args)
        ol, _ = mlstm_scan(pm_l, ps_l, gbias, st, *args)
        out.append((oc, ol))
    return out


def _block_join_masks():
    r = lax.broadcasted_iota(jnp.int32, (CHUNK, CHUNK), 0)
    c = lax.broadcasted_iota(jnp.int32, (CHUNK, CHUNK), 1)
    masks, s = [], 1
    while s < CHUNK:
        masks.append(jnp.where((r // (2 * s) == c // (2 * s)) & (r // s != c // s), 1.0, 0.0))
        s *= 2
    return masks


def _unit_tri_inverses(a_list, join_masks):
    r = lax.broadcasted_iota(jnp.int32, (CHUNK, CHUNK), 0)
    c = lax.broadcasted_iota(jnp.int32, (CHUNK, CHUNK), 1)
    eye = jnp.where(r == c, 1.0, 0.0)
    ts = [eye - a * join_masks[0] for a in a_list]
    for m in join_masks[1:]:
        tb = [t.astype(BF16) for t in ts]
        xs = [_dot(t, (a * m).astype(BF16)) for t, a in zip(tb, a_list)]
        ts = [t - _dot(x.astype(BF16), t_b) for t, x, t_b in zip(ts, xs, tb)]
    return ts


def _delta_kernel(q_ref, k_ref, v_ref, ps_ref, gp_ref, cw_ref, s0_ref, o_ref, st_ref, qc_ref, kc_ref, vc_ref,
                  *, rev, heads, nc, seg, g_col, beta_col):
    @pl.when(pl.program_id(1) == 0)
    def _():
        st_ref[...] = s0_ref[...]

    hw = heads * DN_DK
    conv = lambda ref, lo: jax.nn.silu(_dwconv(ref[0], cw_ref.at[:, lo:lo + ref.shape[-1]], seg))
    qa, ka = conv(q_ref, 0), conv(k_ref, hw)
    vc_ref[...] = conv(v_ref, 2 * hw)
    for h in range(heads):
        ks = slice(h * DN_DK, (h + 1) * DN_DK)
        qh, kh = qa[:, ks], ka[:, ks]
        qc_ref[:, ks] = qh * lax.rsqrt(jnp.sum(qh * qh, axis=-1, keepdims=True) + EPS) * (DN_DK ** -0.5)
        kc_ref[:, ks] = kh * lax.rsqrt(jnp.sum(kh * kh, axis=-1, keepdims=True) + EPS)

    mask = _incl_mask(rev)
    mask_bf = jnp.where(mask, 1.0, 0.0).astype(BF16)
    r = lax.broadcasted_iota(jnp.int32, (CHUNK, CHUNK), 0)
    c = lax.broadcasted_iota(jnp.int32, (CHUNK, CHUNK), 1)
    off_diag = jnp.where(r != c, 1.0, 0.0)
    join_masks = _block_join_masks()
    L = CHUNK

    def prepare(rows):
        pre = ps_ref[0, rows, :]
        g = gp_ref[1:2, :] * _softplus(pre + gp_ref[0:1, :])
        beta = jax.nn.sigmoid(pre)
        gcum = _chunk_cumsum(mask_bf, g)
        g_end = jnp.sum(g, axis=0, keepdims=True)
        gcum_t = gcum.T
        e_cum, e_rest, e_end = jnp.exp(gcum), jnp.exp(g_end - gcum), jnp.exp(g_end)
        a_l, rhs_l, qe_l, kd_l, qk_l, end_l = [], [], [], [], [], []
        for h in range(heads):
            gc, bc = g_col + h, beta_col + h
            ks = slice(h * DN_DK, (h + 1) * DN_DK)
            q, k = qc_ref[rows, ks], kc_ref[rows, ks]
            b_c = beta[:, bc:bc + 1]
            decay = jnp.exp(jnp.where(mask, gcum[:, gc:gc + 1] - gcum_t[gc:gc + 1, :], -jnp.inf))
            kbeta = k * b_c
            both = _dot(jnp.concatenate([kbeta, q], axis=0).astype(BF16), k.astype(BF16), NT)
            a_l.append(both[:L] * decay * off_diag)
            qk_l.append((both[L:] * decay).astype(BF16))
            vbeta = vc_ref[rows, h * DN_DV:(h + 1) * DN_DV] * b_c
            rhs_l.append(jnp.concatenate([vbeta, kbeta * e_cum[:, gc:gc + 1]], axis=1).astype(BF16))
            qe_l.append((q * e_cum[:, gc:gc + 1]).astype(BF16))
            kd_l.append((k * e_rest[:, gc:gc + 1]).astype(BF16))
            end_l.append(e_end[:, gc:gc + 1])
        return a_l, rhs_l, qe_l, kd_l, qk_l, end_l

    def body(gi, carry):
        rows_l = []
        for j in range(DELTA_GROUP):
            ci = gi * DELTA_GROUP + j
            cidx = (nc - 1 - ci) if rev else ci
            rows_l.append(pl.ds(pl.multiple_of(cidx * CHUNK, CHUNK), CHUNK))
        preps = [prepare(rows) for rows in rows_l]
        t_l = _unit_tri_inverses([a for p in preps for a in p[0]], join_masks)
        uw_l = [_dot(t.astype(BF16), rhs) for t, rhs in zip(t_l, [r for p in preps for r in p[1]])]
        for j, (rows, (_, _, qe_l, kd_l, qk_l, end_l)) in enumerate(zip(rows_l, preps)):
            for h in range(heads):
                uw = uw_l[j * heads + h]
                st = st_ref[0, h]
                lhs = jnp.concatenate([uw[:, DN_DV:].astype(BF16), qe_l[h]], axis=0)
                ws = _dot(lhs, st.astype(BF16))
                v_new = (uw[:, :DN_DV] - ws[:L]).astype(BF16)
                o_ref[0, rows, h * DN_DV:(h + 1) * DN_DV] = ws[L:] + _dot(qk_l[h], v_new)
                st_ref[0, h] = end_l[h] * st + _dot(kd_l[h], v_new, TN)
        return carry

    assert nc % DELTA_GROUP == 0
    lax.fori_loop(0, nc // DELTA_GROUP, body, 0)


def delta_scan(pm, ps, gparams, conv_w, s0, rev, seg, q_off, k_off, v_off, heads, g_col, beta_col):
    B, T, _ = pm.shape
    Tb = _time_block(T)
    NB = T // Tb
    assert Tb % seg == 0 or (NB == 1 and seg == T)
    kw, vw = heads * DN_DK, heads * DN_DV
    assert q_off % kw == 0 and k_off % kw == 0 and v_off % vw == 0
    blk = (lambda i: NB - 1 - i) if rev else (lambda i: i)
    st_spec = pl.BlockSpec((1,) + s0.shape[1:], lambda b, i: (b, 0, 0, 0))
    return pl.pallas_call(
        functools.partial(_delta_kernel, rev=rev, heads=heads, nc=Tb // CHUNK, seg=seg, g_col=g_col,
                          beta_col=beta_col),
        grid=(B, NB),
        in_specs=[pl.BlockSpec((1, Tb, kw), lambda b, i: (b, blk(i), q_off // kw)),
                  pl.BlockSpec((1, Tb, kw), lambda b, i: (b, blk(i), k_off // kw)),
                  pl.BlockSpec((1, Tb, vw), lambda b, i: (b, blk(i), v_off // vw)),
                  pl.BlockSpec((1, Tb, ps.shape[-1]), lambda b, i: (b, blk(i), 0)),
                  pl.BlockSpec(gparams.shape, lambda b, i: (0, 0)),
                  pl.BlockSpec(conv_w.shape, lambda b, i: (0, 0)),
                  st_spec],
        out_specs=[pl.BlockSpec((1, Tb, vw), lambda b, i: (b, blk(i), 0)), st_spec],
        out_shape=[jax.ShapeDtypeStruct((B, T, vw), F32), jax.ShapeDtypeStruct(s0.shape, F32)],
        scratch_shapes=[pltpu.VMEM((Tb, kw), F32), pltpu.VMEM((Tb, kw), F32), pltpu.VMEM((Tb, vw), F32)],
        name="delta_scan",
        compiler_params=_params(("parallel", "arbitrary")),
    )(pm, pm, pm, ps, gparams, conv_w, s0)


def delta_bidir(pm_c, ps_c, pm_l, ps_l, conv_w, a_log, dt_bias, q_off, k_off, v_off, col0):
    B = pm_l.shape[0]
    heads = a_log.shape[-1]
    ncol = ps_l.shape[-1]
    gparams = jnp.zeros((8, ncol), F32)
    gparams = gparams.at[0, col0:col0 + 2 * heads].set(dt_bias.reshape(-1))
    gparams = gparams.at[1, col0:col0 + 2 * heads].set(-jnp.exp(a_log).reshape(-1))
    out = []
    for d in (0, 1):
        s0 = jnp.zeros((B, heads, DN_DK, DN_DV), F32)
        args = (q_off, k_off, v_off, heads, col0 + d * heads, col0 + 2 * heads + d * heads)
        oc, s1 = delta_scan(pm_c, ps_c, gparams, conv_w, s0, bool(d), pm_c.shape[1], *args)
        ol, _ = delta_scan(pm_l, ps_l, gparams, conv_w, s1, bool(d), GRID_W, *args)
        out.append((oc, ol))
    return out


def _rglru_kernel(x_ref, cw_ref, cb_ref, wr_ref, br_ref, wi_ref, bi_ref, lam_ref, h0_ref, o_ref, hT_ref,
                  *, rev, nc, seg):
    @pl.when(pl.program_id(2) == 0)
    def _():
        hT_ref[...] = h0_ref[...]

    n = x_ref.shape[1]
    xc = _dwconv(x_ref[0], cw_ref, seg) + cb_ref[...]
    xb = xc.astype(BF16)
    r = jax.nn.sigmoid(_dot(xb, wr_ref[0].astype(BF16)) + br_ref[...])
    i = jax.nn.sigmoid(_dot(xb, wi_ref[0].astype(BF16)) + bi_ref[...])
    log_a = -RG_C * r * _softplus(-lam_ref[...])
    a = jnp.exp(log_a)
    bx = jnp.sqrt(1.0 - jnp.exp(2.0 * log_a)) * (i * xc)

    pos = lax.broadcasted_iota(jnp.int32, (n, 1), 0) % CHUNK
    s = 1
    while s < CHUNK:
        ok = (pos < CHUNK - s) if rev else (pos >= s)
        shift = (n - s) if rev else s
        a_sh = pltpu.roll(a, shift, axis=0)
        b_sh = pltpu.roll(bx, shift, axis=0)
        bx = jnp.where(ok, a * b_sh + bx, bx)
        a = jnp.where(ok, a * a_sh, a)
        s *= 2

    h = hT_ref[0]
    for ci in range(nc):
        c = (nc - 1 - ci) if rev else ci
        rows = slice(c * CHUNK, (c + 1) * CHUNK)
        hc = bx[rows] + a[rows] * h
        o_ref[0, rows, :] = hc
        h = hc[0:1] if rev else hc[CHUNK - 1:CHUNK]
    hT_ref[0] = h


def rglru_scan(pm, conv_w, conv_b, w_r, b_r, w_i, b_i, lam, h0, rev, seg, width):
    B, T, _ = pm.shape
    nblk, bw = w_r.shape[0], w_r.shape[1]
    assert bw % LANES == 0 and nblk * bw == width
    Tb = _tile(T, 1024)
    NB = T // Tb
    assert Tb % seg == 0 or (NB == 1 and seg == T)
    blk = (lambda i: NB - 1 - i) if rev else (lambda i: i)
    row = pl.BlockSpec((1, bw), lambda b, g, i: (0, g))
    return pl.pallas_call(
        functools.partial(_rglru_kernel, rev=rev, nc=Tb // CHUNK, seg=seg),
        grid=(B, nblk, NB),
        in_specs=[pl.BlockSpec((1, Tb, bw), lambda b, g, i: (b, blk(i), g)),
                  pl.BlockSpec((conv_w.shape[0], bw), lambda b, g, i: (0, g)),
                  row,
                  pl.BlockSpec((1, bw, bw), lambda b, g, i: (g, 0, 0)), row,
                  pl.BlockSpec((1, bw, bw), lambda b, g, i: (g, 0, 0)), row,
                  row,
                  pl.BlockSpec((1, 1, bw), lambda b, g, i: (b, 0, g))],
        out_specs=[pl.BlockSpec((1, Tb, bw), lambda b, g, i: (b, blk(i), g)),
                   pl.BlockSpec((1, 1, bw), lambda b, g, i: (b, 0, g))],
        out_shape=[jax.ShapeDtypeStruct((B, T, width), F32), jax.ShapeDtypeStruct(h0.shape, F32)],
        name="rglru_scan",
        compiler_params=_params(("parallel", "parallel", "arbitrary")),
    )(pm, conv_w, conv_b, w_r, b_r, w_i, b_i, lam, h0)


def rglru_bidir(pm_c, pm_l, conv_w, conv_b, w_r, b_r, w_i, b_i, lam):
    B = pm_l.shape[0]
    width = conv_b.shape[-1]
    out = []
    for d in (0, 1):
        h0 = jnp.zeros((B, 1, width), F32)
        args = (conv_w, conv_b.reshape(1, width), w_r[d], b_r[d].reshape(1, width), w_i[d],
                b_i[d].reshape(1, width), lam[d].reshape(1, width))
        oc, h1 = rglru_scan(pm_c, *args, h0, bool(d), pm_c.shape[1], width)
        ol, _ = rglru_scan(pm_l, *args, h1, bool(d), GRID_W, width)
        out.append((oc, ol))
    return out


def _pad_cols(w, mult):
    return jnp.pad(w, ((0, 0), (0, (-w.shape[-1]) % mult)))


def kernel(x, c, ctx, c_ctx, mod_w, mod_b, norm_mix_g, norm_ffn_g, ffn_w_gu, ffn_w_down, final_norm_g,
           ab_w_in, ab_w_out, ml_ig_b, ml_fg_b, ml_norm_g, dn_conv_w, dn_a_log, dn_dt_bias, dn_norm_g,
           cd_w_in, cd_w_out, rg_conv_w, rg_conv_b, rg_w_r, rg_b_r, rg_w_i, rg_b_i, rg_lambda,
           gla_w_up, gla_b_up, gla_norm_g):
    B, T, D = x.shape
    depth = mod_w.shape[0]
    mlh, dnh = ml_ig_b.shape[-1], dn_a_log.shape[-1]
    rgw = rg_conv_b.shape[-1]
    glh = gla_w_up.shape[-1] // GLA_DK

    cond = jnp.concatenate([c, c_ctx[None, :], jnp.zeros((8 - B - 1, D), F32)], axis=0)

    m_all = adaln_all(cond, mod_w, mod_b)

    xc, xl = ctx, x
    for layer in range(depth):
        last = layer == depth - 1
        j = layer // 2
        m = m_all[layer]
        lat = [t[:, None, :] for t in jnp.split(m[:B], 6, axis=-1)]
        cx = [t[:, None, :] for t in jnp.split(m[B:B + 1], 6, axis=-1)]

        if layer % 2 == 0:
            w_in = ab_w_in[j]
            s = np.cumsum((0, mlh * ML_DK, mlh * ML_DK, mlh * ML_DV, mlh * ML_DV, 2 * mlh, 2 * mlh,
                           dnh * DN_DK, dnh * DN_DK, dnh * DN_DV, dnh * DN_DV, 2 * dnh, 2 * dnh))
            w_main = jnp.concatenate([w_in[:, s[0]:s[4]], w_in[:, s[6]:s[10]]], axis=1)
            w_small = jnp.concatenate([w_in[:, s[4]:s[6]], w_in[:, s[10]:s[12]]], axis=1)
            w_out = ab_w_out[j]
            o = np.cumsum((0, mlh * ML_DK, mlh * ML_DK, mlh * ML_DV, mlh * ML_DV,
                           dnh * DN_DK, dnh * DN_DK, dnh * DN_DV))
        else:
            w_in = cd_w_in[j]
            o = np.cumsum((0, rgw, rgw, glh * GLA_DK, glh * GLA_DK, glh * GLA_DV))
            nmain = int(o[5]) + glh * GLA_DV
            w_main, w_small = w_in[:, :nmain], w_in[:, nmain:]
            w_out = cd_w_out[j]
        o = [int(v) for v in o]
        w_main = w_main.astype(BF16)
        w_small = _pad_cols(w_small, LANES).astype(BF16)
        w_out = w_out.astype(BF16)

        pm_l, ps_l = inproj(xl, norm_mix_g[layer], lat[1], lat[0], w_main, w_small)
        pm_c, ps_c = inproj(xc, norm_mix_g[layer], cx[1], cx[0], w_main, w_small)

        if layer % 2 == 0:
            mix_a = mlstm_bidir(pm_c, ps_c, pm_l, ps_l, ml_ig_b[j], ml_fg_b[j], o[0], o[1], o[2])
            mix_b = delta_bidir(pm_c, ps_c, pm_l, ps_l, dn_conv_w[j], dn_a_log[j], dn_dt_bias[j],
                                o[4], o[5], o[6], 4 * mlh)
            margs = (0, o[3], o[7], ml_norm_g[j], dn_norm_g[j], ML_DV, DN_DV)
        else:
            mix_a = rglru_bidir(pm_c, pm_l, rg_conv_w[j], rg_conv_b[j], rg_w_r[j], rg_b_r[j], rg_w_i[j],
                                rg_b_i[j], rg_lambda[j])
            mix_b = gla_bidir(pm_c, ps_c, pm_l, ps_l, gla_w_up[j], gla_b_up[j], o[2], o[3], o[4])
            margs = (1, o[1], o[5], jnp.ones((rgw,), F32), gla_norm_g[j], rgw, GLA_DV)

        def merged(which, pm, xres, gate):
            kind, ga_off, gb_off, na, nb, da, db = margs
            a_dirs = (mix_a[0][which], mix_a[1][which])
            b_dirs = (mix_b[0][which], mix_b[1][which])
            return merge_outproj(kind, a_dirs, b_dirs, pm, ga_off, gb_off, na, nb, da, db, xres, gate, w_out)

        w_gu = ffn_w_gu[layer].astype(BF16)
        w_dn = ffn_w_down[layer].astype(BF16)
        xl = merged(1, pm_l, xl, lat[2])
        xl = ffn(xl, norm_ffn_g[layer], lat[4], lat[3], lat[5], w_gu, w_dn, final_norm_g if last else None)
        if not last:
            xc = merged(0, pm_c, xc, cx[2])
            xc = ffn(xc, norm_ffn_g[layer], cx[4], cx[3], cx[5], w_gu, w_dn)
    return xl
```

```python
import functools

import jax
import jax.numpy as jnp
import numpy as np
from jax import lax
from jax.experimental import pallas as pl
from jax.experimental.pallas import tpu as pltpu

F32 = jnp.float32
BF16 = jnp.bfloat16

GRID_W = 64
CHUNK = 64
EPS = 1e-6
ML_DK, ML_DV = 128, 256
DN_DK, DN_DV = 128, 128
GLA_DK, GLA_DV = 128, 256
RG_C = 8.0
GLA_TAU = 16.0

LANES = 128
DELTA_GROUP = 4
GLA_GROUP = 4
FFN_SUB = 256
VMEM_LIMIT = 48 * 1024 * 1024

NT = (((1,), (1,)), ((), ()))
TN = (((0,), (0,)), ((), ()))


def _tile(n, pref):
    t = min(n, pref)
    while n % t:
        t //= 2
    return t


def _params(sem, vmem=VMEM_LIMIT):
    return pltpu.CompilerParams(dimension_semantics=sem, vmem_limit_bytes=vmem)


def _dot(a, b, dims=None):
    if dims is None:
        return jnp.dot(a, b, preferred_element_type=F32)
    return lax.dot_general(a, b, dims, preferred_element_type=F32)


def _modulated_norm(x, g, sc, sh):
    y = x * lax.rsqrt(jnp.mean(x * x, axis=-1, keepdims=True) + EPS) * g
    return y * (1.0 + sc) + sh


def _adaln_kernel(s_ref, w_ref, b_ref, o_ref):
    s = s_ref[...]
    s = (s * jax.nn.sigmoid(s)).astype(BF16)
    o_ref[0] = _dot(s, w_ref[0].astype(BF16)) + b_ref[0]


def adaln_all(cond, w_mod, b_mod):
    R, D = cond.shape
    depth, _, C = w_mod.shape
    tn = _tile(C, 1024)
    return pl.pallas_call(
        _adaln_kernel,
        grid=(depth, C // tn),
        in_specs=[pl.BlockSpec((R, D), lambda l, j: (0, 0)),
                  pl.BlockSpec((1, D, tn), lambda l, j: (l, 0, j)),
                  pl.BlockSpec((1, 1, tn), lambda l, j: (l, 0, j))],
        out_specs=pl.BlockSpec((1, R, tn), lambda l, j: (l, 0, j)),
        out_shape=jax.ShapeDtypeStruct((depth, R, C), F32),
        name="adaln_all",
        compiler_params=_params(("parallel", "parallel")),
    )(cond, w_mod, b_mod.reshape(depth, 1, C))


def _inproj_kernel(x_ref, g_ref, sc_ref, sh_ref, w_ref, ws_ref, cw_ref, o_ref, os_ref, h_ref,
                   *, conv_tiles, norm_tiles, seg, norm_group):
    j = pl.program_id(2)

    @pl.when(j == 0)
    def _():
        h = _modulated_norm(x_ref[0], g_ref[...], sc_ref[0], sh_ref[0]).astype(BF16)
        h_ref[...] = h
        os_ref[0] = _dot(h, ws_ref[...])

    o_ref[0] = _dot(h_ref[...], w_ref[...])

    @pl.when((j >= conv_tiles[0]) & (j < conv_tiles[1]))
    def _():
        o_ref[0] = jax.nn.silu(_dwconv(o_ref[0], cw_ref, seg))

    @pl.when((j >= norm_tiles[0]) & (j < norm_tiles[1]))
    def _():
        for lo in range(0, o_ref.shape[-1], norm_group):
            y = o_ref[0, :, lo:lo + norm_group]
            o_ref[0, :, lo:lo + norm_group] = y * lax.rsqrt(jnp.sum(y * y, axis=-1, keepdims=True) + EPS)


def inproj(x, g, sc, sh, w_main, w_small, conv_w=None, conv_off=0, norm_cols=0, seg=1, norm_group=LANES):
    B, T, D = x.shape
    C = w_main.shape[1]
    Cs = w_small.shape[1]
    tm, tn = _tile(T, 1024), _tile(C, 1024)
    if conv_w is None:
        conv_w, conv_tiles, norm_tiles = jnp.zeros((1, tn), F32), (0, 0), (0, 0)
    else:
        assert conv_off % tn == 0 and conv_w.shape[1] % tn == 0 and norm_cols % tn == 0 and tn % norm_group == 0
        assert tm % seg == 0 or (tm == T and seg == T)
        conv_tiles = (conv_off // tn, (conv_off + conv_w.shape[1]) // tn)
        norm_tiles = (conv_off // tn, (conv_off + norm_cols) // tn)
    n_conv = conv_w.shape[1] // tn
    mod = (lambda b, i, j: (b, 0, 0)) if sc.shape[0] == B else (lambda b, i, j: (0, 0, 0))
    return pl.pallas_call(
        functools.partial(_inproj_kernel, conv_tiles=conv_tiles, norm_tiles=norm_tiles, seg=seg,
                          norm_group=norm_group),
        grid=(B, T // tm, C // tn),
        in_specs=[pl.BlockSpec((1, tm, D), lambda b, i, j: (b, i, 0)),
                  pl.BlockSpec((1, D), lambda b, i, j: (0, 0)),
                  pl.BlockSpec((1, 1, D), mod),
                  pl.BlockSpec((1, 1, D), mod),
                  pl.BlockSpec((D, tn), lambda b, i, j: (0, j)),
                  pl.BlockSpec((D, Cs), lambda b, i, j: (0, 0)),
                  pl.BlockSpec((conv_w.shape[0], tn),
                               lambda b, i, j: (0, jnp.clip(j - conv_tiles[0], 0, n_conv - 1)))],
        out_specs=[pl.BlockSpec((1, tm, tn), lambda b, i, j: (b, i, j)),
                   pl.BlockSpec((1, tm, Cs), lambda b, i, j: (b, i, 0))],
        out_shape=[jax.ShapeDtypeStruct((B, T, C), F32), jax.ShapeDtypeStruct((B, T, Cs), F32)],
        scratch_shapes=[pltpu.VMEM((tm, D), BF16)],
        name="inproj",
        compiler_params=_params(("parallel", "parallel", "arbitrary")),
    )(x, g.reshape(1, D), sc, sh, w_main, w_small, conv_w)


def _head_norm(o, d, g):
    parts = []
    for h in range(o.shape[-1] // d):
        oh = o[:, h * d:(h + 1) * d]
        parts.append(oh * lax.rsqrt(jnp.mean(oh * oh, axis=-1, keepdims=True) + EPS))
    return jnp.concatenate(parts, axis=-1) * g


def _merge_kernel(af_ref, ab_ref, bf_ref, bb_ref, ga_ref, gb_ref, na_ref, nb_ref, x_ref, gate_ref, w_ref, o_ref,
                  *, kind, da, db):
    a = af_ref[0] + ab_ref[0]
    b = bf_ref[0] + bb_ref[0]
    if kind == 0:
        ya = _head_norm(a, da, na_ref[...]) * jax.nn.sigmoid(ga_ref[0])
    else:
        ya = jax.nn.gelu(ga_ref[0]) * a
    gb = gb_ref[0]
    yb = _head_norm(b, db, nb_ref[...]) * (gb * jax.nn.sigmoid(gb))
    wa = a.shape[-1]
    y = _dot(ya.astype(BF16), w_ref[0:wa, :]) + _dot(yb.astype(BF16), w_ref[wa:, :])
    o_ref[0] = x_ref[0] + gate_ref[0] * y


def merge_outproj(kind, a_dirs, b_dirs, pm, ga_off, gb_off, na, nb, da, db, x, gate, w):
    B, T, D = x.shape
    wa, wb = a_dirs[0].shape[-1], b_dirs[0].shape[-1]
    assert ga_off % wa == 0 and gb_off % wb == 0
    tm = _tile(T, 256)
    mod = (lambda b, i: (b, 0, 0)) if gate.shape[0] == B else (lambda b, i: (0, 0, 0))
    tok = lambda width, col: pl.BlockSpec((1, tm, width), lambda b, i: (b, i, col))
    return pl.pallas_call(
        functools.partial(_merge_kernel, kind=kind, da=da, db=db),
        grid=(B, T // tm),
        in_specs=[tok(wa, 0), tok(wa, 0), tok(wb, 0), tok(wb, 0), tok(wa, ga_off // wa), tok(wb, gb_off // wb),
                  pl.BlockSpec((1, wa), lambda b, i: (0, 0)),
                  pl.BlockSpec((1, wb), lambda b, i: (0, 0)),
                  tok(D, 0),
                  pl.BlockSpec((1, 1, D), mod),
                  pl.BlockSpec((wa + wb, D), lambda b, i: (0, 0))],
        out_specs=tok(D, 0),
        out_shape=jax.ShapeDtypeStruct((B, T, D), F32),
        name="merge_outproj",
        compiler_params=_params(("parallel", "parallel")),
    )(a_dirs[0], a_dirs[1], b_dirs[0], b_dirs[1], pm, pm, na.reshape(1, wa), nb.reshape(1, wb), x, gate, w)


def _ffn_kernel(x_ref, g_ref, sc_ref, sh_ref, gate_ref, wg_ref, wu_ref, wd_ref, fg_ref, o_ref, h_ref,
                *, final_norm):
    k = pl.program_id(2)

    @pl.when(k == 0)
    def _():
        h_ref[...] = _modulated_norm(x_ref[0], g_ref[...], sc_ref[0], sh_ref[0]).astype(BF16)
        o_ref[...] = jnp.zeros_like(o_ref)

    h = h_ref[...]
    acts = []
    for lo in range(0, wg_ref.shape[1], FFN_SUB):
        a = _dot(h, wg_ref[:, lo:lo + FFN_SUB])
        u = _dot(h, wu_ref[:, lo:lo + FFN_SUB])
        acts.append((a * jax.nn.sigmoid(a) * u).astype(BF16))
    o_ref[0] += _dot(jnp.concatenate(acts, axis=1), wd_ref[...])

    @pl.when(k == pl.num_programs(2) - 1)
    def _():
        r = x_ref[0] + gate_ref[0] * o_ref[0]
        if final_norm:
            r = r * lax.rsqrt(jnp.mean(r * r, axis=-1, keepdims=True) + EPS) * fg_ref[...]
        o_ref[0] = r


def ffn(x, g, sc, sh, gate, w_gu, w_down, final_g=None):
    B, T, D = x.shape
    Fh = w_down.shape[0]
    tm, tf = _tile(T, 512), _tile(Fh, 512)
    nf = Fh // tf
    mod = (lambda b, i, k: (b, 0, 0)) if sc.shape[0] == B else (lambda b, i, k: (0, 0, 0))
    fg = jnp.ones((1, D), F32) if final_g is None else final_g.reshape(1, D)
    return pl.pallas_call(
        functools.partial(_ffn_kernel, final_norm=final_g is not None),
        grid=(B, T // tm, nf),
        in_specs=[pl.BlockSpec((1, tm, D), lambda b, i, k: (b, i, 0)),
                  pl.BlockSpec((1, D), lambda b, i, k: (0, 0)),
                  pl.BlockSpec((1, 1, D), mod),
                  pl.BlockSpec((1, 1, D), mod),
                  pl.BlockSpec((1, 1, D), mod),
                  pl.BlockSpec((D, tf), lambda b, i, k: (0, k)),
                  pl.BlockSpec((D, tf), lambda b, i, k: (0, k + nf)),
                  pl.BlockSpec((tf, D), lambda b, i, k: (k, 0)),
                  pl.BlockSpec((1, D), lambda b, i, k: (0, 0))],
        out_specs=pl.BlockSpec((1, tm, D), lambda b, i, k: (b, i, 0)),
        out_shape=jax.ShapeDtypeStruct((B, T, D), F32),
        scratch_shapes=[pltpu.VMEM((tm, D), BF16)],
        name="ffn",
        compiler_params=_params(("parallel", "parallel", "arbitrary")),
    )(x, g.reshape(1, D), sc, sh, gate, w_gu, w_gu, w_down, fg)


def _incl_mask(rev):
    r = lax.broadcasted_iota(jnp.int32, (CHUNK, CHUNK), 0)
    c = lax.broadcasted_iota(jnp.int32, (CHUNK, CHUNK), 1)
    return (r <= c) if rev else (r >= c)


def _split_bf16(x):
    hi = x.astype(BF16)
    return hi, (x - hi.astype(F32)).astype(BF16)


def _chunk_cumsum(mask_bf16, x):
    hi, lo = _split_bf16(x)
    return _dot(mask_bf16, hi) + _dot(mask_bf16, lo)


def _log_sigmoid(z):
    return jnp.minimum(z, 0.0) - jnp.log1p(jnp.exp(-jnp.abs(z)))


def _softplus(x):
    return jnp.maximum(x, 0.0) + jnp.log1p(jnp.exp(-jnp.abs(x)))


def _time_block(T):
    return _tile(T, 512)


def _dwconv(x, w_ref, seg):
    n = x.shape[0]
    taps = w_ref.shape[0]
    pos = lax.broadcasted_iota(jnp.int32, (n, 1), 0) % seg
    out = None
    for j in range(taps):
        sh = j - taps // 2
        xs = x if sh == 0 else pltpu.roll(x, (-sh) % n, axis=0)
        ok = (pos + sh >= 0) & (pos + sh < seg)
        term = jnp.where(ok, xs, 0.0) * w_ref[j:j + 1, :]
        out = term if out is None else out + term
    return out


def _gla_kernel(q_ref, k_ref, v_ref, ps_ref, wup_ref, bup_ref, s0_ref, o_ref, st_ref, lg_ref, *, rev, heads, nc):
    @pl.when(pl.program_id(1) == 0)
    def _():
        st_ref[...] = s0_ref[...]

    mask = _incl_mask(rev)
    mask_bf = jnp.where(mask, 1.0, 0.0).astype(BF16)
    z = _dot(ps_ref[0].astype(BF16), wup_ref[...]) + bup_ref[...]
    lg_ref[...] = _log_sigmoid(z) * (1.0 / GLA_TAU)

    hs = range(heads)
    ks = [slice(h * GLA_DK, (h + 1) * GLA_DK) for h in hs]
    vs = [slice(h * GLA_DV, (h + 1) * GLA_DV) for h in hs]

    def prepare(rows):
        lg = lg_ref[rows, :]
        b = _chunk_cumsum(mask_bf, lg)
        b_end = jnp.sum(lg, axis=0, keepdims=True)
        k = k_ref[0, rows, :]
        qe = (q_ref[0, rows, :] * (GLA_DK ** -0.5) * jnp.exp(b)).astype(BF16)
        ke = (k * jnp.exp(-b)).astype(BF16)
        kd = (k * jnp.exp(b_end - b)).astype(BF16)
        v = [v_ref[0, rows, vs[h]].astype(BF16) for h in hs]
        att = [_dot(qe[:, ks[h]], ke[:, ks[h]], NT) for h in hs]
        upd = [_dot(v[h], kd[:, ks[h]], TN) for h in hs]
        intra = [_dot(jnp.where(mask, att[h], 0.0).astype(BF16), v[h]) for h in hs]
        return qe, intra, upd, jnp.exp(b_end)

    def body(gi, carry):
        rows_l = []
        for j in range(GLA_GROUP):
            ci = gi * GLA_GROUP + j
            c = (nc - 1 - ci) if rev else ci
            rows_l.append(pl.ds(pl.multiple_of(c * CHUNK, CHUNK), CHUNK))
        preps = [prepare(rows) for rows in rows_l]
        for rows, (qe, intra, upd, e_end) in zip(rows_l, preps):
            for h in hs:
                st = st_ref[0, h]
                o_ref[0, rows, vs[h]] = intra[h] + _dot(qe[:, ks[h]], st.astype(BF16), NT)
                st_ref[0, h] = st * e_end[:, ks[h]] + upd[h]
        return carry

    assert nc % GLA_GROUP == 0
    lax.fori_loop(0, nc // GLA_GROUP, body, 0)


def gla_scan(pm, ps, wup, bup, s0, rev, q_off, k_off, v_off, heads):
    B, T, _ = pm.shape
    Tb = _time_block(T)
    NB = T // Tb
    kw, vw = heads * GLA_DK, heads * GLA_DV
    assert q_off % kw == 0 and k_off % kw == 0 and v_off % vw == 0
    blk = (lambda i: NB - 1 - i) if rev else (lambda i: i)
    return pl.pallas_call(
        functools.partial(_gla_kernel, rev=rev, heads=heads, nc=Tb // CHUNK),
        grid=(B, NB),
        in_specs=[pl.BlockSpec((1, Tb, kw), lambda b, i: (b, blk(i), q_off // kw)),
                  pl.BlockSpec((1, Tb, kw), lambda b, i: (b, blk(i), k_off // kw)),
                  pl.BlockSpec((1, Tb, vw), lambda b, i: (b, blk(i), v_off // vw)),
                  pl.BlockSpec((1, Tb, ps.shape[-1]), lambda b, i: (b, blk(i), 0)),
                  pl.BlockSpec(wup.shape, lambda b, i: (0, 0)),
                  pl.BlockSpec(bup.shape, lambda b, i: (0, 0)),
                  pl.BlockSpec((1,) + s0.shape[1:], lambda b, i: (b, 0, 0, 0))],
        out_specs=[pl.BlockSpec((1, Tb, vw), lambda b, i: (b, blk(i), 0)),
                   pl.BlockSpec((1,) + s0.shape[1:], lambda b, i: (b, 0, 0, 0))],
        out_shape=[jax.ShapeDtypeStruct((B, T, vw), F32), jax.ShapeDtypeStruct(s0.shape, F32)],
        scratch_shapes=[pltpu.VMEM((Tb, kw), F32)],
        name="gla_scan",
        compiler_params=_params(("parallel", "arbitrary")),
    )(pm, pm, pm, ps, wup, bup, s0)


def gla_bidir(pm_c, ps_c, pm_l, ps_l, w_up, b_up, q_off, k_off, v_off):
    B = pm_l.shape[0]
    rank, hk = w_up.shape[1], w_up.shape[2]
    heads = hk // GLA_DK
    out = []
    for d in (0, 1):
        wup = jnp.zeros((ps_l.shape[-1], hk), F32).at[d * rank:(d + 1) * rank].set(w_up[d]).astype(BF16)
        bup = b_up[d].reshape(1, hk)
        s0 = jnp.zeros((B, heads, GLA_DV, GLA_DK), F32)
        oc, s1 = gla_scan(pm_c, ps_c, wup, bup, s0, bool(d), q_off, k_off, v_off, heads)
        ol, _ = gla_scan(pm_l, ps_l, wup, bup, s1, bool(d), q_off, k_off, v_off, heads)
        out.append((oc, ol))
    return out


def _mlstm_kernel(q_ref, k_ref, v_ref, ps_ref, gb_ref, c0_ref, nm0_ref, o_ref, c_ref, nm_ref,
                  *, rev, heads, nc, ig_col, fg_col):
    @pl.when(pl.program_id(1) == 0)
    def _():
        c_ref[...] = c0_ref[...]
        nm_ref[...] = nm0_ref[...]

    mask = _incl_mask(rev)
    mask_bf = jnp.where(mask, 1.0, 0.0).astype(BF16)

    def body(ci, carry):
        c = (nc - 1 - ci) if rev else ci
        rows = pl.ds(pl.multiple_of(c * CHUNK, CHUNK), CHUNK)
        g = ps_ref[0, rows, :] + gb_ref[...]
        lf = _log_sigmoid(g)
        bcum = _chunk_cumsum(mask_bf, lf)
        b_end = jnp.sum(lf, axis=0, keepdims=True)
        bcum_t, g_t = bcum.T, g.T
        hs = range(heads)
        cols = [(ig_col + h, fg_col + h) for h in hs]
        b_col = [bcum[:, fc:fc + 1] for _, fc in cols]
        b_l = [b_end[:, fc:fc + 1] for _, fc in cols]
        q = [q_ref[0, rows, h * ML_DK:(h + 1) * ML_DK] * (ML_DK ** -0.5) for h in hs]
        k = [k_ref[0, rows, h * ML_DK:(h + 1) * ML_DK] for h in hs]
        v = [v_ref[0, rows, h * ML_DV:(h + 1) * ML_DV] for h in hs]
        qb = [t.astype(BF16) for t in q]
        kb = [t.astype(BF16) for t in k]
        cst = [c_ref[0, h] for h in hs]
        n = [nm_ref[0, h, 0:1, :] for h in hs]
        m = [nm_ref[0, h, 1:2, 0:1] for h in hs]
        qk = [_dot(qb[h], kb[h], NT) for h in hs]
        q_c = [_dot(qb[h], cst[h].astype(BF16), NT) for h in hs]
        q_n = [jnp.sum(q[h] * n[h], axis=-1, keepdims=True) for h in hs]
        d_log = [jnp.where(mask, b_col[h] - bcum_t[fc:fc + 1, :] + g_t[ic:ic + 1, :], -jnp.inf)
                 for h, (ic, fc) in enumerate(cols)]
        inter_log = [b_col[h] + m[h] for h in hs]
        m_out = [jnp.maximum(jnp.max(d_log[h], axis=-1, keepdims=True), inter_log[h]) for h in hs]
        s = [qk[h] * jnp.exp(d_log[h] - m_out[h]) for h in hs]
        e_inter = [jnp.exp(inter_log[h] - m_out[h]) for h in hs]
        num = [_dot(s[h].astype(BF16), v[h].astype(BF16)) + e_inter[h] * q_c[h] for h in hs]
        den = [jnp.sum(s[h], axis=-1, keepdims=True) + e_inter[h] * q_n[h] for h in hs]
        for h in hs:
            o_ref[0, rows, h * ML_DV:(h + 1) * ML_DV] = num[h] / jnp.maximum(jnp.abs(den[h]), jnp.exp(-m_out[h]))
        w_log = [b_l[h] - b_col[h] + g[:, ic:ic + 1] for h, (ic, _) in enumerate(cols)]
        m_loc = [jnp.max(t, axis=0, keepdims=True) for t in w_log]
        w = [jnp.exp(w_log[h] - m_loc[h]) for h in hs]
        m_new = [jnp.maximum(b_l[h] + m[h], m_loc[h]) for h in hs]
        a = [jnp.exp(b_l[h] + m[h] - m_new[h]) for h in hs]
        sc = [jnp.exp(m_loc[h] - m_new[h]) for h in hs]
        upd = [_dot((w[h] * v[h]).astype(BF16), kb[h], TN) for h in hs]
        for h in hs:
            c_ref[0, h] = a[h] * cst[h] + sc[h] * upd[h]
            nm_ref[0, h, 0:1, :] = a[h] * n[h] + sc[h] * jnp.sum(w[h] * k[h], axis=0, keepdims=True)
            nm_ref[0, h, 1:2, :] = jnp.broadcast_to(m_new[h], (1, ML_DK))
        return carry

    lax.fori_loop(0, nc, body, 0)


def mlstm_scan(pm, ps, gbias, state, rev, q_off, k_off, v_off, heads, ig_col, fg_col):
    B, T, _ = pm.shape
    Tb = _time_block(T)
    NB = T // Tb
    kw, vw = heads * ML_DK, heads * ML_DV
    assert q_off % kw == 0 and k_off % kw == 0 and v_off % vw == 0
    c0, nm0 = state
    blk = (lambda i: NB - 1 - i) if rev else (lambda i: i)
    st_spec = lambda a: pl.BlockSpec((1,) + a.shape[1:], lambda b, i: (b, 0, 0, 0))
    o, c1, nm1 = pl.pallas_call(
        functools.partial(_mlstm_kernel, rev=rev, heads=heads, nc=Tb // CHUNK, ig_col=ig_col, fg_col=fg_col),
        grid=(B, NB),
        in_specs=[pl.BlockSpec((1, Tb, kw), lambda b, i: (b, blk(i), q_off // kw)),
                  pl.BlockSpec((1, Tb, kw), lambda b, i: (b, blk(i), k_off // kw)),
                  pl.BlockSpec((1, Tb, vw), lambda b, i: (b, blk(i), v_off // vw)),
                  pl.BlockSpec((1, Tb, ps.shape[-1]), lambda b, i: (b, blk(i), 0)),
                  pl.BlockSpec(gbias.shape, lambda b, i: (0, 0)),
                  st_spec(c0), st_spec(nm0)],
        out_specs=[pl.BlockSpec((1, Tb, vw), lambda b, i: (b, blk(i), 0)), st_spec(c0), st_spec(nm0)],
        out_shape=[jax.ShapeDtypeStruct((B, T, vw), F32), jax.ShapeDtypeStruct(c0.shape, F32),
                   jax.ShapeDtypeStruct(nm0.shape, F32)],
        name="mlstm_scan",
        compiler_params=_params(("parallel", "arbitrary")),
    )(pm, pm, pm, ps, gbias, c0, nm0)
    return o, (c1, nm1)


def mlstm_bidir(pm_c, ps_c, pm_l, ps_l, ig_b, fg_b, q_off, k_off, v_off):
    B = pm_l.shape[0]
    heads = ig_b.shape[-1]
    ncol = ps_l.shape[-1]
    gbias = jnp.zeros((1, ncol), F32).at[0, :4 * heads].set(jnp.concatenate([ig_b.reshape(-1), fg_b.reshape(-1)]))
    out = []
    for d in (0, 1):
        st = (jnp.zeros((B, heads, ML_DV, ML_DK), F32), jnp.zeros((B, heads, 8, ML_DK), F32))
        args = (bool(d), q_off, k_off, v_off, heads, d * heads, 2 * heads + d * heads)
        oc, st = mlstm_scan(pm_c, ps_c, gbias, st, *args)
        ol, _ = mlstm_scan(pm_l, ps_l, gbias, st, *args)
        out.append((oc, ol))
    return out


def _block_join_masks():
    r = lax.broadcasted_iota(jnp.int32, (CHUNK, CHUNK), 0)
    c = lax.broadcasted_iota(jnp.int32, (CHUNK, CHUNK), 1)
    masks, s = [], 1
    while s < CHUNK:
        masks.append(jnp.where((r // (2 * s) == c // (2 * s)) & (r // s != c // s), 1.0, 0.0))
        s *= 2
    return masks


def _unit_tri_inverses(a_list, join_masks):
    r = lax.broadcasted_iota(jnp.int32, (CHUNK, CHUNK), 0)
    c = lax.broadcasted_iota(jnp.int32, (CHUNK, CHUNK), 1)
    eye = jnp.where(r == c, 1.0, 0.0)
    ts = [eye - a * join_masks[0] for a in a_list]
    for m in join_masks[1:]:
        tb = [t.astype(BF16) for t in ts]
        xs = [_dot(t, (a * m).astype(BF16)) for t, a in zip(tb, a_list)]
        ts = [t - _dot(x.astype(BF16), t_b) for t, x, t_b in zip(ts, xs, tb)]
    return ts


def _delta_kernel(q_ref, k_ref, v_ref, ps_ref, gp_ref, s0_ref, o_ref, st_ref, *, rev, heads, nc, g_col, beta_col):
    @pl.when(pl.program_id(1) == 0)
    def _():
        st_ref[...] = s0_ref[...]

    mask = _incl_mask(rev)
    mask_bf = jnp.where(mask, 1.0, 0.0).astype(BF16)
    r = lax.broadcasted_iota(jnp.int32, (CHUNK, CHUNK), 0)
    c = lax.broadcasted_iota(jnp.int32, (CHUNK, CHUNK), 1)
    off_diag = jnp.where(r != c, 1.0, 0.0)
    join_masks = _block_join_masks()
    L = CHUNK

    def prepare(rows):
        pre = ps_ref[0, rows, :]
        g = gp_ref[1:2, :] * _softplus(pre + gp_ref[0:1, :])
        beta = jax.nn.sigmoid(pre)
        gcum = _chunk_cumsum(mask_bf, g)
        g_end = jnp.sum(g, axis=0, keepdims=True)
        gcum_t = gcum.T
        e_cum, e_rest, e_end = jnp.exp(gcum), jnp.exp(g_end - gcum), jnp.exp(g_end)
        a_l, rhs_l, qe_l, kd_l, qk_l, end_l = [], [], [], [], [], []
        for h in range(heads):
            gc, bc = g_col + h, beta_col + h
            ks = slice(h * DN_DK, (h + 1) * DN_DK)
            q, k = q_ref[0, rows, ks] * (DN_DK ** -0.5), k_ref[0, rows, ks]
            b_c = beta[:, bc:bc + 1]
            decay = jnp.exp(jnp.where(mask, gcum[:, gc:gc + 1] - gcum_t[gc:gc + 1, :], -jnp.inf))
            kbeta = k * b_c
            both = _dot(jnp.concatenate([kbeta, q], axis=0).astype(BF16), k.astype(BF16), NT)
            a_l.append(both[:L] * decay * off_diag)
            qk_l.append((both[L:] * decay).astype(BF16))
            vbeta = v_ref[0, rows, h * DN_DV:(h + 1) * DN_DV] * b_c
            rhs_l.append(jnp.concatenate([vbeta, kbeta * e_cum[:, gc:gc + 1]], axis=1).astype(BF16))
            qe_l.append((q * e_cum[:, gc:gc + 1]).astype(BF16))
            kd_l.append((k * e_rest[:, gc:gc + 1]).astype(BF16))
            end_l.append(e_end[:, gc:gc + 1])
        return a_l, rhs_l, qe_l, kd_l, qk_l, end_l

    def body(gi, carry):
        rows_l = []
        for j in range(DELTA_GROUP):
            ci = gi * DELTA_GROUP + j
            cidx = (nc - 1 - ci) if rev else ci
            rows_l.append(pl.ds(pl.multiple_of(cidx * CHUNK, CHUNK), CHUNK))
        preps = [prepare(rows) for rows in rows_l]
        t_l = _unit_tri_inverses([a for p in preps for a in p[0]], join_masks)
        uw_l = [_dot(t.astype(BF16), rhs) for t, rhs in zip(t_l, [r for p in preps for r in p[1]])]
        for j, (rows, (_, _, qe_l, kd_l, qk_l, end_l)) in enumerate(zip(rows_l, preps)):
            for h in range(heads):
                uw = uw_l[j * heads + h]
                st = st_ref[0, h]
                lhs = jnp.concatenate([uw[:, DN_DV:].astype(BF16), qe_l[h]], axis=0)
                ws = _dot(lhs, st.astype(BF16))
                v_new = (uw[:, :DN_DV] - ws[:L]).astype(BF16)
                o_ref[0, rows, h * DN_DV:(h + 1) * DN_DV] = ws[L:] + _dot(qk_l[h], v_new)
                st_ref[0, h] = end_l[h] * st + _dot(kd_l[h], v_new, TN)
        return carry

    assert nc % DELTA_GROUP == 0
    lax.fori_loop(0, nc // DELTA_GROUP, body, 0)


def delta_scan(pm, ps, gparams, s0, rev, q_off, k_off, v_off, heads, g_col, beta_col):
    B, T, _ = pm.shape
    Tb = _time_block(T)
    NB = T // Tb
    kw, vw = heads * DN_DK, heads * DN_DV
    assert q_off % kw == 0 and k_off % kw == 0 and v_off % vw == 0
    blk = (lambda i: NB - 1 - i) if rev else (lambda i: i)
    st_spec = pl.BlockSpec((1,) + s0.shape[1:], lambda b, i: (b, 0, 0, 0))
    return pl.pallas_call(
        functools.partial(_delta_kernel, rev=rev, heads=heads, nc=Tb // CHUNK, g_col=g_col, beta_col=beta_col),
        grid=(B, NB),
        in_specs=[pl.BlockSpec((1, Tb, kw), lambda b, i: (b, blk(i), q_off // kw)),
                  pl.BlockSpec((1, Tb, kw), lambda b, i: (b, blk(i), k_off // kw)),
                  pl.BlockSpec((1, Tb, vw), lambda b, i: (b, blk(i), v_off // vw)),
                  pl.BlockSpec((1, Tb, ps.shape[-1]), lambda b, i: (b, blk(i), 0)),
                  pl.BlockSpec(gparams.shape, lambda b, i: (0, 0)),
                  st_spec],
        out_specs=[pl.BlockSpec((1, Tb, vw), lambda b, i: (b, blk(i), 0)), st_spec],
        out_shape=[jax.ShapeDtypeStruct((B, T, vw), F32), jax.ShapeDtypeStruct(s0.shape, F32)],
        name="delta_scan",
        compiler_params=_params(("parallel", "arbitrary")),
    )(pm, pm, pm, ps, gparams, s0)


def delta_bidir(pm_c, ps_c, pm_l, ps_l, a_log, dt_bias, q_off, k_off, v_off, col0):
    B = pm_l.shape[0]
    heads = a_log.shape[-1]
    ncol = ps_l.shape[-1]
    gparams = jnp.zeros((8, ncol), F32)
    gparams = gparams.at[0, col0:col0 + 2 * heads].set(dt_bias.reshape(-1))
    gparams = gparams.at[1, col0:col0 + 2 * heads].set(-jnp.exp(a_log).reshape(-1))
    out = []
    for d in (0, 1):
        s0 = jnp.zeros((B, heads, DN_DK, DN_DV), F32)
        args = (q_off, k_off, v_off, heads, col0 + d * heads, col0 + 2 * heads + d * heads)
        oc, s1 = delta_scan(pm_c, ps_c, gparams, s0, bool(d), *args)
        ol, _ = delta_scan(pm_l, ps_l, gparams, s1, bool(d), *args)
        out.append((oc, ol))
    return out


def _rglru_kernel(x_ref, cw_ref, cb_ref, wr_ref, br_ref, wi_ref, bi_ref, lam_ref, h0_ref, o_ref, hT_ref,
                  *, rev, nc, seg):
    @pl.when(pl.program_id(2) == 0)
    def _():
        hT_ref[...] = h0_ref[...]

    n = x_ref.shape[1]
    xc = _dwconv(x_ref[0], cw_ref, seg) + cb_ref[...]
    xb = xc.astype(BF16)
    r = jax.nn.sigmoid(_dot(xb, wr_ref[0].astype(BF16)) + br_ref[...])
    i = jax.nn.sigmoid(_dot(xb, wi_ref[0].astype(BF16)) + bi_ref[...])
    log_a = -RG_C * r * _softplus(-lam_ref[...])
    a = jnp.exp(log_a)
    bx = jnp.sqrt(1.0 - jnp.exp(2.0 * log_a)) * (i * xc)

    pos = lax.broadcasted_iota(jnp.int32, (n, 1), 0) % CHUNK
    s = 1
    while s < CHUNK:
        ok = (pos < CHUNK - s) if rev else (pos >= s)
        shift = (n - s) if rev else s
        a_sh = pltpu.roll(a, shift, axis=0)
        b_sh = pltpu.roll(bx, shift, axis=0)
        bx = jnp.where(ok, a * b_sh + bx, bx)
        a = jnp.where(ok, a * a_sh, a)
        s *= 2

    h = hT_ref[0]
    for ci in range(nc):
        c = (nc - 1 - ci) if rev else ci
        rows = slice(c * CHUNK, (c + 1) * CHUNK)
        hc = bx[rows] + a[rows] * h
        o_ref[0, rows, :] = hc
        h = hc[0:1] if rev else hc[CHUNK - 1:CHUNK]
    hT_ref[0] = h


def rglru_scan(pm, conv_w, conv_b, w_r, b_r, w_i, b_i, lam, h0, rev, seg, width):
    B, T, _ = pm.shape
    nblk, bw = w_r.shape[0], w_r.shape[1]
    assert bw % LANES == 0 and nblk * bw == width
    Tb = _tile(T, 1024)
    NB = T // Tb
    assert Tb % seg == 0 or (NB == 1 and seg == T)
    blk = (lambda i: NB - 1 - i) if rev else (lambda i: i)
    row = pl.BlockSpec((1, bw), lambda b, g, i: (0, g))
    return pl.pallas_call(
        functools.partial(_rglru_kernel, rev=rev, nc=Tb // CHUNK, seg=seg),
        grid=(B, nblk, NB),
        in_specs=[pl.BlockSpec((1, Tb, bw), lambda b, g, i: (b, blk(i), g)),
                  pl.BlockSpec((conv_w.shape[0], bw), lambda b, g, i: (0, g)),
                  row,
                  pl.BlockSpec((1, bw, bw), lambda b, g, i: (g, 0, 0)), row,
                  pl.BlockSpec((1, bw, bw), lambda b, g, i: (g, 0, 0)), row,
                  row,
                  pl.BlockSpec((1, 1, bw), lambda b, g, i: (b, 0, g))],
        out_specs=[pl.BlockSpec((1, Tb, bw), lambda b, g, i: (b, blk(i), g)),
                   pl.BlockSpec((1, 1, bw), lambda b, g, i: (b, 0, g))],
        out_shape=[jax.ShapeDtypeStruct((B, T, width), F32), jax.ShapeDtypeStruct(h0.shape, F32)],
        name="rglru_scan",
        compiler_params=_params(("parallel", "parallel", "arbitrary")),
    )(pm, conv_w, conv_b, w_r, b_r, w_i, b_i, lam, h0)


def rglru_bidir(pm_c, pm_l, conv_w, conv_b, w_r, b_r, w_i, b_i, lam):
    B = pm_l.shape[0]
    width = conv_b.shape[-1]
    out = []
    for d in (0, 1):
        h0 = jnp.zeros((B, 1, width), F32)
        args = (conv_w, conv_b.reshape(1, width), w_r[d], b_r[d].reshape(1, width), w_i[d],
                b_i[d].reshape(1, width), lam[d].reshape(1, width))
        oc, h1 = rglru_scan(pm_c, *args, h0, bool(d), pm_c.shape[1], width)
        ol, _ = rglru_scan(pm_l, *args, h1, bool(d), GRID_W, width)
        out.append((oc, ol))
    return out


def _pad_cols(w, mult):
    return jnp.pad(w, ((0, 0), (0, (-w.shape[-1]) % mult)))


def kernel(x, c, ctx, c_ctx, mod_w, mod_b, norm_mix_g, norm_ffn_g, ffn_w_gu, ffn_w_down, final_norm_g,
           ab_w_in, ab_w_out, ml_ig_b, ml_fg_b, ml_norm_g, dn_conv_w, dn_a_log, dn_dt_bias, dn_norm_g,
           cd_w_in, cd_w_out, rg_conv_w, rg_conv_b, rg_w_r, rg_b_r, rg_w_i, rg_b_i, rg_lambda,
           gla_w_up, gla_b_up, gla_norm_g):
    B, T, D = x.shape
    depth = mod_w.shape[0]
    mlh, dnh = ml_ig_b.shape[-1], dn_a_log.shape[-1]
    rgw = rg_conv_b.shape[-1]
    glh = gla_w_up.shape[-1] // GLA_DK

    cond = jnp.concatenate([c, c_ctx[None, :], jnp.zeros((8 - B - 1, D), F32)], axis=0)

    m_all = adaln_all(cond, mod_w, mod_b)

    xc, xl = ctx, x
    for layer in range(depth):
        last = layer == depth - 1
        j = layer // 2
        m = m_all[layer]
        lat = [t[:, None, :] for t in jnp.split(m[:B], 6, axis=-1)]
        cx = [t[:, None, :] for t in jnp.split(m[B:B + 1], 6, axis=-1)]

        if layer % 2 == 0:
            w_in = ab_w_in[j]
            s = np.cumsum((0, mlh * ML_DK, mlh * ML_DK, mlh * ML_DV, mlh * ML_DV, 2 * mlh, 2 * mlh,
                           dnh * DN_DK, dnh * DN_DK, dnh * DN_DV, dnh * DN_DV, 2 * dnh, 2 * dnh))
            w_main = jnp.concatenate([w_in[:, s[0]:s[4]], w_in[:, s[6]:s[10]]], axis=1)
            w_small = jnp.concatenate([w_in[:, s[4]:s[6]], w_in[:, s[10]:s[12]]], axis=1)
            w_out = ab_w_out[j]
            o = np.cumsum((0, mlh * ML_DK, mlh * ML_DK, mlh * ML_DV, mlh * ML_DV,
                           dnh * DN_DK, dnh * DN_DK, dnh * DN_DV))
        else:
            w_in = cd_w_in[j]
            o = np.cumsum((0, rgw, rgw, glh * GLA_DK, glh * GLA_DK, glh * GLA_DV))
            nmain = int(o[5]) + glh * GLA_DV
            w_main, w_small = w_in[:, :nmain], w_in[:, nmain:]
            w_out = cd_w_out[j]
        o = [int(v) for v in o]
        w_main = w_main.astype(BF16)
        w_small = _pad_cols(w_small, LANES).astype(BF16)
        w_out = w_out.astype(BF16)

        if layer % 2 == 0:
            conv = lambda seg: dict(conv_w=dn_conv_w[j], conv_off=o[4], norm_cols=2 * dnh * DN_DK, seg=seg,
                                    norm_group=DN_DK)
        else:
            conv = lambda seg: {}
        pm_l, ps_l = inproj(xl, norm_mix_g[layer], lat[1], lat[0], w_main, w_small, **conv(GRID_W))
        pm_c, ps_c = inproj(xc, norm_mix_g[layer], cx[1], cx[0], w_main, w_small, **conv(xc.shape[1]))

        if layer % 2 == 0:
            mix_a = mlstm_bidir(pm_c, ps_c, pm_l, ps_l, ml_ig_b[j], ml_fg_b[j], o[0], o[1], o[2])
            mix_b = delta_bidir(pm_c, ps_c, pm_l, ps_l, dn_a_log[j], dn_dt_bias[j], o[4], o[5], o[6], 4 * mlh)
            margs = (0, o[3], o[7], ml_norm_g[j], dn_norm_g[j], ML_DV, DN_DV)
        else:
            mix_a = rglru_bidir(pm_c, pm_l, rg_conv_w[j], rg_conv_b[j], rg_w_r[j], rg_b_r[j], rg_w_i[j],
                                rg_b_i[j], rg_lambda[j])
            mix_b = gla_bidir(pm_c, ps_c, pm_l, ps_l, gla_w_up[j], gla_b_up[j], o[2], o[3], o[4])
            margs = (1, o[1], o[5], jnp.ones((rgw,), F32), gla_norm_g[j], rgw, GLA_DV)

        def merged(which, pm, xres, gate):
            kind, ga_off, gb_off, na, nb, da, db = margs
            a_dirs = (mix_a[0][which], mix_a[1][which])
            b_dirs = (mix_b[0][which], mix_b[1][which])
            return merge_outproj(kind, a_dirs, b_dirs, pm, ga_off, gb_off, na, nb, da, db, xres, gate, w_out)

        w_gu = ffn_w_gu[layer].astype(BF16)
        w_dn = ffn_w_down[layer].astype(BF16)
        xl = merged(1, pm_l, xl, lat[2])
        xl = ffn(xl, norm_ffn_g[layer], lat[4], lat[3], lat[5], w_gu, w_dn, final_norm_g if last else None)
        if not last:
            xc = merged(0, pm_c, xc, cx[2])
            xc = ffn(xc, norm_ffn_g[layer], cx[4], cx[3], cx[5], w_gu, w_dn)
    return xl
```

```python
import functools

import jax
import jax.numpy as jnp
import numpy as np
from jax import lax
from jax.experimental import pallas as pl
from jax.experimental.pallas import tpu as pltpu

F32 = jnp.float32
BF16 = jnp.bfloat16

GRID_W = 64
CHUNK = 64
EPS = 1e-6
ML_DK, ML_DV = 128, 256
DN_DK, DN_DV = 128, 128
GLA_DK, GLA_DV = 128, 256
RG_C = 8.0
GLA_TAU = 16.0

LANES = 128
DELTA_GROUP = 4
GLA_GROUP = 4
FFN_SUB = 256
VMEM_LIMIT = 48 * 1024 * 1024

NT = (((1,), (1,)), ((), ()))
TN = (((0,), (0,)), ((), ()))


def _tile(n, pref):
    t = min(n, pref)
    while n % t:
        t //= 2
    return t


def _params(sem, vmem=VMEM_LIMIT):
    return pltpu.CompilerParams(dimension_semantics=sem, vmem_limit_bytes=vmem)


def _dot(a, b, dims=None):
    if dims is None:
        return jnp.dot(a, b, preferred_element_type=F32)
    return lax.dot_general(a, b, dims, preferred_element_type=F32)


def _modulated_norm(x, g, sc, sh):
    y = x * lax.rsqrt(jnp.mean(x * x, axis=-1, keepdims=True) + EPS) * g
    return y * (1.0 + sc) + sh


def _adaln_kernel(s_ref, w_ref, b_ref, o_ref):
    s = s_ref[...]
    s = (s * jax.nn.sigmoid(s)).astype(BF16)
    o_ref[0] = _dot(s, w_ref[0].astype(BF16)) + b_ref[0]


def adaln_all(cond, w_mod, b_mod):
    R, D = cond.shape
    depth, _, C = w_mod.shape
    tn = _tile(C, 1024)
    return pl.pallas_call(
        _adaln_kernel,
        grid=(depth, C // tn),
        in_specs=[pl.BlockSpec((R, D), lambda l, j: (0, 0)),
                  pl.BlockSpec((1, D, tn), lambda l, j: (l, 0, j)),
                  pl.BlockSpec((1, 1, tn), lambda l, j: (l, 0, j))],
        out_specs=pl.BlockSpec((1, R, tn), lambda l, j: (l, 0, j)),
        out_shape=jax.ShapeDtypeStruct((depth, R, C), F32),
        name="adaln_all",
        compiler_params=_params(("parallel", "parallel")),
    )(cond, w_mod, b_mod.reshape(depth, 1, C))


def _inproj_kernel(x_ref, g_ref, sc_ref, sh_ref, w_ref, ws_ref, cw_ref, o_ref, os_ref, h_ref,
                   *, conv_tiles, norm_tiles, seg, norm_group):
    j = pl.program_id(2)

    @pl.when(j == 0)
    def _():
        h = _modulated_norm(x_ref[0], g_ref[...], sc_ref[0], sh_ref[0]).astype(BF16)
        h_ref[...] = h
        os_ref[0] = _dot(h, ws_ref[...])

    o_ref[0] = _dot(h_ref[...], w_ref[...])

    @pl.when((j >= conv_tiles[0]) & (j < conv_tiles[1]))
    def _():
        o_ref[0] = jax.nn.silu(_dwconv(o_ref[0], cw_ref, seg))

    @pl.when((j >= norm_tiles[0]) & (j < norm_tiles[1]))
    def _():
        for lo in range(0, o_ref.shape[-1], norm_group):
            y = o_ref[0, :, lo:lo + norm_group]
            o_ref[0, :, lo:lo + norm_group] = y * lax.rsqrt(jnp.sum(y * y, axis=-1, keepdims=True) + EPS)


def inproj(x, g, sc, sh, w_main, w_small, conv_w=None, conv_off=0, norm_cols=0, seg=1, norm_group=LANES):
    B, T, D = x.shape
    C = w_main.shape[1]
    Cs = w_small.shape[1]
    tm, tn = _tile(T, 1024), _tile(C, 1024)
    if conv_w is None:
        conv_w, conv_tiles, norm_tiles = jnp.zeros((1, tn), F32), (0, 0), (0, 0)
    else:
        assert conv_off % tn == 0 and conv_w.shape[1] % tn == 0 and norm_cols % tn == 0 and tn % norm_group == 0
        assert tm % seg == 0 or (tm == T and seg == T)
        conv_tiles = (conv_off // tn, (conv_off + conv_w.shape[1]) // tn)
        norm_tiles = (conv_off // tn, (conv_off + norm_cols) // tn)
    n_conv = conv_w.shape[1] // tn
    mod = (lambda b, i, j: (b, 0, 0)) if sc.shape[0] == B else (lambda b, i, j: (0, 0, 0))
    return pl.pallas_call(
        functools.partial(_inproj_kernel, conv_tiles=conv_tiles, norm_tiles=norm_tiles, seg=seg,
                          norm_group=norm_group),
        grid=(B, T // tm, C // tn),
        in_specs=[pl.BlockSpec((1, tm, D), lambda b, i, j: (b, i, 0)),
                  pl.BlockSpec((1, D), lambda b, i, j: (0, 0)),
                  pl.BlockSpec((1, 1, D), mod),
                  pl.BlockSpec((1, 1, D), mod),
                  pl.BlockSpec((D, tn), lambda b, i, j: (0, j)),
                  pl.BlockSpec((D, Cs), lambda b, i, j: (0, 0)),
                  pl.BlockSpec((conv_w.shape[0], tn),
                               lambda b, i, j: (0, jnp.clip(j - conv_tiles[0], 0, n_conv - 1)))],
        out_specs=[pl.BlockSpec((1, tm, tn), lambda b, i, j: (b, i, j)),
                   pl.BlockSpec((1, tm, Cs), lambda b, i, j: (b, i, 0))],
        out_shape=[jax.ShapeDtypeStruct((B, T, C), F32), jax.ShapeDtypeStruct((B, T, Cs), F32)],
        scratch_shapes=[pltpu.VMEM((tm, D), BF16)],
        name="inproj",
        compiler_params=_params(("parallel", "parallel", "arbitrary")),
    )(x, g.reshape(1, D), sc, sh, w_main, w_small, conv_w)


def _head_norm(o, d, g):
    parts = []
    for h in range(o.shape[-1] // d):
        oh = o[:, h * d:(h + 1) * d]
        parts.append(oh * lax.rsqrt(jnp.mean(oh * oh, axis=-1, keepdims=True) + EPS))
    return jnp.concatenate(parts, axis=-1) * g


def _merge_kernel(af_ref, ab_ref, bf_ref, bb_ref, ga_ref, gb_ref, na_ref, nb_ref, x_ref, gate_ref, w_ref, o_ref,
                  *, kind, da, db):
    a = af_ref[0] + ab_ref[0]
    b = bf_ref[0] + bb_ref[0]
    if kind == 0:
        ya = _head_norm(a, da, na_ref[...]) * jax.nn.sigmoid(ga_ref[0])
    else:
        ya = jax.nn.gelu(ga_ref[0]) * a
    gb = gb_ref[0]
    yb = _head_norm(b, db, nb_ref[...]) * (gb * jax.nn.sigmoid(gb))
    wa = a.shape[-1]
    y = _dot(ya.astype(BF16), w_ref[0:wa, :]) + _dot(yb.astype(BF16), w_ref[wa:, :])
    o_ref[0] = x_ref[0] + gate_ref[0] * y


def merge_outproj(kind, a_dirs, b_dirs, pm, ga_off, gb_off, na, nb, da, db, x, gate, w):
    B, T, D = x.shape
    wa, wb = a_dirs[0].shape[-1], b_dirs[0].shape[-1]
    assert ga_off % wa == 0 and gb_off % wb == 0
    tm = _tile(T, 256)
    mod = (lambda b, i: (b, 0, 0)) if gate.shape[0] == B else (lambda b, i: (0, 0, 0))
    tok = lambda width, col: pl.BlockSpec((1, tm, width), lambda b, i: (b, i, col))
    return pl.pallas_call(
        functools.partial(_merge_kernel, kind=kind, da=da, db=db),
        grid=(B, T // tm),
        in_specs=[tok(wa, 0), tok(wa, 0), tok(wb, 0), tok(wb, 0), tok(wa, ga_off // wa), tok(wb, gb_off // wb),
                  pl.BlockSpec((1, wa), lambda b, i: (0, 0)),
                  pl.BlockSpec((1, wb), lambda b, i: (0, 0)),
                  tok(D, 0),
                  pl.BlockSpec((1, 1, D), mod),
                  pl.BlockSpec((wa + wb, D), lambda b, i: (0, 0))],
        out_specs=tok(D, 0),
        out_shape=jax.ShapeDtypeStruct((B, T, D), F32),
        name="merge_outproj",
        compiler_params=_params(("parallel", "parallel")),
    )(a_dirs[0], a_dirs[1], b_dirs[0], b_dirs[1], pm, pm, na.reshape(1, wa), nb.reshape(1, wb), x, gate, w)


def _ffn_kernel(x_ref, g_ref, sc_ref, sh_ref, gate_ref, wg_ref, wu_ref, wd_ref, fg_ref, o_ref, h_ref,
                *, final_norm):
    k = pl.program_id(2)

    @pl.when(k == 0)
    def _():
        h_ref[...] = _modulated_norm(x_ref[0], g_ref[...], sc_ref[0], sh_ref[0]).astype(BF16)
        o_ref[...] = jnp.zeros_like(o_ref)

    h = h_ref[...]
    acts = []
    for lo in range(0, wg_ref.shape[1], FFN_SUB):
        a = _dot(h, wg_ref[:, lo:lo + FFN_SUB])
        u = _dot(h, wu_ref[:, lo:lo + FFN_SUB])
        acts.append((a * jax.nn.sigmoid(a) * u).astype(BF16))
    o_ref[0] += _dot(jnp.concatenate(acts, axis=1), wd_ref[...])

    @pl.when(k == pl.num_programs(2) - 1)
    def _():
        r = x_ref[0] + gate_ref[0] * o_ref[0]
        if final_norm:
            r = r * lax.rsqrt(jnp.mean(r * r, axis=-1, keepdims=True) + EPS) * fg_ref[...]
        o_ref[0] = r


def ffn(x, g, sc, sh, gate, w_gu, w_down, final_g=None):
    B, T, D = x.shape
    Fh = w_down.shape[0]
    tm, tf = _tile(T, 512), _tile(Fh, 512)
    nf = Fh // tf
    mod = (lambda b, i, k: (b, 0, 0)) if sc.shape[0] == B else (lambda b, i, k: (0, 0, 0))
    fg = jnp.ones((1, D), F32) if final_g is None else final_g.reshape(1, D)
    return pl.pallas_call(
        functools.partial(_ffn_kernel, final_norm=final_g is not None),
        grid=(B, T // tm, nf),
        in_specs=[pl.BlockSpec((1, tm, D), lambda b, i, k: (b, i, 0)),
                  pl.BlockSpec((1, D), lambda b, i, k: (0, 0)),
                  pl.BlockSpec((1, 1, D), mod),
                  pl.BlockSpec((1, 1, D), mod),
                  pl.BlockSpec((1, 1, D), mod),
                  pl.BlockSpec((D, tf), lambda b, i, k: (0, k)),
                  pl.BlockSpec((D, tf), lambda b, i, k: (0, k + nf)),
                  pl.BlockSpec((tf, D), lambda b, i, k: (k, 0)),
                  pl.BlockSpec((1, D), lambda b, i, k: (0, 0))],
        out_specs=pl.BlockSpec((1, tm, D), lambda b, i, k: (b, i, 0)),
        out_shape=jax.ShapeDtypeStruct((B, T, D), F32),
        scratch_shapes=[pltpu.VMEM((tm, D), BF16)],
        name="ffn",
        compiler_params=_params(("parallel", "parallel", "arbitrary")),
    )(x, g.reshape(1, D), sc, sh, gate, w_gu, w_gu, w_down, fg)


def _incl_mask(rev):
    r = lax.broadcasted_iota(jnp.int32, (CHUNK, CHUNK), 0)
    c = lax.broadcasted_iota(jnp.int32, (CHUNK, CHUNK), 1)
    return (r <= c) if rev else (r >= c)


def _split_bf16(x):
    hi = x.astype(BF16)
    return hi, (x - hi.astype(F32)).astype(BF16)


def _chunk_cumsum(mask_bf16, x):
    hi, lo = _split_bf16(x)
    return _dot(mask_bf16, hi) + _dot(mask_bf16, lo)


def _log_sigmoid(z):
    return jnp.minimum(z, 0.0) - jnp.log1p(jnp.exp(-jnp.abs(z)))


def _softplus(x):
    return jnp.maximum(x, 0.0) + jnp.log1p(jnp.exp(-jnp.abs(x)))


def _time_block(T):
    return _tile(T, 512)


def _dwconv(x, w_ref, seg):
    n = x.shape[0]
    taps = w_ref.shape[0]
    pos = lax.broadcasted_iota(jnp.int32, (n, 1), 0) % seg
    out = None
    for j in range(taps):
        sh = j - taps // 2
        xs = x if sh == 0 else pltpu.roll(x, (-sh) % n, axis=0)
        ok = (pos + sh >= 0) & (pos + sh < seg)
        term = jnp.where(ok, xs, 0.0) * w_ref[j:j + 1, :]
        out = term if out is None else out + term
    return out


def _gla_kernel(q_ref, k_ref, v_ref, ps_ref, wup_ref, bup_ref, s0_ref, o_ref, st_ref, lg_ref, *, rev, heads, nc):
    @pl.when(pl.program_id(1) == 0)
    def _():
        st_ref[...] = s0_ref[...]

    mask = _incl_mask(rev)
    mask_bf = jnp.where(mask, 1.0, 0.0).astype(BF16)
    z = _dot(ps_ref[0].astype(BF16), wup_ref[...]) + bup_ref[...]
    lg_ref[...] = _log_sigmoid(z) * (1.0 / GLA_TAU)

    hs = range(heads)
    ks = [slice(h * GLA_DK, (h + 1) * GLA_DK) for h in hs]
    vs = [slice(h * GLA_DV, (h + 1) * GLA_DV) for h in hs]

    def prepare(rows):
        lg = lg_ref[rows, :]
        b = _chunk_cumsum(mask_bf, lg)
        b_end = jnp.sum(lg, axis=0, keepdims=True)
        k = k_ref[0, rows, :]
        qe = (q_ref[0, rows, :] * (GLA_DK ** -0.5) * jnp.exp(b)).astype(BF16)
        ke = (k * jnp.exp(-b)).astype(BF16)
        kd = (k * jnp.exp(b_end - b)).astype(BF16)
        v = [v_ref[0, rows, vs[h]].astype(BF16) for h in hs]
        att = [_dot(qe[:, ks[h]], ke[:, ks[h]], NT) for h in hs]
        upd = [_dot(v[h], kd[:, ks[h]], TN) for h in hs]
        intra = [_dot(jnp.where(mask, att[h], 0.0).astype(BF16), v[h]) for h in hs]
        return qe, intra, upd, jnp.exp(b_end)

    def body(gi, carry):
        rows_l = []
        for j in range(GLA_GROUP):
            ci = gi * GLA_GROUP + j
            c = (nc - 1 - ci) if rev else ci
            rows_l.append(pl.ds(pl.multiple_of(c * CHUNK, CHUNK), CHUNK))
        preps = [prepare(rows) for rows in rows_l]
        sts = [st_ref[0, h] for h in hs]
        for rows, (qe, intra, upd, e_end) in zip(rows_l, preps):
            inter = [_dot(qe[:, ks[h]], sts[h].astype(BF16), NT) for h in hs]
            for h in hs:
                o_ref[0, rows, vs[h]] = intra[h] + inter[h]
            sts = [sts[h] * e_end[:, ks[h]] + upd[h] for h in hs]
        for h in hs:
            st_ref[0, h] = sts[h]
        return carry

    assert nc % GLA_GROUP == 0
    lax.fori_loop(0, nc // GLA_GROUP, body, 0)


def gla_scan(pm, ps, wup, bup, s0, rev, q_off, k_off, v_off, heads):
    B, T, _ = pm.shape
    Tb = _time_block(T)
    NB = T // Tb
    kw, vw = heads * GLA_DK, heads * GLA_DV
    assert q_off % kw == 0 and k_off % kw == 0 and v_off % vw == 0
    blk = (lambda i: NB - 1 - i) if rev else (lambda i: i)
    return pl.pallas_call(
        functools.partial(_gla_kernel, rev=rev, heads=heads, nc=Tb // CHUNK),
        grid=(B, NB),
        in_specs=[pl.BlockSpec((1, Tb, kw), lambda b, i: (b, blk(i), q_off // kw)),
                  pl.BlockSpec((1, Tb, kw), lambda b, i: (b, blk(i), k_off // kw)),
                  pl.BlockSpec((1, Tb, vw), lambda b, i: (b, blk(i), v_off // vw)),
                  pl.BlockSpec((1, Tb, ps.shape[-1]), lambda b, i: (b, blk(i), 0)),
                  pl.BlockSpec(wup.shape, lambda b, i: (0, 0)),
                  pl.BlockSpec(bup.shape, lambda b, i: (0, 0)),
                  pl.BlockSpec((1,) + s0.shape[1:], lambda b, i: (b, 0, 0, 0))],
        out_specs=[pl.BlockSpec((1, Tb, vw), lambda b, i: (b, blk(i), 0)),
                   pl.BlockSpec((1,) + s0.shape[1:], lambda b, i: (b, 0, 0, 0))],
        out_shape=[jax.ShapeDtypeStruct((B, T, vw), F32), jax.ShapeDtypeStruct(s0.shape, F32)],
        scratch_shapes=[pltpu.VMEM((Tb, kw), F32)],
        name="gla_scan",
        compiler_params=_params(("parallel", "arbitrary")),
    )(pm, pm, pm, ps, wup, bup, s0)


def gla_bidir(pm_c, ps_c, pm_l, ps_l, w_up, b_up, q_off, k_off, v_off):
    B = pm_l.shape[0]
    rank, hk = w_up.shape[1], w_up.shape[2]
    heads = hk // GLA_DK
    out = []
    for d in (0, 1):
        wup = jnp.zeros((ps_l.shape[-1], hk), F32).at[d * rank:(d + 1) * rank].set(w_up[d]).astype(BF16)
        bup = b_up[d].reshape(1, hk)
        s0 = jnp.zeros((B, heads, GLA_DV, GLA_DK), F32)
        oc, s1 = gla_scan(pm_c, ps_c, wup, bup, s0, bool(d), q_off, k_off, v_off, heads)
        ol, _ = gla_scan(pm_l, ps_l, wup, bup, s1, bool(d), q_off, k_off, v_off, heads)
        out.append((oc, ol))
    return out


def _mlstm_kernel(q_ref, k_ref, v_ref, ps_ref, gb_ref, c0_ref, nm0_ref, o_ref, c_ref, nm_ref,
                  *, rev, heads, nc, ig_col, fg_col):
    @pl.when(pl.program_id(1) == 0)
    def _():
        c_ref[...] = c0_ref[...]
        nm_ref[...] = nm0_ref[...]

    mask = _incl_mask(rev)
    mask_bf = jnp.where(mask, 1.0, 0.0).astype(BF16)

    def body(ci, carry):
        c = (nc - 1 - ci) if rev else ci
        rows = pl.ds(pl.multiple_of(c * CHUNK, CHUNK), CHUNK)
        g = ps_ref[0, rows, :] + gb_ref[...]
        lf = _log_sigmoid(g)
        bcum = _chunk_cumsum(mask_bf, lf)
        b_end = jnp.sum(lf, axis=0, keepdims=True)
        bcum_t, g_t = bcum.T, g.T
        hs = range(heads)
        cols = [(ig_col + h, fg_col + h) for h in hs]
        b_col = [bcum[:, fc:fc + 1] for _, fc in cols]
        b_l = [b_end[:, fc:fc + 1] for _, fc in cols]
        q = [q_ref[0, rows, h * ML_DK:(h + 1) * ML_DK] * (ML_DK ** -0.5) for h in hs]
        k = [k_ref[0, rows, h * ML_DK:(h + 1) * ML_DK] for h in hs]
        v = [v_ref[0, rows, h * ML_DV:(h + 1) * ML_DV] for h in hs]
        qb = [t.astype(BF16) for t in q]
        kb = [t.astype(BF16) for t in k]
        cst = [c_ref[0, h] for h in hs]
        n = [nm_ref[0, h, 0:1, :] for h in hs]
        m = [nm_ref[0, h, 1:2, 0:1] for h in hs]
        qk = [_dot(qb[h], kb[h], NT) for h in hs]
        q_c = [_dot(qb[h], cst[h].astype(BF16), NT) for h in hs]
        q_n = [jnp.sum(q[h] * n[h], axis=-1, keepdims=True) for h in hs]
        d_log = [jnp.where(mask, b_col[h] - bcum_t[fc:fc + 1, :] + g_t[ic:ic + 1, :], -jnp.inf)
                 for h, (ic, fc) in enumerate(cols)]
        inter_log = [b_col[h] + m[h] for h in hs]
        m_out = [jnp.maximum(jnp.max(d_log[h], axis=-1, keepdims=True), inter_log[h]) for h in hs]
        s = [qk[h] * jnp.exp(d_log[h] - m_out[h]) for h in hs]
        e_inter = [jnp.exp(inter_log[h] - m_out[h]) for h in hs]
        num = [_dot(s[h].astype(BF16), v[h].astype(BF16)) + e_inter[h] * q_c[h] for h in hs]
        den = [jnp.sum(s[h], axis=-1, keepdims=True) + e_inter[h] * q_n[h] for h in hs]
        for h in hs:
            o_ref[0, rows, h * ML_DV:(h + 1) * ML_DV] = num[h] / jnp.maximum(jnp.abs(den[h]), jnp.exp(-m_out[h]))
        w_log = [b_l[h] - b_col[h] + g[:, ic:ic + 1] for h, (ic, _) in enumerate(cols)]
        m_loc = [jnp.max(t, axis=0, keepdims=True) for t in w_log]
        w = [jnp.exp(w_log[h] - m_loc[h]) for h in hs]
        m_new = [jnp.maximum(b_l[h] + m[h], m_loc[h]) for h in hs]
        a = [jnp.exp(b_l[h] + m[h] - m_new[h]) for h in hs]
        sc = [jnp.exp(m_loc[h] - m_new[h]) for h in hs]
        upd = [_dot((w[h] * v[h]).astype(BF16), kb[h], TN) for h in hs]
        for h in hs:
            c_ref[0, h] = a[h] * cst[h] + sc[h] * upd[h]
            nm_ref[0, h, 0:1, :] = a[h] * n[h] + sc[h] * jnp.sum(w[h] * k[h], axis=0, keepdims=True)
            nm_ref[0, h, 1:2, :] = jnp.broadcast_to(m_new[h], (1, ML_DK))
        return carry

    lax.fori_loop(0, nc, body, 0)


def mlstm_scan(pm, ps, gbias, state, rev, q_off, k_off, v_off, heads, ig_col, fg_col):
    B, T, _ = pm.shape
    Tb = _time_block(T)
    NB = T // Tb
    kw, vw = heads * ML_DK, heads * ML_DV
    assert q_off % kw == 0 and k_off % kw == 0 and v_off % vw == 0
    c0, nm0 = state
    blk = (lambda i: NB - 1 - i) if rev else (lambda i: i)
    st_spec = lambda a: pl.BlockSpec((1,) + a.shape[1:], lambda b, i: (b, 0, 0, 0))
    o, c1, nm1 = pl.pallas_call(
        functools.partial(_mlstm_kernel, rev=rev, heads=heads, nc=Tb // CHUNK, ig_col=ig_col, fg_col=fg_col),
        grid=(B, NB),
        in_specs=[pl.BlockSpec((1, Tb, kw), lambda b, i: (b, blk(i), q_off // kw)),
                  pl.BlockSpec((1, Tb, kw), lambda b, i: (b, blk(i), k_off // kw)),
                  pl.BlockSpec((1, Tb, vw), lambda b, i: (b, blk(i), v_off // vw)),
                  pl.BlockSpec((1, Tb, ps.shape[-1]), lambda b, i: (b, blk(i), 0)),
                  pl.BlockSpec(gbias.shape, lambda b, i: (0, 0)),
                  st_spec(c0), st_spec(nm0)],
        out_specs=[pl.BlockSpec((1, Tb, vw), lambda b, i: (b, blk(i), 0)), st_spec(c0), st_spec(nm0)],
        out_shape=[jax.ShapeDtypeStruct((B, T, vw), F32), jax.ShapeDtypeStruct(c0.shape, F32),
                   jax.ShapeDtypeStruct(nm0.shape, F32)],
        name="mlstm_scan",
        compiler_params=_params(("parallel", "arbitrary")),
    )(pm, pm, pm, ps, gbias, c0, nm0)
    return o, (c1, nm1)


def mlstm_bidir(pm_c, ps_c, pm_l, ps_l, ig_b, fg_b, q_off, k_off, v_off):
    B = pm_l.shape[0]
    heads = ig_b.shape[-1]
    ncol = ps_l.shape[-1]
    gbias = jnp.zeros((1, ncol), F32).at[0, :4 * heads].set(jnp.concatenate([ig_b.reshape(-1), fg_b.reshape(-1)]))
    out = []
    for d in (0, 1):
        st = (jnp.zeros((B, heads, ML_DV, ML_DK), F32), jnp.zeros((B, heads, 8, ML_DK), F32))
        args = (bool(d), q_off, k_off, v_off, heads, d * heads, 2 * heads + d * heads)
        oc, st = mlstm_scan(pm_c, ps_c, gbias, st, *args)
        ol, _ = mlstm_scan(pm_l, ps_l, gbias, st, *args)
        out.append((oc, ol))
    return out


def _block_join_masks():
    r = lax.broadcasted_iota(jnp.int32, (CHUNK, CHUNK), 0)
    c = lax.broadcasted_iota(jnp.int32, (CHUNK, CHUNK), 1)
    masks, s = [], 1
    while s < CHUNK:
        masks.append(jnp.where((r // (2 * s) == c // (2 * s)) & (r // s != c // s), 1.0, 0.0))
        s *= 2
    return masks


def _unit_tri_inverses(a_list, join_masks):
    r = lax.broadcasted_iota(jnp.int32, (CHUNK, CHUNK), 0)
    c = lax.broadcasted_iota(jnp.int32, (CHUNK, CHUNK), 1)
    eye = jnp.where(r == c, 1.0, 0.0)
    ts = [eye - a * join_masks[0] for a in a_list]
    for m in join_masks[1:]:
        tb = [t.astype(BF16) for t in ts]
        xs = [_dot(t, (a * m).astype(BF16)) for t, a in zip(tb, a_list)]
        ts = [t - _dot(x.astype(BF16), t_b) for t, x, t_b in zip(ts, xs, tb)]
    return ts


def _delta_kernel(q_ref, k_ref, v_ref, ps_ref, gp_ref, s0_ref, o_ref, st_ref, *, rev, heads, nc, g_col, beta_col):
    @pl.when(pl.program_id(1) == 0)
    def _():
        st_ref[...] = s0_ref[...]

    mask = _incl_mask(rev)
    mask_bf = jnp.where(mask, 1.0, 0.0).astype(BF16)
    r = lax.broadcasted_iota(jnp.int32, (CHUNK, CHUNK), 0)
    c = lax.broadcasted_iota(jnp.int32, (CHUNK, CHUNK), 1)
    off_diag = jnp.where(r != c, 1.0, 0.0)
    join_masks = _block_join_masks()
    L = CHUNK

    def prepare(rows):
        pre = ps_ref[0, rows, :]
        g = gp_ref[1:2, :] * _softplus(pre + gp_ref[0:1, :])
        beta = jax.nn.sigmoid(pre)
        gcum = _chunk_cumsum(mask_bf, g)
        g_end = jnp.sum(g, axis=0, keepdims=True)
        gcum_t = gcum.T
        e_cum, e_rest, e_end = jnp.exp(gcum), jnp.exp(g_end - gcum), jnp.exp(g_end)
        a_l, rhs_l, qe_l, kd_l, qk_l, end_l = [], [], [], [], [], []
        for h in range(heads):
            gc, bc = g_col + h, beta_col + h
            ks = slice(h * DN_DK, (h + 1) * DN_DK)
            q, k = q_ref[0, rows, ks] * (DN_DK ** -0.5), k_ref[0, rows, ks]
            b_c = beta[:, bc:bc + 1]
            decay = jnp.exp(jnp.where(mask, gcum[:, gc:gc + 1] - gcum_t[gc:gc + 1, :], -jnp.inf))
            kbeta = k * b_c
            both = _dot(jnp.concatenate([kbeta, q], axis=0).astype(BF16), k.astype(BF16), NT)
            a_l.append(both[:L] * decay * off_diag)
            qk_l.append((both[L:] * decay).astype(BF16))
            vbeta = v_ref[0, rows, h * DN_DV:(h + 1) * DN_DV] * b_c
            rhs_l.append(jnp.concatenate([vbeta, kbeta * e_cum[:, gc:gc + 1]], axis=1).astype(BF16))
            qe_l.append((q * e_cum[:, gc:gc + 1]).astype(BF16))
            kd_l.append((k * e_rest[:, gc:gc + 1]).astype(BF16))
            end_l.append(e_end[:, gc:gc + 1])
        return a_l, rhs_l, qe_l, kd_l, qk_l, end_l

    def body(gi, carry):
        rows_l = []
        for j in range(DELTA_GROUP):
            ci = gi * DELTA_GROUP + j
            cidx = (nc - 1 - ci) if rev else ci
            rows_l.append(pl.ds(pl.multiple_of(cidx * CHUNK, CHUNK), CHUNK))
        preps = [prepare(rows) for rows in rows_l]
        t_l = _unit_tri_inverses([a for p in preps for a in p[0]], join_masks)
        uw_l = [_dot(t.astype(BF16), rhs) for t, rhs in zip(t_l, [r for p in preps for r in p[1]])]
        hs = range(heads)
        sts = [st_ref[0, h] for h in hs]
        for j, (rows, (_, _, qe_l, kd_l, qk_l, end_l)) in enumerate(zip(rows_l, preps)):
            uws = uw_l[j * heads:(j + 1) * heads]
            lhs = [jnp.concatenate([uws[h][:, DN_DV:].astype(BF16), qe_l[h]], axis=0) for h in hs]
            ws = [_dot(lhs[h], sts[h].astype(BF16)) for h in hs]
            v_new = [(uws[h][:, :DN_DV] - ws[h][:L]).astype(BF16) for h in hs]
            outs = [ws[h][L:] + _dot(qk_l[h], v_new[h]) for h in hs]
            incs = [_dot(kd_l[h], v_new[h], TN) for h in hs]
            for h in hs:
                o_ref[0, rows, h * DN_DV:(h + 1) * DN_DV] = outs[h]
            sts = [end_l[h] * sts[h] + incs[h] for h in hs]
        for h in hs:
            st_ref[0, h] = sts[h]
        return carry

    assert nc % DELTA_GROUP == 0
    lax.fori_loop(0, nc // DELTA_GROUP, body, 0)


def delta_scan(pm, ps, gparams, s0, rev, q_off, k_off, v_off, heads, g_col, beta_col):
    B, T, _ = pm.shape
    Tb = _time_block(T)
    NB = T // Tb
    kw, vw = heads * DN_DK, heads * DN_DV
    assert q_off % kw == 0 and k_off % kw == 0 and v_off % vw == 0
    blk = (lambda i: NB - 1 - i) if rev else (lambda i: i)
    st_spec = pl.BlockSpec((1,) + s0.shape[1:], lambda b, i: (b, 0, 0, 0))
    return pl.pallas_call(
        functools.partial(_delta_kernel, rev=rev, heads=heads, nc=Tb // CHUNK, g_col=g_col, beta_col=beta_col),
        grid=(B, NB),
        in_specs=[pl.BlockSpec((1, Tb, kw), lambda b, i: (b, blk(i), q_off // kw)),
                  pl.BlockSpec((1, Tb, kw), lambda b, i: (b, blk(i), k_off // kw)),
                  pl.BlockSpec((1, Tb, vw), lambda b, i: (b, blk(i), v_off // vw)),
                  pl.BlockSpec((1, Tb, ps.shape[-1]), lambda b, i: (b, blk(i), 0)),
                  pl.BlockSpec(gparams.shape, lambda b, i: (0, 0)),
                  st_spec],
        out_specs=[pl.BlockSpec((1, Tb, vw), lambda b, i: (b, blk(i), 0)), st_spec],
        out_shape=[jax.ShapeDtypeStruct((B, T, vw), F32), jax.ShapeDtypeStruct(s0.shape, F32)],
        name="delta_scan",
        compiler_params=_params(("parallel", "arbitrary")),
    )(pm, pm, pm, ps, gparams, s0)


def delta_bidir(pm_c, ps_c, pm_l, ps_l, a_log, dt_bias, q_off, k_off, v_off, col0):
    B = pm_l.shape[0]
    heads = a_log.shape[-1]
    ncol = ps_l.shape[-1]
    gparams = jnp.zeros((8, ncol), F32)
    gparams = gparams.at[0, col0:col0 + 2 * heads].set(dt_bias.reshape(-1))
    gparams = gparams.at[1, col0:col0 + 2 * heads].set(-jnp.exp(a_log).reshape(-1))
    out = []
    for d in (0, 1):
        s0 = jnp.zeros((B, heads, DN_DK, DN_DV), F32)
        args = (q_off, k_off, v_off, heads, col0 + d * heads, col0 + 2 * heads + d * heads)
        oc, s1 = delta_scan(pm_c, ps_c, gparams, s0, bool(d), *args)
        ol, _ = delta_scan(pm_l, ps_l, gparams, s1, bool(d), *args)
        out.append((oc, ol))
    return out


def _rglru_kernel(x_ref, cw_ref, cb_ref, wr_ref, br_ref, wi_ref, bi_ref, lam_ref, h0_ref, o_ref, hT_ref,
                  *, rev, nc, seg):
    @pl.when(pl.program_id(2) == 0)
    def _():
        hT_ref[...] = h0_ref[...]

    n = x_ref.shape[1]
    xc = _dwconv(x_ref[0], cw_ref, seg) + cb_ref[...]
    xb = xc.astype(BF16)
    r = jax.nn.sigmoid(_dot(xb, wr_ref[0].astype(BF16)) + br_ref[...])
    i = jax.nn.sigmoid(_dot(xb, wi_ref[0].astype(BF16)) + bi_ref[...])
    log_a = -RG_C * r * _softplus(-lam_ref[...])
    a = jnp.exp(log_a)
    bx = jnp.sqrt(1.0 - jnp.exp(2.0 * log_a)) * (i * xc)

    pos = lax.broadcasted_iota(jnp.int32, (n, 1), 0) % CHUNK
    s = 1
    while s < CHUNK:
        ok = (pos < CHUNK - s) if rev else (pos >= s)
        shift = (n - s) if rev else s
        a_sh = pltpu.roll(a, shift, axis=0)
        b_sh = pltpu.roll(bx, shift, axis=0)
        bx = jnp.where(ok, a * b_sh + bx, bx)
        a = jnp.where(ok, a * a_sh, a)
        s *= 2

    h = hT_ref[0]
    for ci in range(nc):
        c = (nc - 1 - ci) if rev else ci
        rows = slice(c * CHUNK, (c + 1) * CHUNK)
        hc = bx[rows] + a[rows] * h
        o_ref[0, rows, :] = hc
        h = hc[0:1] if rev else hc[CHUNK - 1:CHUNK]
    hT_ref[0] = h


def rglru_scan(pm, conv_w, conv_b, w_r, b_r, w_i, b_i, lam, h0, rev, seg, width):
    B, T, _ = pm.shape
    nblk, bw = w_r.shape[0], w_r.shape[1]
    assert bw % LANES == 0 and nblk * bw == width
    Tb = _tile(T, 1024)
    NB = T // Tb
    assert Tb % seg == 0 or (NB == 1 and seg == T)
    blk = (lambda i: NB - 1 - i) if rev else (lambda i: i)
    row = pl.BlockSpec((1, bw), lambda b, g, i: (0, g))
    return pl.pallas_call(
        functools.partial(_rglru_kernel, rev=rev, nc=Tb // CHUNK, seg=seg),
        grid=(B, nblk, NB),
        in_specs=[pl.BlockSpec((1, Tb, bw), lambda b, g, i: (b, blk(i), g)),
                  pl.BlockSpec((conv_w.shape[0], bw), lambda b, g, i: (0, g)),
                  row,
                  pl.BlockSpec((1, bw, bw), lambda b, g, i: (g, 0, 0)), row,
                  pl.BlockSpec((1, bw, bw), lambda b, g, i: (g, 0, 0)), row,
                  row,
                  pl.BlockSpec((1, 1, bw), lambda b, g, i: (b, 0, g))],
        out_specs=[pl.BlockSpec((1, Tb, bw), lambda b, g, i: (b, blk(i), g)),
                   pl.BlockSpec((1, 1, bw), lambda b, g, i: (b, 0, g))],
        out_shape=[jax.ShapeDtypeStruct((B, T, width), F32), jax.ShapeDtypeStruct(h0.shape, F32)],
        name="rglru_scan",
        compiler_params=_params(("parallel", "parallel", "arbitrary")),
    )(pm, conv_w, conv_b, w_r, b_r, w_i, b_i, lam, h0)


def rglru_bidir(pm_c, pm_l, conv_w, conv_b, w_r, b_r, w_i, b_i, lam):
    B = pm_l.shape[0]
    width = conv_b.shape[-1]
    out = []
    for d in (0, 1):
        h0 = jnp.zeros((B, 1, width), F32)
        args = (conv_w, conv_b.reshape(1, width), w_r[d], b_r[d].reshape(1, width), w_i[d],
                b_i[d].reshape(1, width), lam[d].reshape(1, width))
        oc, h1 = rglru_scan(pm_c, *args, h0, bool(d), pm_c.shape[1], width)
        ol, _ = rglru_scan(pm_l, *args, h1, bool(d), GRID_W, width)
        out.append((oc, ol))
    return out


def _pad_cols(w, mult):
    return jnp.pad(w, ((0, 0), (0, (-w.shape[-1]) % mult)))


def kernel(x, c, ctx, c_ctx, mod_w, mod_b, norm_mix_g, norm_ffn_g, ffn_w_gu, ffn_w_down, final_norm_g,
           ab_w_in, ab_w_out, ml_ig_b, ml_fg_b, ml_norm_g, dn_conv_w, dn_a_log, dn_dt_bias, dn_norm_g,
           cd_w_in, cd_w_out, rg_conv_w, rg_conv_b, rg_w_r, rg_b_r, rg_w_i, rg_b_i, rg_lambda,
           gla_w_up, gla_b_up, gla_norm_g):
    B, T, D = x.shape
    depth = mod_w.shape[0]
    mlh, dnh = ml_ig_b.shape[-1], dn_a_log.shape[-1]
    rgw = rg_conv_b.shape[-1]
    glh = gla_w_up.shape[-1] // GLA_DK

    cond = jnp.concatenate([c, c_ctx[None, :], jnp.zeros((8 - B - 1, D), F32)], axis=0)

    m_all = adaln_all(cond, mod_w, mod_b)

    xc, xl = ctx, x
    for layer in range(depth):
        last = layer == depth - 1
        j = layer // 2
        m = m_all[layer]
        lat = [t[:, None, :] for t in jnp.split(m[:B], 6, axis=-1)]
        cx = [t[:, None, :] for t in jnp.split(m[B:B + 1], 6, axis=-1)]

        if layer % 2 == 0:
            w_in = ab_w_in[j]
            s = np.cumsum((0, mlh * ML_DK, mlh * ML_DK, mlh * ML_DV, mlh * ML_DV, 2 * mlh, 2 * mlh,
                           dnh * DN_DK, dnh * DN_DK, dnh * DN_DV, dnh * DN_DV, 2 * dnh, 2 * dnh))
            w_main = jnp.concatenate([w_in[:, s[0]:s[4]], w_in[:, s[6]:s[10]]], axis=1)
            w_small = jnp.concatenate([w_in[:, s[4]:s[6]], w_in[:, s[10]:s[12]]], axis=1)
            w_out = ab_w_out[j]
            o = np.cumsum((0, mlh * ML_DK, mlh * ML_DK, mlh * ML_DV, mlh * ML_DV,
                           dnh * DN_DK, dnh * DN_DK, dnh * DN_DV))
        else:
            w_in = cd_w_in[j]
            o = np.cumsum((0, rgw, rgw, glh * GLA_DK, glh * GLA_DK, glh * GLA_DV))
            nmain = int(o[5]) + glh * GLA_DV
            w_main, w_small = w_in[:, :nmain], w_in[:, nmain:]
            w_out = cd_w_out[j]
        o = [int(v) for v in o]
        w_main = w_main.astype(BF16)
        w_small = _pad_cols(w_small, LANES).astype(BF16)
        w_out = w_out.astype(BF16)

        if layer % 2 == 0:
            conv = lambda seg: dict(conv_w=dn_conv_w[j], conv_off=o[4], norm_cols=2 * dnh * DN_DK, seg=seg,
                                    norm_group=DN_DK)
        else:
            conv = lambda seg: {}
        pm_l, ps_l = inproj(xl, norm_mix_g[layer], lat[1], lat[0], w_main, w_small, **conv(GRID_W))
        pm_c, ps_c = inproj(xc, norm_mix_g[layer], cx[1], cx[0], w_main, w_small, **conv(xc.shape[1]))

        if layer % 2 == 0:
            mix_a = mlstm_bidir(pm_c, ps_c, pm_l, ps_l, ml_ig_b[j], ml_fg_b[j], o[0], o[1], o[2])
            mix_b = delta_bidir(pm_c, ps_c, pm_l, ps_l, dn_a_log[j], dn_dt_bias[j], o[4], o[5], o[6], 4 * mlh)
            margs = (0, o[3], o[7], ml_norm_g[j], dn_norm_g[j], ML_DV, DN_DV)
        else:
            mix_a = rglru_bidir(pm_c, pm_l, rg_conv_w[j], rg_conv_b[j], rg_w_r[j], rg_b_r[j], rg_w_i[j],
                                rg_b_i[j], rg_lambda[j])
            mix_b = gla_bidir(pm_c, ps_c, pm_l, ps_l, gla_w_up[j], gla_b_up[j], o[2], o[3], o[4])
            margs = (1, o[1], o[5], jnp.ones((rgw,), F32), gla_norm_g[j], rgw, GLA_DV)

        def merged(which, pm, xres, gate):
            kind, ga_off, gb_off, na, nb, da, db = margs
            a_dirs = (mix_a[0][which], mix_a[1][which])
            b_dirs = (mix_b[0][which], mix_b[1][which])
            return merge_outproj(kind, a_dirs, b_dirs, pm, ga_off, gb_off, na, nb, da, db, xres, gate, w_out)

        w_gu = ffn_w_gu[layer].astype(BF16)
        w_dn = ffn_w_down[layer].astype(BF16)
        xl = merged(1, pm_l, xl, lat[2])
        xl = ffn(xl, norm_ffn_g[layer], lat[4], lat[3], lat[5], w_gu, w_dn, final_norm_g if last else None)
        if not last:
            xc = merged(0, pm_c, xc, cx[2])
            xc = ffn(xc, norm_ffn_g[layer], cx[4], cx[3], cx[5], w_gu, w_dn)
    return xl
```

```python
import functools

import jax
import jax.numpy as jnp
import numpy as np
from jax import lax
from jax.experimental import pallas as pl
from jax.experimental.pallas import tpu as pltpu

F32 = jnp.float32
BF16 = jnp.bfloat16

GRID_W = 64
CHUNK = 64
EPS = 1e-6
ML_DK, ML_DV = 128, 256
DN_DK, DN_DV = 128, 128
GLA_DK, GLA_DV = 128, 256
RG_C = 8.0
GLA_TAU = 16.0

LANES = 128
DELTA_GROUP = 4
GLA_GROUP = 4
ML_GROUP = 4
FFN_SUB = 256
VMEM_LIMIT = 48 * 1024 * 1024

NT = (((1,), (1,)), ((), ()))
TN = (((0,), (0,)), ((), ()))


def _tile(n, pref):
    t = min(n, pref)
    while n % t:
        t //= 2
    return t


def _params(sem, vmem=VMEM_LIMIT):
    return pltpu.CompilerParams(dimension_semantics=sem, vmem_limit_bytes=vmem)


def _dot(a, b, dims=None):
    if dims is None:
        return jnp.dot(a, b, preferred_element_type=F32)
    return lax.dot_general(a, b, dims, preferred_element_type=F32)


def _modulated_norm(x, g, sc, sh):
    y = x * lax.rsqrt(jnp.mean(x * x, axis=-1, keepdims=True) + EPS) * g
    return y * (1.0 + sc) + sh


def _adaln_kernel(s_ref, w_ref, b_ref, o_ref):
    s = s_ref[...]
    s = (s * jax.nn.sigmoid(s)).astype(BF16)
    o_ref[0] = _dot(s, w_ref[0].astype(BF16)) + b_ref[0]


def adaln_all(cond, w_mod, b_mod):
    R, D = cond.shape
    depth, _, C = w_mod.shape
    tn = _tile(C, 1024)
    return pl.pallas_call(
        _adaln_kernel,
        grid=(depth, C // tn),
        in_specs=[pl.BlockSpec((R, D), lambda l, j: (0, 0)),
                  pl.BlockSpec((1, D, tn), lambda l, j: (l, 0, j)),
                  pl.BlockSpec((1, 1, tn), lambda l, j: (l, 0, j))],
        out_specs=pl.BlockSpec((1, R, tn), lambda l, j: (l, 0, j)),
        out_shape=jax.ShapeDtypeStruct((depth, R, C), F32),
        name="adaln_all",
        compiler_params=_params(("parallel", "parallel")),
    )(cond, w_mod, b_mod.reshape(depth, 1, C))


def _inproj_kernel(x_ref, g_ref, sc_ref, sh_ref, w_ref, ws_ref, cw_ref, o_ref, os_ref, h_ref,
                   *, conv_tiles, norm_tiles, seg, norm_group):
    j = pl.program_id(2)

    @pl.when(j == 0)
    def _():
        h = _modulated_norm(x_ref[0], g_ref[...], sc_ref[0], sh_ref[0]).astype(BF16)
        h_ref[...] = h
        os_ref[0] = _dot(h, ws_ref[...])

    o_ref[0] = _dot(h_ref[...], w_ref[...])

    @pl.when((j >= conv_tiles[0]) & (j < conv_tiles[1]))
    def _():
        o_ref[0] = jax.nn.silu(_dwconv(o_ref[0], cw_ref, seg))

    @pl.when((j >= norm_tiles[0]) & (j < norm_tiles[1]))
    def _():
        for lo in range(0, o_ref.shape[-1], norm_group):
            y = o_ref[0, :, lo:lo + norm_group]
            o_ref[0, :, lo:lo + norm_group] = y * lax.rsqrt(jnp.sum(y * y, axis=-1, keepdims=True) + EPS)


def inproj(x, g, sc, sh, w_main, w_small, conv_w=None, conv_off=0, norm_cols=0, seg=1, norm_group=LANES):
    B, T, D = x.shape
    C = w_main.shape[1]
    Cs = w_small.shape[1]
    tm, tn = _tile(T, 1024), _tile(C, 1024)
    if conv_w is None:
        conv_w, conv_tiles, norm_tiles = jnp.zeros((1, tn), F32), (0, 0), (0, 0)
    else:
        assert conv_off % tn == 0 and conv_w.shape[1] % tn == 0 and norm_cols % tn == 0 and tn % norm_group == 0
        assert tm % seg == 0 or (tm == T and seg == T)
        conv_tiles = (conv_off // tn, (conv_off + conv_w.shape[1]) // tn)
        norm_tiles = (conv_off // tn, (conv_off + norm_cols) // tn)
    n_conv = conv_w.shape[1] // tn
    mod = (lambda b, i, j: (b, 0, 0)) if sc.shape[0] == B else (lambda b, i, j: (0, 0, 0))
    return pl.pallas_call(
        functools.partial(_inproj_kernel, conv_tiles=conv_tiles, norm_tiles=norm_tiles, seg=seg,
                          norm_group=norm_group),
        grid=(B, T // tm, C // tn),
        in_specs=[pl.BlockSpec((1, tm, D), lambda b, i, j: (b, i, 0)),
                  pl.BlockSpec((1, D), lambda b, i, j: (0, 0)),
                  pl.BlockSpec((1, 1, D), mod),
                  pl.BlockSpec((1, 1, D), mod),
                  pl.BlockSpec((D, tn), lambda b, i, j: (0, j)),
                  pl.BlockSpec((D, Cs), lambda b, i, j: (0, 0)),
                  pl.BlockSpec((conv_w.shape[0], tn),
                               lambda b, i, j: (0, jnp.clip(j - conv_tiles[0], 0, n_conv - 1)))],
        out_specs=[pl.BlockSpec((1, tm, tn), lambda b, i, j: (b, i, j)),
                   pl.BlockSpec((1, tm, Cs), lambda b, i, j: (b, i, 0))],
        out_shape=[jax.ShapeDtypeStruct((B, T, C), F32), jax.ShapeDtypeStruct((B, T, Cs), F32)],
        scratch_shapes=[pltpu.VMEM((tm, D), BF16)],
        name="inproj",
        compiler_params=_params(("parallel", "parallel", "arbitrary")),
    )(x, g.reshape(1, D), sc, sh, w_main, w_small, conv_w)


def _head_norm(o, d, g):
    parts = []
    for h in range(o.shape[-1] // d):
        oh = o[:, h * d:(h + 1) * d]
        parts.append(oh * lax.rsqrt(jnp.mean(oh * oh, axis=-1, keepdims=True) + EPS))
    return jnp.concatenate(parts, axis=-1) * g


def _merge_kernel(af_ref, ab_ref, bf_ref, bb_ref, ga_ref, gb_ref, na_ref, nb_ref, x_ref, gate_ref, w_ref, o_ref,
                  *, kind, da, db):
    a = af_ref[0] + ab_ref[0]
    b = bf_ref[0] + bb_ref[0]
    if kind == 0:
        ya = _head_norm(a, da, na_ref[...]) * jax.nn.sigmoid(ga_ref[0])
    else:
        ya = jax.nn.gelu(ga_ref[0]) * a
    gb = gb_ref[0]
    yb = _head_norm(b, db, nb_ref[...]) * (gb * jax.nn.sigmoid(gb))
    wa = a.shape[-1]
    y = _dot(ya.astype(BF16), w_ref[0:wa, :]) + _dot(yb.astype(BF16), w_ref[wa:, :])
    o_ref[0] = x_ref[0] + gate_ref[0] * y


def merge_outproj(kind, a_dirs, b_dirs, pm, ga_off, gb_off, na, nb, da, db, x, gate, w):
    B, T, D = x.shape
    wa, wb = a_dirs[0].shape[-1], b_dirs[0].shape[-1]
    assert ga_off % wa == 0 and gb_off % wb == 0
    tm = _tile(T, 256)
    mod = (lambda b, i: (b, 0, 0)) if gate.shape[0] == B else (lambda b, i: (0, 0, 0))
    tok = lambda width, col: pl.BlockSpec((1, tm, width), lambda b, i: (b, i, col))
    return pl.pallas_call(
        functools.partial(_merge_kernel, kind=kind, da=da, db=db),
        grid=(B, T // tm),
        in_specs=[tok(wa, 0), tok(wa, 0), tok(wb, 0), tok(wb, 0), tok(wa, ga_off // wa), tok(wb, gb_off // wb),
                  pl.BlockSpec((1, wa), lambda b, i: (0, 0)),
                  pl.BlockSpec((1, wb), lambda b, i: (0, 0)),
                  tok(D, 0),
                  pl.BlockSpec((1, 1, D), mod),
                  pl.BlockSpec((wa + wb, D), lambda b, i: (0, 0))],
        out_specs=tok(D, 0),
        out_shape=jax.ShapeDtypeStruct((B, T, D), F32),
        name="merge_outproj",
        compiler_params=_params(("parallel", "parallel")),
    )(a_dirs[0], a_dirs[1], b_dirs[0], b_dirs[1], pm, pm, na.reshape(1, wa), nb.reshape(1, wb), x, gate, w)


def _ffn_kernel(x_ref, g_ref, sc_ref, sh_ref, gate_ref, wg_ref, wu_ref, wd_ref, fg_ref, o_ref, h_ref,
                *, final_norm):
    k = pl.program_id(2)

    @pl.when(k == 0)
    def _():
        h_ref[...] = _modulated_norm(x_ref[0], g_ref[...], sc_ref[0], sh_ref[0]).astype(BF16)
        o_ref[...] = jnp.zeros_like(o_ref)

    h = h_ref[...]
    acts = []
    for lo in range(0, wg_ref.shape[1], FFN_SUB):
        a = _dot(h, wg_ref[:, lo:lo + FFN_SUB])
        u = _dot(h, wu_ref[:, lo:lo + FFN_SUB])
        acts.append((a * jax.nn.sigmoid(a) * u).astype(BF16))
    o_ref[0] += _dot(jnp.concatenate(acts, axis=1), wd_ref[...])

    @pl.when(k == pl.num_programs(2) - 1)
    def _():
        r = x_ref[0] + gate_ref[0] * o_ref[0]
        if final_norm:
            r = r * lax.rsqrt(jnp.mean(r * r, axis=-1, keepdims=True) + EPS) * fg_ref[...]
        o_ref[0] = r


def ffn(x, g, sc, sh, gate, w_gu, w_down, final_g=None):
    B, T, D = x.shape
    Fh = w_down.shape[0]
    tm, tf = _tile(T, 512), _tile(Fh, 512)
    nf = Fh // tf
    mod = (lambda b, i, k: (b, 0, 0)) if sc.shape[0] == B else (lambda b, i, k: (0, 0, 0))
    fg = jnp.ones((1, D), F32) if final_g is None else final_g.reshape(1, D)
    return pl.pallas_call(
        functools.partial(_ffn_kernel, final_norm=final_g is not None),
        grid=(B, T // tm, nf),
        in_specs=[pl.BlockSpec((1, tm, D), lambda b, i, k: (b, i, 0)),
                  pl.BlockSpec((1, D), lambda b, i, k: (0, 0)),
                  pl.BlockSpec((1, 1, D), mod),
                  pl.BlockSpec((1, 1, D), mod),
                  pl.BlockSpec((1, 1, D), mod),
                  pl.BlockSpec((D, tf), lambda b, i, k: (0, k)),
                  pl.BlockSpec((D, tf), lambda b, i, k: (0, k + nf)),
                  pl.BlockSpec((tf, D), lambda b, i, k: (k, 0)),
                  pl.BlockSpec((1, D), lambda b, i, k: (0, 0))],
        out_specs=pl.BlockSpec((1, tm, D), lambda b, i, k: (b, i, 0)),
        out_shape=jax.ShapeDtypeStruct((B, T, D), F32),
        scratch_shapes=[pltpu.VMEM((tm, D), BF16)],
        name="ffn",
        compiler_params=_params(("parallel", "parallel", "arbitrary")),
    )(x, g.reshape(1, D), sc, sh, gate, w_gu, w_gu, w_down, fg)


def _incl_mask(rev):
    r = lax.broadcasted_iota(jnp.int32, (CHUNK, CHUNK), 0)
    c = lax.broadcasted_iota(jnp.int32, (CHUNK, CHUNK), 1)
    return (r <= c) if rev else (r >= c)


def _split_bf16(x):
    hi = x.astype(BF16)
    return hi, (x - hi.astype(F32)).astype(BF16)


def _chunk_cumsum(mask_bf16, x):
    hi, lo = _split_bf16(x)
    return _dot(mask_bf16, hi) + _dot(mask_bf16, lo)


def _log_sigmoid(z):
    return jnp.minimum(z, 0.0) - jnp.log1p(jnp.exp(-jnp.abs(z)))


def _softplus(x):
    return jnp.maximum(x, 0.0) + jnp.log1p(jnp.exp(-jnp.abs(x)))


def _time_block(T):
    return _tile(T, 512)


def _dwconv(x, w_ref, seg):
    n = x.shape[0]
    taps = w_ref.shape[0]
    pos = lax.broadcasted_iota(jnp.int32, (n, 1), 0) % seg
    out = None
    for j in range(taps):
        sh = j - taps // 2
        xs = x if sh == 0 else pltpu.roll(x, (-sh) % n, axis=0)
        ok = (pos + sh >= 0) & (pos + sh < seg)
        term = jnp.where(ok, xs, 0.0) * w_ref[j:j + 1, :]
        out = term if out is None else out + term
    return out


def _gla_kernel(q_ref, k_ref, v_ref, ps_ref, wup_ref, bup_ref, s0_ref, o_ref, st_ref, lg_ref, *, rev, heads, nc):
    @pl.when(pl.program_id(1) == 0)
    def _():
        st_ref[...] = s0_ref[...]

    mask = _incl_mask(rev)
    mask_bf = jnp.where(mask, 1.0, 0.0).astype(BF16)
    z = _dot(ps_ref[0].astype(BF16), wup_ref[...]) + bup_ref[...]
    lg_ref[...] = _log_sigmoid(z) * (1.0 / GLA_TAU)

    hs = range(heads)
    ks = [slice(h * GLA_DK, (h + 1) * GLA_DK) for h in hs]
    vs = [slice(h * GLA_DV, (h + 1) * GLA_DV) for h in hs]

    def prepare(rows):
        lg = lg_ref[rows, :]
        b = _chunk_cumsum(mask_bf, lg)
        b_end = jnp.sum(lg, axis=0, keepdims=True)
        k = k_ref[0, rows, :]
        qe = (q_ref[0, rows, :] * (GLA_DK ** -0.5) * jnp.exp(b)).astype(BF16)
        ke = (k * jnp.exp(-b)).astype(BF16)
        kd = (k * jnp.exp(b_end - b)).astype(BF16)
        v = [v_ref[0, rows, vs[h]].astype(BF16) for h in hs]
        att = [_dot(qe[:, ks[h]], ke[:, ks[h]], NT) for h in hs]
        upd = [_dot(v[h], kd[:, ks[h]], TN) for h in hs]
        intra = [_dot(jnp.where(mask, att[h], 0.0).astype(BF16), v[h]) for h in hs]
        return qe, intra, upd, jnp.exp(b_end)

    def body(gi, carry):
        rows_l = []
        for j in range(GLA_GROUP):
            ci = gi * GLA_GROUP + j
            c = (nc - 1 - ci) if rev else ci
            rows_l.append(pl.ds(pl.multiple_of(c * CHUNK, CHUNK), CHUNK))
        preps = [prepare(rows) for rows in rows_l]
        sts = [st_ref[0, h] for h in hs]
        for rows, (qe, intra, upd, e_end) in zip(rows_l, preps):
            inter = [_dot(qe[:, ks[h]], sts[h].astype(BF16), NT) for h in hs]
            for h in hs:
                o_ref[0, rows, vs[h]] = intra[h] + inter[h]
            sts = [sts[h] * e_end[:, ks[h]] + upd[h] for h in hs]
        for h in hs:
            st_ref[0, h] = sts[h]
        return carry

    assert nc % GLA_GROUP == 0
    lax.fori_loop(0, nc // GLA_GROUP, body, 0)


def gla_scan(pm, ps, wup, bup, s0, rev, q_off, k_off, v_off, heads):
    B, T, _ = pm.shape
    Tb = _time_block(T)
    NB = T // Tb
    kw, vw = heads * GLA_DK, heads * GLA_DV
    assert q_off % kw == 0 and k_off % kw == 0 and v_off % vw == 0
    blk = (lambda i: NB - 1 - i) if rev else (lambda i: i)
    return pl.pallas_call(
        functools.partial(_gla_kernel, rev=rev, heads=heads, nc=Tb // CHUNK),
        grid=(B, NB),
        in_specs=[pl.BlockSpec((1, Tb, kw), lambda b, i: (b, blk(i), q_off // kw)),
                  pl.BlockSpec((1, Tb, kw), lambda b, i: (b, blk(i), k_off // kw)),
                  pl.BlockSpec((1, Tb, vw), lambda b, i: (b, blk(i), v_off // vw)),
                  pl.BlockSpec((1, Tb, ps.shape[-1]), lambda b, i: (b, blk(i), 0)),
                  pl.BlockSpec(wup.shape, lambda b, i: (0, 0)),
                  pl.BlockSpec(bup.shape, lambda b, i: (0, 0)),
                  pl.BlockSpec((1,) + s0.shape[1:], lambda b, i: (b, 0, 0, 0))],
        out_specs=[pl.BlockSpec((1, Tb, vw), lambda b, i: (b, blk(i), 0)),
                   pl.BlockSpec((1,) + s0.shape[1:], lambda b, i: (b, 0, 0, 0))],
        out_shape=[jax.ShapeDtypeStruct((B, T, vw), F32), jax.ShapeDtypeStruct(s0.shape, F32)],
        scratch_shapes=[pltpu.VMEM((Tb, kw), F32)],
        name="gla_scan",
        compiler_params=_params(("parallel", "arbitrary")),
    )(pm, pm, pm, ps, wup, bup, s0)


def gla_bidir(pm_c, ps_c, pm_l, ps_l, w_up, b_up, q_off, k_off, v_off):
    B = pm_l.shape[0]
    rank, hk = w_up.shape[1], w_up.shape[2]
    heads = hk // GLA_DK
    out = []
    for d in (0, 1):
        wup = jnp.zeros((ps_l.shape[-1], hk), F32).at[d * rank:(d + 1) * rank].set(w_up[d]).astype(BF16)
        bup = b_up[d].reshape(1, hk)
        s0 = jnp.zeros((B, heads, GLA_DV, GLA_DK), F32)
        oc, s1 = gla_scan(pm_c, ps_c, wup, bup, s0, bool(d), q_off, k_off, v_off, heads)
        ol, _ = gla_scan(pm_l, ps_l, wup, bup, s1, bool(d), q_off, k_off, v_off, heads)
        out.append((oc, ol))
    return out


def _mlstm_kernel(q_ref, k_ref, v_ref, ps_ref, gb_ref, c0_ref, nm0_ref, o_ref, c_ref, nm_ref,
                  *, rev, heads, nc, ig_col, fg_col):
    @pl.when(pl.program_id(1) == 0)
    def _():
        c_ref[...] = c0_ref[...]
        nm_ref[...] = nm0_ref[...]

    mask = _incl_mask(rev)
    mask_bf = jnp.where(mask, 1.0, 0.0).astype(BF16)

    hs = range(heads)
    cols = [(ig_col + h, fg_col + h) for h in hs]

    def prepare(rows):
        g = ps_ref[0, rows, :] + gb_ref[...]
        lf = _log_sigmoid(g)
        bcum = _chunk_cumsum(mask_bf, lf)
        b_end = jnp.sum(lf, axis=0, keepdims=True)
        bcum_t, g_t = bcum.T, g.T
        b_col = [bcum[:, fc:fc + 1] for _, fc in cols]
        b_l = [b_end[:, fc:fc + 1] for _, fc in cols]
        q = [q_ref[0, rows, h * ML_DK:(h + 1) * ML_DK] * (ML_DK ** -0.5) for h in hs]
        k = [k_ref[0, rows, h * ML_DK:(h + 1) * ML_DK] for h in hs]
        v = [v_ref[0, rows, h * ML_DV:(h + 1) * ML_DV] for h in hs]
        qb = [t.astype(BF16) for t in q]
        kb = [t.astype(BF16) for t in k]
        qk = [_dot(qb[h], kb[h], NT) for h in hs]
        d_log = [jnp.where(mask, b_col[h] - bcum_t[fc:fc + 1, :] + g_t[ic:ic + 1, :], -jnp.inf)
                 for h, (ic, fc) in enumerate(cols)]
        m_intra = [jnp.max(t, axis=-1, keepdims=True) for t in d_log]
        s = [qk[h] * jnp.exp(d_log[h] - m_intra[h]) for h in hs]
        num = [_dot(s[h].astype(BF16), v[h].astype(BF16)) for h in hs]
        den = [jnp.sum(t, axis=-1, keepdims=True) for t in s]
        w_log = [b_l[h] - b_col[h] + g[:, ic:ic + 1] for h, (ic, _) in enumerate(cols)]
        m_loc = [jnp.max(t, axis=0, keepdims=True) for t in w_log]
        w = [jnp.exp(w_log[h] - m_loc[h]) for h in hs]
        c_inc = [_dot((w[h] * v[h]).astype(BF16), kb[h], TN) for h in hs]
        n_inc = [jnp.sum(w[h] * k[h], axis=0, keepdims=True) for h in hs]
        return q, qb, b_col, b_l, m_intra, num, den, m_loc, c_inc, n_inc

    def body(gi, carry):
        rows_l = []
        for j in range(ML_GROUP):
            ci = gi * ML_GROUP + j
            c = (nc - 1 - ci) if rev else ci
            rows_l.append(pl.ds(pl.multiple_of(c * CHUNK, CHUNK), CHUNK))
        preps = [prepare(rows) for rows in rows_l]
        cst = [c_ref[0, h] for h in hs]
        n = [nm_ref[0, h, 0:1, :] for h in hs]
        m = [nm_ref[0, h, 1:2, 0:1] for h in hs]
        for rows, (q, qb, b_col, b_l, m_intra, num, den, m_loc, c_inc, n_inc) in zip(rows_l, preps):
            q_c = [_dot(qb[h], cst[h].astype(BF16), NT) for h in hs]
            q_n = [jnp.sum(q[h] * n[h], axis=-1, keepdims=True) for h in hs]
            inter_log = [b_col[h] + m[h] for h in hs]
            m_out = [jnp.maximum(m_intra[h], inter_log[h]) for h in hs]
            e_intra = [jnp.exp(m_intra[h] - m_out[h]) for h in hs]
            e_inter = [jnp.exp(inter_log[h] - m_out[h]) for h in hs]
            top = [e_intra[h] * num[h] + e_inter[h] * q_c[h] for h in hs]
            bot = [e_intra[h] * den[h] + e_inter[h] * q_n[h] for h in hs]
            for h in hs:
                o_ref[0, rows, h * ML_DV:(h + 1) * ML_DV] = (
                    top[h] / jnp.maximum(jnp.abs(bot[h]), jnp.exp(-m_out[h])))
            m_new = [jnp.maximum(b_l[h] + m[h], m_loc[h]) for h in hs]
            a = [jnp.exp(b_l[h] + m[h] - m_new[h]) for h in hs]
            sc = [jnp.exp(m_loc[h] - m_new[h]) for h in hs]
            cst = [a[h] * cst[h] + sc[h] * c_inc[h] for h in hs]
            n = [a[h] * n[h] + sc[h] * n_inc[h] for h in hs]
            m = m_new
        for h in hs:
            c_ref[0, h] = cst[h]
            nm_ref[0, h, 0:1, :] = n[h]
            nm_ref[0, h, 1:2, :] = jnp.broadcast_to(m[h], (1, ML_DK))
        return carry

    assert nc % ML_GROUP == 0
    lax.fori_loop(0, nc // ML_GROUP, body, 0)


def mlstm_scan(pm, ps, gbias, state, rev, q_off, k_off, v_off, heads, ig_col, fg_col):
    B, T, _ = pm.shape
    Tb = _time_block(T)
    NB = T // Tb
    kw, vw = heads * ML_DK, heads * ML_DV
    assert q_off % kw == 0 and k_off % kw == 0 and v_off % vw == 0
    c0, nm0 = state
    blk = (lambda i: NB - 1 - i) if rev else (lambda i: i)
    st_spec = lambda a: pl.BlockSpec((1,) + a.shape[1:], lambda b, i: (b, 0, 0, 0))
    o, c1, nm1 = pl.pallas_call(
        functools.partial(_mlstm_kernel, rev=rev, heads=heads, nc=Tb // CHUNK, ig_col=ig_col, fg_col=fg_col),
        grid=(B, NB),
        in_specs=[pl.BlockSpec((1, Tb, kw), lambda b, i: (b, blk(i), q_off // kw)),
                  pl.BlockSpec((1, Tb, kw), lambda b, i: (b, blk(i), k_off // kw)),
                  pl.BlockSpec((1, Tb, vw), lambda b, i: (b, blk(i), v_off // vw)),
                  pl.BlockSpec((1, Tb, ps.shape[-1]), lambda b, i: (b, blk(i), 0)),
                  pl.BlockSpec(gbias.shape, lambda b, i: (0, 0)),
                  st_spec(c0), st_spec(nm0)],
        out_specs=[pl.BlockSpec((1, Tb, vw), lambda b, i: (b, blk(i), 0)), st_spec(c0), st_spec(nm0)],
        out_shape=[jax.ShapeDtypeStruct((B, T, vw), F32), jax.ShapeDtypeStruct(c0.shape, F32),
                   jax.ShapeDtypeStruct(nm0.shape, F32)],
        name="mlstm_scan",
        compiler_params=_params(("parallel", "arbitrary")),
    )(pm, pm, pm, ps, gbias, c0, nm0)
    return o, (c1, nm1)


def mlstm_bidir(pm_c, ps_c, pm_l, ps_l, ig_b, fg_b, q_off, k_off, v_off):
    B = pm_l.shape[0]
    heads = ig_b.shape[-1]
    ncol = ps_l.shape[-1]
    gbias = jnp.zeros((1, ncol), F32).at[0, :4 * heads].set(jnp.concatenate([ig_b.reshape(-1), fg_b.reshape(-1)]))
    out = []
    for d in (0, 1):
        st = (jnp.zeros((B, heads, ML_DV, ML_DK), F32), jnp.zeros((B, heads, 8, ML_DK), F32))
        args = (bool(d), q_off, k_off, v_off, heads, d * heads, 2 * heads + d * heads)
        oc, st = mlstm_scan(pm_c, ps_c, gbias, st, *args)
        ol, _ = mlstm_scan(pm_l, ps_l, gbias, st, *args)
        out.append((oc, ol))
    return out


def _block_join_masks():
    r = lax.broadcasted_iota(jnp.int32, (CHUNK, CHUNK), 0)
    c = lax.broadcasted_iota(jnp.int32, (CHUNK, CHUNK), 1)
    masks, s = [], 1
    while s < CHUNK:
        masks.append(jnp.where((r // (2 * s) == c // (2 * s)) & (r // s != c // s), 1.0, 0.0))
        s *= 2
    return masks


def _unit_tri_inverses(a_list, join_masks):
    r = lax.broadcasted_iota(jnp.int32, (CHUNK, CHUNK), 0)
    c = lax.broadcasted_iota(jnp.int32, (CHUNK, CHUNK), 1)
    eye = jnp.where(r == c, 1.0, 0.0)
    ts = [eye - a * join_masks[0] for a in a_list]
    for m in join_masks[1:]:
        tb = [t.astype(BF16) for t in ts]
        xs = [_dot(t, (a * m).astype(BF16)) for t, a in zip(tb, a_list)]
        ts = [t - _dot(x.astype(BF16), t_b) for t, x, t_b in zip(ts, xs, tb)]
    return ts


def _delta_kernel(q_ref, k_ref, v_ref, ps_ref, gp_ref, s0_ref, o_ref, st_ref, *, rev, heads, nc, g_col, beta_col):
    @pl.when(pl.program_id(1) == 0)
    def _():
        st_ref[...] = s0_ref[...]

    mask = _incl_mask(rev)
    mask_bf = jnp.where(mask, 1.0, 0.0).astype(BF16)
    r = lax.broadcasted_iota(jnp.int32, (CHUNK, CHUNK), 0)
    c = lax.broadcasted_iota(jnp.int32, (CHUNK, CHUNK), 1)
    off_diag = jnp.where(r != c, 1.0, 0.0)
    join_masks = _block_join_masks()
    L = CHUNK

    def prepare(rows):
        pre = ps_ref[0, rows, :]
        g = gp_ref[1:2, :] * _softplus(pre + gp_ref[0:1, :])
        beta = jax.nn.sigmoid(pre)
        gcum = _chunk_cumsum(mask_bf, g)
        g_end = jnp.sum(g, axis=0, keepdims=True)
        gcum_t = gcum.T
        e_cum, e_rest, e_end = jnp.exp(gcum), jnp.exp(g_end - gcum), jnp.exp(g_end)
        a_l, rhs_l, qe_l, kd_l, qk_l, end_l = [], [], [], [], [], []
        for h in range(heads):
            gc, bc = g_col + h, beta_col + h
            ks = slice(h * DN_DK, (h + 1) * DN_DK)
            q, k = q_ref[0, rows, ks] * (DN_DK ** -0.5), k_ref[0, rows, ks]
            b_c = beta[:, bc:bc + 1]
            decay = jnp.exp(jnp.where(mask, gcum[:, gc:gc + 1] - gcum_t[gc:gc + 1, :], -jnp.inf))
            kbeta = k * b_c
            both = _dot(jnp.concatenate([kbeta, q], axis=0).astype(BF16), k.astype(BF16), NT)
            a_l.append(both[:L] * decay * off_diag)
            qk_l.append((both[L:] * decay).astype(BF16))
            vbeta = v_ref[0, rows, h * DN_DV:(h + 1) * DN_DV] * b_c
            rhs_l.append(jnp.concatenate([vbeta, kbeta * e_cum[:, gc:gc + 1]], axis=1).astype(BF16))
            qe_l.append((q * e_cum[:, gc:gc + 1]).astype(BF16))
            kd_l.append((k * e_rest[:, gc:gc + 1]).astype(BF16))
            end_l.append(e_end[:, gc:gc + 1])
        return a_l, rhs_l, qe_l, kd_l, qk_l, end_l

    def body(gi, carry):
        rows_l = []
        for j in range(DELTA_GROUP):
            ci = gi * DELTA_GROUP + j
            cidx = (nc - 1 - ci) if rev else ci
            rows_l.append(pl.ds(pl.multiple_of(cidx * CHUNK, CHUNK), CHUNK))
        preps = [prepare(rows) for rows in rows_l]
        t_l = _unit_tri_inverses([a for p in preps for a in p[0]], join_masks)
        uw_l = [_dot(t.astype(BF16), rhs) for t, rhs in zip(t_l, [r for p in preps for r in p[1]])]
        hs = range(heads)
        sts = [st_ref[0, h] for h in hs]
        for j, (rows, (_, _, qe_l, kd_l, qk_l, end_l)) in enumerate(zip(rows_l, preps)):
            uws = uw_l[j * heads:(j + 1) * heads]
            lhs = [jnp.concatenate([uws[h][:, DN_DV:].astype(BF16), qe_l[h]], axis=0) for h in hs]
            ws = [_dot(lhs[h], sts[h].astype(BF16)) for h in hs]
            v_new = [(uws[h][:, :DN_DV] - ws[h][:L]).astype(BF16) for h in hs]
            outs = [ws[h][L:] + _dot(qk_l[h], v_new[h]) for h in hs]
            incs = [_dot(kd_l[h], v_new[h], TN) for h in hs]
            for h in hs:
                o_ref[0, rows, h * DN_DV:(h + 1) * DN_DV] = outs[h]
            sts = [end_l[h] * sts[h] + incs[h] for h in hs]
        for h in hs:
            st_ref[0, h] = sts[h]
        return carry

    assert nc % DELTA_GROUP == 0
    lax.fori_loop(0, nc // DELTA_GROUP, body, 0)


def delta_scan(pm, ps, gparams, s0, rev, q_off, k_off, v_off, heads, g_col, beta_col):
    B, T, _ = pm.shape
    Tb = _time_block(T)
    NB = T // Tb
    kw, vw = heads * DN_DK, heads * DN_DV
    assert q_off % kw == 0 and k_off % kw == 0 and v_off % vw == 0
    blk = (lambda i: NB - 1 - i) if rev else (lambda i: i)
    st_spec = pl.BlockSpec((1,) + s0.shape[1:], lambda b, i: (b, 0, 0, 0))
    return pl.pallas_call(
        functools.partial(_delta_kernel, rev=rev, heads=heads, nc=Tb // CHUNK, g_col=g_col, beta_col=beta_col),
        grid=(B, NB),
        in_specs=[pl.BlockSpec((1, Tb, kw), lambda b, i: (b, blk(i), q_off // kw)),
                  pl.BlockSpec((1, Tb, kw), lambda b, i: (b, blk(i), k_off // kw)),
                  pl.BlockSpec((1, Tb, vw), lambda b, i: (b, blk(i), v_off // vw)),
                  pl.BlockSpec((1, Tb, ps.shape[-1]), lambda b, i: (b, blk(i), 0)),
                  pl.BlockSpec(gparams.shape, lambda b, i: (0, 0)),
                  st_spec],
        out_specs=[pl.BlockSpec((1, Tb, vw), lambda b, i: (b, blk(i), 0)), st_spec],
        out_shape=[jax.ShapeDtypeStruct((B, T, vw), F32), jax.ShapeDtypeStruct(s0.shape, F32)],
        name="delta_scan",
        compiler_params=_params(("parallel", "arbitrary")),
    )(pm, pm, pm, ps, gparams, s0)


def delta_bidir(pm_c, ps_c, pm_l, ps_l, a_log, dt_bias, q_off, k_off, v_off, col0):
    B = pm_l.shape[0]
    heads = a_log.shape[-1]
    ncol = ps_l.shape[-1]
    gparams = jnp.zeros((8, ncol), F32)
    gparams = gparams.at[0, col0:col0 + 2 * heads].set(dt_bias.reshape(-1))
    gparams = gparams.at[1, col0:col0 + 2 * heads].set(-jnp.exp(a_log).reshape(-1))
    out = []
    for d in (0, 1):
        s0 = jnp.zeros((B, heads, DN_DK, DN_DV), F32)
        args = (q_off, k_off, v_off, heads, col0 + d * heads, col0 + 2 * heads + d * heads)
        oc, s1 = delta_scan(pm_c, ps_c, gparams, s0, bool(d), *args)
        ol, _ = delta_scan(pm_l, ps_l, gparams, s1, bool(d), *args)
        out.append((oc, ol))
    return out


def _rglru_kernel(x_ref, cw_ref, cb_ref, wr_ref, br_ref, wi_ref, bi_ref, lam_ref, h0_ref, o_ref, hT_ref,
                  *, rev, nc, seg):
    @pl.when(pl.program_id(2) == 0)
    def _():
        hT_ref[...] = h0_ref[...]

    n = x_ref.shape[1]
    xc = _dwconv(x_ref[0], cw_ref, seg) + cb_ref[...]
    xb = xc.astype(BF16)
    r = jax.nn.sigmoid(_dot(xb, wr_ref[0].astype(BF16)) + br_ref[...])
    i = jax.nn.sigmoid(_dot(xb, wi_ref[0].astype(BF16)) + bi_ref[...])
    log_a = -RG_C * r * _softplus(-lam_ref[...])
    a = jnp.exp(log_a)
    bx = jnp.sqrt(1.0 - jnp.exp(2.0 * log_a)) * (i * xc)

    pos = lax.broadcasted_iota(jnp.int32, (n, 1), 0) % CHUNK
    s = 1
    while s < CHUNK:
        ok = (pos < CHUNK - s) if rev else (pos >= s)
        shift = (n - s) if rev else s
        a_sh = pltpu.roll(a, shift, axis=0)
        b_sh = pltpu.roll(bx, shift, axis=0)
        bx = jnp.where(ok, a * b_sh + bx, bx)
        a = jnp.where(ok, a * a_sh, a)
        s *= 2

    h = hT_ref[0]
    for ci in range(nc):
        c = (nc - 1 - ci) if rev else ci
        rows = slice(c * CHUNK, (c + 1) * CHUNK)
        hc = bx[rows] + a[rows] * h
        o_ref[0, rows, :] = hc
        h = hc[0:1] if rev else hc[CHUNK - 1:CHUNK]
    hT_ref[0] = h


def rglru_scan(pm, conv_w, conv_b, w_r, b_r, w_i, b_i, lam, h0, rev, seg, width):
    B, T, _ = pm.shape
    nblk, bw = w_r.shape[0], w_r.shape[1]
    assert bw % LANES == 0 and nblk * bw == width
    Tb = _tile(T, 1024)
    NB = T // Tb
    assert Tb % seg == 0 or (NB == 1 and seg == T)
    blk = (lambda i: NB - 1 - i) if rev else (lambda i: i)
    row = pl.BlockSpec((1, bw), lambda b, g, i: (0, g))
    return pl.pallas_call(
        functools.partial(_rglru_kernel, rev=rev, nc=Tb // CHUNK, seg=seg),
        grid=(B, nblk, NB),
        in_specs=[pl.BlockSpec((1, Tb, bw), lambda b, g, i: (b, blk(i), g)),
                  pl.BlockSpec((conv_w.shape[0], bw), lambda b, g, i: (0, g)),
                  row,
                  pl.BlockSpec((1, bw, bw), lambda b, g, i: (g, 0, 0)), row,
                  pl.BlockSpec((1, bw, bw), lambda b, g, i: (g, 0, 0)), row,
                  row,
                  pl.BlockSpec((1, 1, bw), lambda b, g, i: (b, 0, g))],
        out_specs=[pl.BlockSpec((1, Tb, bw), lambda b, g, i: (b, blk(i), g)),
                   pl.BlockSpec((1, 1, bw), lambda b, g, i: (b, 0, g))],
        out_shape=[jax.ShapeDtypeStruct((B, T, width), F32), jax.ShapeDtypeStruct(h0.shape, F32)],
        name="rglru_scan",
        compiler_params=_params(("parallel", "parallel", "arbitrary")),
    )(pm, conv_w, conv_b, w_r, b_r, w_i, b_i, lam, h0)


def rglru_bidir(pm_c, pm_l, conv_w, conv_b, w_r, b_r, w_i, b_i, lam):
    B = pm_l.shape[0]
    width = conv_b.shape[-1]
    out = []
    for d in (0, 1):
        h0 = jnp.zeros((B, 1, width), F32)
        args = (conv_w, conv_b.reshape(1, width), w_r[d], b_r[d].reshape(1, width), w_i[d],
                b_i[d].reshape(1, width), lam[d].reshape(1, width))
        oc, h1 = rglru_scan(pm_c, *args, h0, bool(d), pm_c.shape[1], width)
        ol, _ = rglru_scan(pm_l, *args, h1, bool(d), GRID_W, width)
        out.append((oc, ol))
    return out


def _pad_cols(w, mult):
    return jnp.pad(w, ((0, 0), (0, (-w.shape[-1]) % mult)))


def kernel(x, c, ctx, c_ctx, mod_w, mod_b, norm_mix_g, norm_ffn_g, ffn_w_gu, ffn_w_down, final_norm_g,
           ab_w_in, ab_w_out, ml_ig_b, ml_fg_b, ml_norm_g, dn_conv_w, dn_a_log, dn_dt_bias, dn_norm_g,
           cd_w_in, cd_w_out, rg_conv_w, rg_conv_b, rg_w_r, rg_b_r, rg_w_i, rg_b_i, rg_lambda,
           gla_w_up, gla_b_up, gla_norm_g):
    B, T, D = x.shape
    depth = mod_w.shape[0]
    mlh, dnh = ml_ig_b.shape[-1], dn_a_log.shape[-1]
    rgw = rg_conv_b.shape[-1]
    glh = gla_w_up.shape[-1] // GLA_DK

    cond = jnp.concatenate([c, c_ctx[None, :], jnp.zeros((8 - B - 1, D), F32)], axis=0)

    m_all = adaln_all(cond, mod_w, mod_b)

    xc, xl = ctx, x
    for layer in range(depth):
        last = layer == depth - 1
        j = layer // 2
        m = m_all[layer]
        lat = [t[:, None, :] for t in jnp.split(m[:B], 6, axis=-1)]
        cx = [t[:, None, :] for t in jnp.split(m[B:B + 1], 6, axis=-1)]

        if layer % 2 == 0:
            w_in = ab_w_in[j]
            s = np.cumsum((0, mlh * ML_DK, mlh * ML_DK, mlh * ML_DV, mlh * ML_DV, 2 * mlh, 2 * mlh,
                           dnh * DN_DK, dnh * DN_DK, dnh * DN_DV, dnh * DN_DV, 2 * dnh, 2 * dnh))
            w_main = jnp.concatenate([w_in[:, s[0]:s[4]], w_in[:, s[6]:s[10]]], axis=1)
            w_small = jnp.concatenate([w_in[:, s[4]:s[6]], w_in[:, s[10]:s[12]]], axis=1)
            w_out = ab_w_out[j]
            o = np.cumsum((0, mlh * ML_DK, mlh * ML_DK, mlh * ML_DV, mlh * ML_DV,
                           dnh * DN_DK, dnh * DN_DK, dnh * DN_DV))
        else:
            w_in = cd_w_in[j]
            o = np.cumsum((0, rgw, rgw, glh * GLA_DK, glh * GLA_DK, glh * GLA_DV))
            nmain = int(o[5]) + glh * GLA_DV
            w_main, w_small = w_in[:, :nmain], w_in[:, nmain:]
            w_out = cd_w_out[j]
        o = [int(v) for v in o]
        w_main = w_main.astype(BF16)
        w_small = _pad_cols(w_small, LANES).astype(BF16)
        w_out = w_out.astype(BF16)

        if layer % 2 == 0:
            conv = lambda seg: dict(conv_w=dn_conv_w[j], conv_off=o[4], norm_cols=2 * dnh * DN_DK, seg=seg,
                                    norm_group=DN_DK)
        else:
            conv = lambda seg: {}
        pm_l, ps_l = inproj(xl, norm_mix_g[layer], lat[1], lat[0], w_main, w_small, **conv(GRID_W))
        pm_c, ps_c = inproj(xc, norm_mix_g[layer], cx[1], cx[0], w_main, w_small, **conv(xc.shape[1]))

        if layer % 2 == 0:
            mix_a = mlstm_bidir(pm_c, ps_c, pm_l, ps_l, ml_ig_b[j], ml_fg_b[j], o[0], o[1], o[2])
            mix_b = delta_bidir(pm_c, ps_c, pm_l, ps_l, dn_a_log[j], dn_dt_bias[j], o[4], o[5], o[6], 4 * mlh)
            margs = (0, o[3], o[7], ml_norm_g[j], dn_norm_g[j], ML_DV, DN_DV)
        else:
            mix_a = rglru_bidir(pm_c, pm_l, rg_conv_w[j], rg_conv_b[j], rg_w_r[j], rg_b_r[j], rg_w_i[j],
                                rg_b_i[j], rg_lambda[j])
            mix_b = gla_bidir(pm_c, ps_c, pm_l, ps_l, gla_w_up[j], gla_b_up[j], o[2], o[3], o[4])
            margs = (1, o[1], o[5], jnp.ones((rgw,), F32), gla_norm_g[j], rgw, GLA_DV)

        def merged(which, pm, xres, gate):
            kind, ga_off, gb_off, na, nb, da, db = margs
            a_dirs = (mix_a[0][which], mix_a[1][which])
            b_dirs = (mix_b[0][which], mix_b[1][which])
            return merge_outproj(kind, a_dirs, b_dirs, pm, ga_off, gb_off, na, nb, da, db, xres, gate, w_out)

        w_gu = ffn_w_gu[layer].astype(BF16)
        w_dn = ffn_w_down[layer].astype(BF16)
        xl = merged(1, pm_l, xl, lat[2])
        xl = ffn(xl, norm_ffn_g[layer], lat[4], lat[3], lat[5], w_gu, w_dn, final_norm_g if last else None)
        if not last:
            xc = merged(0, pm_c, xc, cx[2])
            xc = ffn(xc, norm_ffn_g[layer], cx[4], cx[3], cx[5], w_gu, w_dn)
    return xl
```

```python
import functools

import jax
import jax.numpy as jnp
import numpy as np
from jax import lax
from jax.experimental import pallas as pl
from jax.experimental.pallas import tpu as pltpu

F32 = jnp.float32
BF16 = jnp.bfloat16

GRID_W = 64
CHUNK = 64
EPS = 1e-6
ML_DK, ML_DV = 128, 256
DN_DK, DN_DV = 128, 128
GLA_DK, GLA_DV = 128, 256
RG_C = 8.0
GLA_TAU = 16.0

LANES = 128
DELTA_GROUP = 4
GLA_GROUP = 4
ML_GROUP = 4
FFN_SUB = 256
VMEM_LIMIT = 48 * 1024 * 1024

NT = (((1,), (1,)), ((), ()))
TN = (((0,), (0,)), ((), ()))


def _tile(n, pref):
    t = min(n, pref)
    while n % t:
        t //= 2
    return t


def _params(sem, vmem=VMEM_LIMIT):
    return pltpu.CompilerParams(dimension_semantics=sem, vmem_limit_bytes=vmem)


def _dot(a, b, dims=None):
    if dims is None:
        return jnp.dot(a, b, preferred_element_type=F32)
    return lax.dot_general(a, b, dims, preferred_element_type=F32)


def _modulated_norm(x, g, sc, sh):
    y = x * lax.rsqrt(jnp.mean(x * x, axis=-1, keepdims=True) + EPS) * g
    return y * (1.0 + sc) + sh


def _adaln_kernel(s_ref, w_ref, b_ref, o_ref):
    s = s_ref[...]
    s = (s * jax.nn.sigmoid(s)).astype(BF16)
    o_ref[0] = _dot(s, w_ref[0].astype(BF16)) + b_ref[0]


def adaln_all(cond, w_mod, b_mod):
    R, D = cond.shape
    depth, _, C = w_mod.shape
    tn = _tile(C, 1024)
    return pl.pallas_call(
        _adaln_kernel,
        grid=(depth, C // tn),
        in_specs=[pl.BlockSpec((R, D), lambda l, j: (0, 0)),
                  pl.BlockSpec((1, D, tn), lambda l, j: (l, 0, j)),
                  pl.BlockSpec((1, 1, tn), lambda l, j: (l, 0, j))],
        out_specs=pl.BlockSpec((1, R, tn), lambda l, j: (l, 0, j)),
        out_shape=jax.ShapeDtypeStruct((depth, R, C), F32),
        name="adaln_all",
        compiler_params=_params(("parallel", "parallel")),
    )(cond, w_mod, b_mod.reshape(depth, 1, C))


def _inproj_kernel(x_ref, g_ref, sc_ref, sh_ref, w_ref, ws_ref, cw_ref, o_ref, os_ref, h_ref,
                   *, conv_tiles, norm_tiles, seg, norm_group):
    j = pl.program_id(2)

    @pl.when(j == 0)
    def _():
        h = _modulated_norm(x_ref[0], g_ref[...], sc_ref[0], sh_ref[0]).astype(BF16)
        h_ref[...] = h
        os_ref[0] = _dot(h, ws_ref[...])

    o_ref[0] = _dot(h_ref[...], w_ref[...])

    @pl.when((j >= conv_tiles[0]) & (j < conv_tiles[1]))
    def _():
        o_ref[0] = jax.nn.silu(_dwconv(o_ref[0], cw_ref, seg))

    @pl.when((j >= norm_tiles[0]) & (j < norm_tiles[1]))
    def _():
        for lo in range(0, o_ref.shape[-1], norm_group):
            y = o_ref[0, :, lo:lo + norm_group]
            o_ref[0, :, lo:lo + norm_group] = y * lax.rsqrt(jnp.sum(y * y, axis=-1, keepdims=True) + EPS)


def inproj(x, g, sc, sh, w_main, w_small, conv_w=None, conv_off=0, norm_cols=0, seg=1, norm_group=LANES):
    B, T, D = x.shape
    C = w_main.shape[1]
    Cs = w_small.shape[1]
    tm, tn = _tile(T, 1024), _tile(C, 1024)
    if conv_w is None:
        conv_w, conv_tiles, norm_tiles = jnp.zeros((1, tn), F32), (0, 0), (0, 0)
    else:
        assert conv_off % tn == 0 and conv_w.shape[1] % tn == 0 and norm_cols % tn == 0 and tn % norm_group == 0
        assert tm % seg == 0 or (tm == T and seg == T)
        conv_tiles = (conv_off // tn, (conv_off + conv_w.shape[1]) // tn)
        norm_tiles = (conv_off // tn, (conv_off + norm_cols) // tn)
    n_conv = conv_w.shape[1] // tn
    mod = (lambda b, i, j: (b, 0, 0)) if sc.shape[0] == B else (lambda b, i, j: (0, 0, 0))
    return pl.pallas_call(
        functools.partial(_inproj_kernel, conv_tiles=conv_tiles, norm_tiles=norm_tiles, seg=seg,
                          norm_group=norm_group),
        grid=(B, T // tm, C // tn),
        in_specs=[pl.BlockSpec((1, tm, D), lambda b, i, j: (b, i, 0)),
                  pl.BlockSpec((1, D), lambda b, i, j: (0, 0)),
                  pl.BlockSpec((1, 1, D), mod),
                  pl.BlockSpec((1, 1, D), mod),
                  pl.BlockSpec((D, tn), lambda b, i, j: (0, j)),
                  pl.BlockSpec((D, Cs), lambda b, i, j: (0, 0)),
                  pl.BlockSpec((conv_w.shape[0], tn),
                               lambda b, i, j: (0, jnp.clip(j - conv_tiles[0], 0, n_conv - 1)))],
        out_specs=[pl.BlockSpec((1, tm, tn), lambda b, i, j: (b, i, j)),
                   pl.BlockSpec((1, tm, Cs), lambda b, i, j: (b, i, 0))],
        out_shape=[jax.ShapeDtypeStruct((B, T, C), F32), jax.ShapeDtypeStruct((B, T, Cs), F32)],
        scratch_shapes=[pltpu.VMEM((tm, D), BF16)],
        name="inproj",
        compiler_params=_params(("parallel", "parallel", "arbitrary")),
    )(x, g.reshape(1, D), sc, sh, w_main, w_small, conv_w)


def _head_norm(o, d, g):
    parts = []
    for h in range(o.shape[-1] // d):
        oh = o[:, h * d:(h + 1) * d]
        parts.append(oh * lax.rsqrt(jnp.mean(oh * oh, axis=-1, keepdims=True) + EPS))
    return jnp.concatenate(parts, axis=-1) * g


def _merge_kernel(a_ref, b_ref, ga_ref, gb_ref, na_ref, nb_ref, x_ref, gate_ref, w_ref, o_ref,
                  *, kind, da, db):
    a, b = a_ref[0], b_ref[0]
    if kind == 0:
        ya = _head_norm(a, da, na_ref[...]) * jax.nn.sigmoid(ga_ref[0])
    else:
        ya = jax.nn.gelu(ga_ref[0]) * a
    gb = gb_ref[0]
    yb = _head_norm(b, db, nb_ref[...]) * (gb * jax.nn.sigmoid(gb))
    wa = a.shape[-1]
    y = _dot(ya.astype(BF16), w_ref[0:wa, :]) + _dot(yb.astype(BF16), w_ref[wa:, :])
    o_ref[0] = x_ref[0] + gate_ref[0] * y


def merge_outproj(kind, a, b, pm, ga_off, gb_off, na, nb, da, db, x, gate, w):
    B, T, D = x.shape
    wa, wb = a.shape[-1], b.shape[-1]
    assert ga_off % wa == 0 and gb_off % wb == 0
    tm = _tile(T, 256)
    mod = (lambda b, i: (b, 0, 0)) if gate.shape[0] == B else (lambda b, i: (0, 0, 0))
    tok = lambda width, col: pl.BlockSpec((1, tm, width), lambda b, i: (b, i, col))
    return pl.pallas_call(
        functools.partial(_merge_kernel, kind=kind, da=da, db=db),
        grid=(B, T // tm),
        in_specs=[tok(wa, 0), tok(wb, 0), tok(wa, ga_off // wa), tok(wb, gb_off // wb),
                  pl.BlockSpec((1, wa), lambda b, i: (0, 0)),
                  pl.BlockSpec((1, wb), lambda b, i: (0, 0)),
                  tok(D, 0),
                  pl.BlockSpec((1, 1, D), mod),
                  pl.BlockSpec((wa + wb, D), lambda b, i: (0, 0))],
        out_specs=tok(D, 0),
        out_shape=jax.ShapeDtypeStruct((B, T, D), F32),
        name="merge_outproj",
        compiler_params=_params(("parallel", "parallel")),
    )(a, b, pm, pm, na.reshape(1, wa), nb.reshape(1, wb), x, gate, w)


def _ffn_kernel(x_ref, g_ref, sc_ref, sh_ref, gate_ref, wg_ref, wu_ref, wd_ref, fg_ref, o_ref, h_ref,
                *, final_norm):
    k = pl.program_id(2)

    @pl.when(k == 0)
    def _():
        h_ref[...] = _modulated_norm(x_ref[0], g_ref[...], sc_ref[0], sh_ref[0]).astype(BF16)
        o_ref[...] = jnp.zeros_like(o_ref)

    h = h_ref[...]
    acts = []
    for lo in range(0, wg_ref.shape[1], FFN_SUB):
        a = _dot(h, wg_ref[:, lo:lo + FFN_SUB])
        u = _dot(h, wu_ref[:, lo:lo + FFN_SUB])
        acts.append((a * jax.nn.sigmoid(a) * u).astype(BF16))
    o_ref[0] += _dot(jnp.concatenate(acts, axis=1), wd_ref[...])

    @pl.when(k == pl.num_programs(2) - 1)
    def _():
        r = x_ref[0] + gate_ref[0] * o_ref[0]
        if final_norm:
            r = r * lax.rsqrt(jnp.mean(r * r, axis=-1, keepdims=True) + EPS) * fg_ref[...]
        o_ref[0] = r


def ffn(x, g, sc, sh, gate, w_gu, w_down, final_g=None):
    B, T, D = x.shape
    Fh = w_down.shape[0]
    tm, tf = _tile(T, 512), _tile(Fh, 512)
    nf = Fh // tf
    mod = (lambda b, i, k: (b, 0, 0)) if sc.shape[0] == B else (lambda b, i, k: (0, 0, 0))
    fg = jnp.ones((1, D), F32) if final_g is None else final_g.reshape(1, D)
    return pl.pallas_call(
        functools.partial(_ffn_kernel, final_norm=final_g is not None),
        grid=(B, T // tm, nf),
        in_specs=[pl.BlockSpec((1, tm, D), lambda b, i, k: (b, i, 0)),
                  pl.BlockSpec((1, D), lambda b, i, k: (0, 0)),
                  pl.BlockSpec((1, 1, D), mod),
                  pl.BlockSpec((1, 1, D), mod),
                  pl.BlockSpec((1, 1, D), mod),
                  pl.BlockSpec((D, tf), lambda b, i, k: (0, k)),
                  pl.BlockSpec((D, tf), lambda b, i, k: (0, k + nf)),
                  pl.BlockSpec((tf, D), lambda b, i, k: (k, 0)),
                  pl.BlockSpec((1, D), lambda b, i, k: (0, 0))],
        out_specs=pl.BlockSpec((1, tm, D), lambda b, i, k: (b, i, 0)),
        out_shape=jax.ShapeDtypeStruct((B, T, D), F32),
        scratch_shapes=[pltpu.VMEM((tm, D), BF16)],
        name="ffn",
        compiler_params=_params(("parallel", "parallel", "arbitrary")),
    )(x, g.reshape(1, D), sc, sh, gate, w_gu, w_gu, w_down, fg)


def _incl_mask(rev):
    r = lax.broadcasted_iota(jnp.int32, (CHUNK, CHUNK), 0)
    c = lax.broadcasted_iota(jnp.int32, (CHUNK, CHUNK), 1)
    return (r <= c) if rev else (r >= c)


def _split_bf16(x):
    hi = x.astype(BF16)
    return hi, (x - hi.astype(F32)).astype(BF16)


def _chunk_cumsum(mask_bf16, x):
    hi, lo = _split_bf16(x)
    return _dot(mask_bf16, hi) + _dot(mask_bf16, lo)


def _log_sigmoid(z):
    return jnp.minimum(z, 0.0) - jnp.log1p(jnp.exp(-jnp.abs(z)))


def _softplus(x):
    return jnp.maximum(x, 0.0) + jnp.log1p(jnp.exp(-jnp.abs(x)))


def _time_block(T):
    return _tile(T, 512)


def _acc_input(acc, block, index_map):
    if acc is None:
        return jnp.zeros((1, 8, LANES), F32), pl.BlockSpec((1, 8, LANES), lambda *_: (0, 0, 0))
    return acc, pl.BlockSpec(block, index_map)


def _dwconv(x, w_ref, seg):
    n = x.shape[0]
    taps = w_ref.shape[0]
    pos = lax.broadcasted_iota(jnp.int32, (n, 1), 0) % seg
    out = None
    for j in range(taps):
        sh = j - taps // 2
        xs = x if sh == 0 else pltpu.roll(x, (-sh) % n, axis=0)
        ok = (pos + sh >= 0) & (pos + sh < seg)
        term = jnp.where(ok, xs, 0.0) * w_ref[j:j + 1, :]
        out = term if out is None else out + term
    return out


def _gla_kernel(q_ref, k_ref, v_ref, ps_ref, wup_ref, bup_ref, s0_ref, acc_ref, o_ref, st_ref, lg_ref,
                *, rev, heads, nc, has_acc):
    @pl.when(pl.program_id(1) == 0)
    def _():
        st_ref[...] = s0_ref[...]

    mask = _incl_mask(rev)
    mask_bf = jnp.where(mask, 1.0, 0.0).astype(BF16)
    z = _dot(ps_ref[0].astype(BF16), wup_ref[...]) + bup_ref[...]
    lg_ref[...] = _log_sigmoid(z) * (1.0 / GLA_TAU)

    hs = range(heads)
    ks = [slice(h * GLA_DK, (h + 1) * GLA_DK) for h in hs]
    vs = [slice(h * GLA_DV, (h + 1) * GLA_DV) for h in hs]

    def prepare(rows):
        lg = lg_ref[rows, :]
        b = _chunk_cumsum(mask_bf, lg)
        b_end = jnp.sum(lg, axis=0, keepdims=True)
        k = k_ref[0, rows, :]
        qe = (q_ref[0, rows, :] * (GLA_DK ** -0.5) * jnp.exp(b)).astype(BF16)
        ke = (k * jnp.exp(-b)).astype(BF16)
        kd = (k * jnp.exp(b_end - b)).astype(BF16)
        v = [v_ref[0, rows, vs[h]].astype(BF16) for h in hs]
        att = [_dot(qe[:, ks[h]], ke[:, ks[h]], NT) for h in hs]
        upd = [_dot(v[h], kd[:, ks[h]], TN) for h in hs]
        intra = [_dot(jnp.where(mask, att[h], 0.0).astype(BF16), v[h]) for h in hs]
        return qe, intra, upd, jnp.exp(b_end)

    def body(gi, carry):
        rows_l = []
        for j in range(GLA_GROUP):
            ci = gi * GLA_GROUP + j
            c = (nc - 1 - ci) if rev else ci
            rows_l.append(pl.ds(pl.multiple_of(c * CHUNK, CHUNK), CHUNK))
        preps = [prepare(rows) for rows in rows_l]
        sts = [st_ref[0, h] for h in hs]
        for rows, (qe, intra, upd, e_end) in zip(rows_l, preps):
            inter = [_dot(qe[:, ks[h]], sts[h].astype(BF16), NT) for h in hs]
            for h in hs:
                out = intra[h] + inter[h]
                o_ref[0, rows, vs[h]] = out + acc_ref[0, rows, vs[h]] if has_acc else out
            sts = [sts[h] * e_end[:, ks[h]] + upd[h] for h in hs]
        for h in hs:
            st_ref[0, h] = sts[h]
        return carry

    assert nc % GLA_GROUP == 0
    lax.fori_loop(0, nc // GLA_GROUP, body, 0)


def gla_scan(pm, ps, wup, bup, s0, rev, q_off, k_off, v_off, heads, acc=None):
    B, T, _ = pm.shape
    Tb = _time_block(T)
    NB = T // Tb
    kw, vw = heads * GLA_DK, heads * GLA_DV
    assert q_off % kw == 0 and k_off % kw == 0 and v_off % vw == 0
    blk = (lambda i: NB - 1 - i) if rev else (lambda i: i)
    out_map = lambda b, i: (b, blk(i), 0)
    acc_arr, acc_spec = _acc_input(acc, (1, Tb, vw), out_map)
    return pl.pallas_call(
        functools.partial(_gla_kernel, rev=rev, heads=heads, nc=Tb // CHUNK, has_acc=acc is not None),
        grid=(B, NB),
        in_specs=[pl.BlockSpec((1, Tb, kw), lambda b, i: (b, blk(i), q_off // kw)),
                  pl.BlockSpec((1, Tb, kw), lambda b, i: (b, blk(i), k_off // kw)),
                  pl.BlockSpec((1, Tb, vw), lambda b, i: (b, blk(i), v_off // vw)),
                  pl.BlockSpec((1, Tb, ps.shape[-1]), lambda b, i: (b, blk(i), 0)),
                  pl.BlockSpec(wup.shape, lambda b, i: (0, 0)),
                  pl.BlockSpec(bup.shape, lambda b, i: (0, 0)),
                  pl.BlockSpec((1,) + s0.shape[1:], lambda b, i: (b, 0, 0, 0)),
                  acc_spec],
        out_specs=[pl.BlockSpec((1, Tb, vw), out_map),
                   pl.BlockSpec((1,) + s0.shape[1:], lambda b, i: (b, 0, 0, 0))],
        out_shape=[jax.ShapeDtypeStruct((B, T, vw), F32), jax.ShapeDtypeStruct(s0.shape, F32)],
        scratch_shapes=[pltpu.VMEM((Tb, kw), F32)],
        name="gla_scan",
        compiler_params=_params(("parallel", "arbitrary")),
    )(pm, pm, pm, ps, wup, bup, s0, acc_arr)


def gla_bidir(pm_c, ps_c, pm_l, ps_l, w_up, b_up, q_off, k_off, v_off):
    B = pm_l.shape[0]
    rank, hk = w_up.shape[1], w_up.shape[2]
    heads = hk // GLA_DK
    oc = ol = None
    for d in (0, 1):
        wup = jnp.zeros((ps_l.shape[-1], hk), F32).at[d * rank:(d + 1) * rank].set(w_up[d]).astype(BF16)
        bup = b_up[d].reshape(1, hk)
        s0 = jnp.zeros((B, heads, GLA_DV, GLA_DK), F32)
        oc, s1 = gla_scan(pm_c, ps_c, wup, bup, s0, bool(d), q_off, k_off, v_off, heads, oc)
        ol, _ = gla_scan(pm_l, ps_l, wup, bup, s1, bool(d), q_off, k_off, v_off, heads, ol)
    return oc, ol


def _mlstm_kernel(q_ref, k_ref, v_ref, ps_ref, gb_ref, c0_ref, nm0_ref, acc_ref, o_ref, c_ref, nm_ref,
                  *, rev, heads, nc, ig_col, fg_col, has_acc):
    @pl.when(pl.program_id(1) == 0)
    def _():
        c_ref[...] = c0_ref[...]
        nm_ref[...] = nm0_ref[...]

    mask = _incl_mask(rev)
    mask_bf = jnp.where(mask, 1.0, 0.0).astype(BF16)

    hs = range(heads)
    cols = [(ig_col + h, fg_col + h) for h in hs]

    def prepare(rows):
        g = ps_ref[0, rows, :] + gb_ref[...]
        lf = _log_sigmoid(g)
        bcum = _chunk_cumsum(mask_bf, lf)
        b_end = jnp.sum(lf, axis=0, keepdims=True)
        bcum_t, g_t = bcum.T, g.T
        b_col = [bcum[:, fc:fc + 1] for _, fc in cols]
        b_l = [b_end[:, fc:fc + 1] for _, fc in cols]
        q = [q_ref[0, rows, h * ML_DK:(h + 1) * ML_DK] * (ML_DK ** -0.5) for h in hs]
        k = [k_ref[0, rows, h * ML_DK:(h + 1) * ML_DK] for h in hs]
        v = [v_ref[0, rows, h * ML_DV:(h + 1) * ML_DV] for h in hs]
        qb = [t.astype(BF16) for t in q]
        kb = [t.astype(BF16) for t in k]
        qk = [_dot(qb[h], kb[h], NT) for h in hs]
        d_log = [jnp.where(mask, b_col[h] - bcum_t[fc:fc + 1, :] + g_t[ic:ic + 1, :], -jnp.inf)
                 for h, (ic, fc) in enumerate(cols)]
        m_intra = [jnp.max(t, axis=-1, keepdims=True) for t in d_log]
        s = [qk[h] * jnp.exp(d_log[h] - m_intra[h]) for h in hs]
        num = [_dot(s[h].astype(BF16), v[h].astype(BF16)) for h in hs]
        den = [jnp.sum(t, axis=-1, keepdims=True) for t in s]
        w_log = [b_l[h] - b_col[h] + g[:, ic:ic + 1] for h, (ic, _) in enumerate(cols)]
        m_loc = [jnp.max(t, axis=0, keepdims=True) for t in w_log]
        w = [jnp.exp(w_log[h] - m_loc[h]) for h in hs]
        c_inc = [_dot((w[h] * v[h]).astype(BF16), kb[h], TN) for h in hs]
        n_inc = [jnp.sum(w[h] * k[h], axis=0, keepdims=True) for h in hs]
        return q, qb, b_col, b_l, m_intra, num, den, m_loc, c_inc, n_inc

    def body(gi, carry):
        rows_l = []
        for j in range(ML_GROUP):
            ci = gi * ML_GROUP + j
            c = (nc - 1 - ci) if rev else ci
            rows_l.append(pl.ds(pl.multiple_of(c * CHUNK, CHUNK), CHUNK))
        preps = [prepare(rows) for rows in rows_l]
        cst = [c_ref[0, h] for h in hs]
        n = [nm_ref[0, h, 0:1, :] for h in hs]
        m = [nm_ref[0, h, 1:2, 0:1] for h in hs]
        for rows, (q, qb, b_col, b_l, m_intra, num, den, m_loc, c_inc, n_inc) in zip(rows_l, preps):
            q_c = [_dot(qb[h], cst[h].astype(BF16), NT) for h in hs]
            q_n = [jnp.sum(q[h] * n[h], axis=-1, keepdims=True) for h in hs]
            inter_log = [b_col[h] + m[h] for h in hs]
            m_out = [jnp.maximum(m_intra[h], inter_log[h]) for h in hs]
            e_intra = [jnp.exp(m_intra[h] - m_out[h]) for h in hs]
            e_inter = [jnp.exp(inter_log[h] - m_out[h]) for h in hs]
            top = [e_intra[h] * num[h] + e_inter[h] * q_c[h] for h in hs]
            bot = [e_intra[h] * den[h] + e_inter[h] * q_n[h] for h in hs]
            for h in hs:
                vs = slice(h * ML_DV, (h + 1) * ML_DV)
                out = top[h] / jnp.maximum(jnp.abs(bot[h]), jnp.exp(-m_out[h]))
                o_ref[0, rows, vs] = out + acc_ref[0, rows, vs] if has_acc else out
            m_new = [jnp.maximum(b_l[h] + m[h], m_loc[h]) for h in hs]
            a = [jnp.exp(b_l[h] + m[h] - m_new[h]) for h in hs]
            sc = [jnp.exp(m_loc[h] - m_new[h]) for h in hs]
            cst = [a[h] * cst[h] + sc[h] * c_inc[h] for h in hs]
            n = [a[h] * n[h] + sc[h] * n_inc[h] for h in hs]
            m = m_new
        for h in hs:
            c_ref[0, h] = cst[h]
            nm_ref[0, h, 0:1, :] = n[h]
            nm_ref[0, h, 1:2, :] = jnp.broadcast_to(m[h], (1, ML_DK))
        return carry

    assert nc % ML_GROUP == 0
    lax.fori_loop(0, nc // ML_GROUP, body, 0)


def mlstm_scan(pm, ps, gbias, state, rev, q_off, k_off, v_off, heads, ig_col, fg_col, acc=None):
    B, T, _ = pm.shape
    Tb = _time_block(T)
    NB = T // Tb
    kw, vw = heads * ML_DK, heads * ML_DV
    assert q_off % kw == 0 and k_off % kw == 0 and v_off % vw == 0
    c0, nm0 = state
    blk = (lambda i: NB - 1 - i) if rev else (lambda i: i)
    st_spec = lambda a: pl.BlockSpec((1,) + a.shape[1:], lambda b, i: (b, 0, 0, 0))
    out_map = lambda b, i: (b, blk(i), 0)
    acc_arr, acc_spec = _acc_input(acc, (1, Tb, vw), out_map)
    o, c1, nm1 = pl.pallas_call(
        functools.partial(_mlstm_kernel, rev=rev, heads=heads, nc=Tb // CHUNK, ig_col=ig_col, fg_col=fg_col,
                          has_acc=acc is not None),
        grid=(B, NB),
        in_specs=[pl.BlockSpec((1, Tb, kw), lambda b, i: (b, blk(i), q_off // kw)),
                  pl.BlockSpec((1, Tb, kw), lambda b, i: (b, blk(i), k_off // kw)),
                  pl.BlockSpec((1, Tb, vw), lambda b, i: (b, blk(i), v_off // vw)),
                  pl.BlockSpec((1, Tb, ps.shape[-1]), lambda b, i: (b, blk(i), 0)),
                  pl.BlockSpec(gbias.shape, lambda b, i: (0, 0)),
                  st_spec(c0), st_spec(nm0), acc_spec],
        out_specs=[pl.BlockSpec((1, Tb, vw), out_map), st_spec(c0), st_spec(nm0)],
        out_shape=[jax.ShapeDtypeStruct((B, T, vw), F32), jax.ShapeDtypeStruct(c0.shape, F32),
                   jax.ShapeDtypeStruct(nm0.shape, F32)],
        name="mlstm_scan",
        compiler_params=_params(("parallel", "arbitrary")),
    )(pm, pm, pm, ps, gbias, c0, nm0, acc_arr)
    return o, (c1, nm1)


def mlstm_bidir(pm_c, ps_c, pm_l, ps_l, ig_b, fg_b, q_off, k_off, v_off):
    B = pm_l.shape[0]
    heads = ig_b.shape[-1]
    ncol = ps_l.shape[-1]
    gbias = jnp.zeros((1, ncol), F32).at[0, :4 * heads].set(jnp.concatenate([ig_b.reshape(-1), fg_b.reshape(-1)]))
    oc = ol = None
    for d in (0, 1):
        st = (jnp.zeros((B, heads, ML_DV, ML_DK), F32), jnp.zeros((B, heads, 8, ML_DK), F32))
        args = (bool(d), q_off, k_off, v_off, heads, d * heads, 2 * heads + d * heads)
        oc, st = mlstm_scan(pm_c, ps_c, gbias, st, *args, oc)
        ol, _ = mlstm_scan(pm_l, ps_l, gbias, st, *args, ol)
    return oc, ol


def _block_join_masks():
    r = lax.broadcasted_iota(jnp.int32, (CHUNK, CHUNK), 0)
    c = lax.broadcasted_iota(jnp.int32, (CHUNK, CHUNK), 1)
    masks, s = [], 1
    while s < CHUNK:
        masks.append(jnp.where((r // (2 * s) == c // (2 * s)) & (r // s != c // s), 1.0, 0.0))
        s *= 2
    return masks


def _unit_tri_inverses(a_list, join_masks):
    r = lax.broadcasted_iota(jnp.int32, (CHUNK, CHUNK), 0)
    c = lax.broadcasted_iota(jnp.int32, (CHUNK, CHUNK), 1)
    eye = jnp.where(r == c, 1.0, 0.0)
    ts = [eye - a * join_masks[0] for a in a_list]
    for m in join_masks[1:]:
        tb = [t.astype(BF16) for t in ts]
        xs = [_dot(t, (a * m).astype(BF16)) for t, a in zip(tb, a_list)]
        ts = [t - _dot(x.astype(BF16), t_b) for t, x, t_b in zip(ts, xs, tb)]
    return ts


def _delta_kernel(q_ref, k_ref, v_ref, ps_ref, gp_ref, s0_ref, acc_ref, o_ref, st_ref,
                  *, rev, heads, nc, g_col, beta_col, has_acc):
    @pl.when(pl.program_id(1) == 0)
    def _():
        st_ref[...] = s0_ref[...]

    mask = _incl_mask(rev)
    mask_bf = jnp.where(mask, 1.0, 0.0).astype(BF16)
    r = lax.broadcasted_iota(jnp.int32, (CHUNK, CHUNK), 0)
    c = lax.broadcasted_iota(jnp.int32, (CHUNK, CHUNK), 1)
    off_diag = jnp.where(r != c, 1.0, 0.0)
    join_masks = _block_join_masks()
    L = CHUNK

    def prepare(rows):
        pre = ps_ref[0, rows, :]
        g = gp_ref[1:2, :] * _softplus(pre + gp_ref[0:1, :])
        beta = jax.nn.sigmoid(pre)
        gcum = _chunk_cumsum(mask_bf, g)
        g_end = jnp.sum(g, axis=0, keepdims=True)
        gcum_t = gcum.T
        e_cum, e_rest, e_end = jnp.exp(gcum), jnp.exp(g_end - gcum), jnp.exp(g_end)
        a_l, rhs_l, qe_l, kd_l, qk_l, end_l = [], [], [], [], [], []
        for h in range(heads):
            gc, bc = g_col + h, beta_col + h
            ks = slice(h * DN_DK, (h + 1) * DN_DK)
            q, k = q_ref[0, rows, ks] * (DN_DK ** -0.5), k_ref[0, rows, ks]
            b_c = beta[:, bc:bc + 1]
            decay = jnp.exp(jnp.where(mask, gcum[:, gc:gc + 1] - gcum_t[gc:gc + 1, :], -jnp.inf))
            kbeta = k * b_c
            both = _dot(jnp.concatenate([kbeta, q], axis=0).astype(BF16), k.astype(BF16), NT)
            a_l.append(both[:L] * decay * off_diag)
            qk_l.append((both[L:] * decay).astype(BF16))
            vbeta = v_ref[0, rows, h * DN_DV:(h + 1) * DN_DV] * b_c
            rhs_l.append(jnp.concatenate([vbeta, kbeta * e_cum[:, gc:gc + 1]], axis=1).astype(BF16))
            qe_l.append((q * e_cum[:, gc:gc + 1]).astype(BF16))
            kd_l.append((k * e_rest[:, gc:gc + 1]).astype(BF16))
            end_l.append(e_end[:, gc:gc + 1])
        return a_l, rhs_l, qe_l, kd_l, qk_l, end_l

    def body(gi, carry):
        rows_l = []
        for j in range(DELTA_GROUP):
            ci = gi * DELTA_GROUP + j
            cidx = (nc - 1 - ci) if rev else ci
            rows_l.append(pl.ds(pl.multiple_of(cidx * CHUNK, CHUNK), CHUNK))
        preps = [prepare(rows) for rows in rows_l]
        t_l = _unit_tri_inverses([a for p in preps for a in p[0]], join_masks)
        uw_l = [_dot(t.astype(BF16), rhs) for t, rhs in zip(t_l, [r for p in preps for r in p[1]])]
        hs = range(heads)
        sts = [st_ref[0, h] for h in hs]
        for j, (rows, (_, _, qe_l, kd_l, qk_l, end_l)) in enumerate(zip(rows_l, preps)):
            uws = uw_l[j * heads:(j + 1) * heads]
            lhs = [jnp.concatenate([uws[h][:, DN_DV:].astype(BF16), qe_l[h]], axis=0) for h in hs]
            ws = [_dot(lhs[h], sts[h].astype(BF16)) for h in hs]
            v_new = [(uws[h][:, :DN_DV] - ws[h][:L]).astype(BF16) for h in hs]
            outs = [ws[h][L:] + _dot(qk_l[h], v_new[h]) for h in hs]
            incs = [_dot(kd_l[h], v_new[h], TN) for h in hs]
            for h in hs:
                vs = slice(h * DN_DV, (h + 1) * DN_DV)
                o_ref[0, rows, vs] = outs[h] + acc_ref[0, rows, vs] if has_acc else outs[h]
            sts = [end_l[h] * sts[h] + incs[h] for h in hs]
        for h in hs:
            st_ref[0, h] = sts[h]
        return carry

    assert nc % DELTA_GROUP == 0
    lax.fori_loop(0, nc // DELTA_GROUP, body, 0)


def delta_scan(pm, ps, gparams, s0, rev, q_off, k_off, v_off, heads, g_col, beta_col, acc=None):
    B, T, _ = pm.shape
    Tb = _time_block(T)
    NB = T // Tb
    kw, vw = heads * DN_DK, heads * DN_DV
    assert q_off % kw == 0 and k_off % kw == 0 and v_off % vw == 0
    blk = (lambda i: NB - 1 - i) if rev else (lambda i: i)
    st_spec = pl.BlockSpec((1,) + s0.shape[1:], lambda b, i: (b, 0, 0, 0))
    out_map = lambda b, i: (b, blk(i), 0)
    acc_arr, acc_spec = _acc_input(acc, (1, Tb, vw), out_map)
    return pl.pallas_call(
        functools.partial(_delta_kernel, rev=rev, heads=heads, nc=Tb // CHUNK, g_col=g_col, beta_col=beta_col,
                          has_acc=acc is not None),
        grid=(B, NB),
        in_specs=[pl.BlockSpec((1, Tb, kw), lambda b, i: (b, blk(i), q_off // kw)),
                  pl.BlockSpec((1, Tb, kw), lambda b, i: (b, blk(i), k_off // kw)),
                  pl.BlockSpec((1, Tb, vw), lambda b, i: (b, blk(i), v_off // vw)),
                  pl.BlockSpec((1, Tb, ps.shape[-1]), lambda b, i: (b, blk(i), 0)),
                  pl.BlockSpec(gparams.shape, lambda b, i: (0, 0)),
                  st_spec, acc_spec],
        out_specs=[pl.BlockSpec((1, Tb, vw), out_map), st_spec],
        out_shape=[jax.ShapeDtypeStruct((B, T, vw), F32), jax.ShapeDtypeStruct(s0.shape, F32)],
        name="delta_scan",
        compiler_params=_params(("parallel", "arbitrary")),
    )(pm, pm, pm, ps, gparams, s0, acc_arr)


def delta_bidir(pm_c, ps_c, pm_l, ps_l, a_log, dt_bias, q_off, k_off, v_off, col0):
    B = pm_l.shape[0]
    heads = a_log.shape[-1]
    ncol = ps_l.shape[-1]
    gparams = jnp.zeros((8, ncol), F32)
    gparams = gparams.at[0, col0:col0 + 2 * heads].set(dt_bias.reshape(-1))
    gparams = gparams.at[1, col0:col0 + 2 * heads].set(-jnp.exp(a_log).reshape(-1))
    oc = ol = None
    for d in (0, 1):
        s0 = jnp.zeros((B, heads, DN_DK, DN_DV), F32)
        args = (q_off, k_off, v_off, heads, col0 + d * heads, col0 + 2 * heads + d * heads)
        oc, s1 = delta_scan(pm_c, ps_c, gparams, s0, bool(d), *args, oc)
        ol, _ = delta_scan(pm_l, ps_l, gparams, s1, bool(d), *args, ol)
    return oc, ol


def _rglru_kernel(x_ref, cw_ref, cb_ref, wr_ref, br_ref, wi_ref, bi_ref, lam_ref, h0_ref, acc_ref, o_ref, hT_ref,
                  *, rev, nc, seg, has_acc):
    @pl.when(pl.program_id(2) == 0)
    def _():
        hT_ref[...] = h0_ref[...]

    n = x_ref.shape[1]
    xc = _dwconv(x_ref[0], cw_ref, seg) + cb_ref[...]
    xb = xc.astype(BF16)
    r = jax.nn.sigmoid(_dot(xb, wr_ref[0].astype(BF16)) + br_ref[...])
    i = jax.nn.sigmoid(_dot(xb, wi_ref[0].astype(BF16)) + bi_ref[...])
    log_a = -RG_C * r * _softplus(-lam_ref[...])
    a = jnp.exp(log_a)
    bx = jnp.sqrt(1.0 - jnp.exp(2.0 * log_a)) * (i * xc)

    pos = lax.broadcasted_iota(jnp.int32, (n, 1), 0) % CHUNK
    s = 1
    while s < CHUNK:
        ok = (pos < CHUNK - s) if rev else (pos >= s)
        shift = (n - s) if rev else s
        a_sh = pltpu.roll(a, shift, axis=0)
        b_sh = pltpu.roll(bx, shift, axis=0)
        bx = jnp.where(ok, a * b_sh + bx, bx)
        a = jnp.where(ok, a * a_sh, a)
        s *= 2

    h = hT_ref[0]
    for ci in range(nc):
        c = (nc - 1 - ci) if rev else ci
        rows = slice(c * CHUNK, (c + 1) * CHUNK)
        hc = bx[rows] + a[rows] * h
        o_ref[0, rows, :] = hc + acc_ref[0, rows, :] if has_acc else hc
        h = hc[0:1] if rev else hc[CHUNK - 1:CHUNK]
    hT_ref[0] = h


def rglru_scan(pm, conv_w, conv_b, w_r, b_r, w_i, b_i, lam, h0, rev, seg, width, acc=None):
    B, T, _ = pm.shape
    nblk, bw = w_r.shape[0], w_r.shape[1]
    assert bw % LANES == 0 and nblk * bw == width
    Tb = _tile(T, 1024)
    NB = T // Tb
    assert Tb % seg == 0 or (NB == 1 and seg == T)
    blk = (lambda i: NB - 1 - i) if rev else (lambda i: i)
    row = pl.BlockSpec((1, bw), lambda b, g, i: (0, g))
    out_map = lambda b, g, i: (b, blk(i), g)
    acc_arr, acc_spec = _acc_input(acc, (1, Tb, bw), out_map)
    return pl.pallas_call(
        functools.partial(_rglru_kernel, rev=rev, nc=Tb // CHUNK, seg=seg, has_acc=acc is not None),
        grid=(B, nblk, NB),
        in_specs=[pl.BlockSpec((1, Tb, bw), lambda b, g, i: (b, blk(i), g)),
                  pl.BlockSpec((conv_w.shape[0], bw), lambda b, g, i: (0, g)),
                  row,
                  pl.BlockSpec((1, bw, bw), lambda b, g, i: (g, 0, 0)), row,
                  pl.BlockSpec((1, bw, bw), lambda b, g, i: (g, 0, 0)), row,
                  row,
                  pl.BlockSpec((1, 1, bw), lambda b, g, i: (b, 0, g)),
                  acc_spec],
        out_specs=[pl.BlockSpec((1, Tb, bw), out_map),
                   pl.BlockSpec((1, 1, bw), lambda b, g, i: (b, 0, g))],
        out_shape=[jax.ShapeDtypeStruct((B, T, width), F32), jax.ShapeDtypeStruct(h0.shape, F32)],
        name="rglru_scan",
        compiler_params=_params(("parallel", "parallel", "arbitrary")),
    )(pm, conv_w, conv_b, w_r, b_r, w_i, b_i, lam, h0, acc_arr)


def rglru_bidir(pm_c, pm_l, conv_w, conv_b, w_r, b_r, w_i, b_i, lam):
    B = pm_l.shape[0]
    width = conv_b.shape[-1]
    oc = ol = None
    for d in (0, 1):
        h0 = jnp.zeros((B, 1, width), F32)
        args = (conv_w, conv_b.reshape(1, width), w_r[d], b_r[d].reshape(1, width), w_i[d],
                b_i[d].reshape(1, width), lam[d].reshape(1, width))
        oc, h1 = rglru_scan(pm_c, *args, h0, bool(d), pm_c.shape[1], width, oc)
        ol, _ = rglru_scan(pm_l, *args, h1, bool(d), GRID_W, width, ol)
    return oc, ol


def _pad_cols(w, mult):
    return jnp.pad(w, ((0, 0), (0, (-w.shape[-1]) % mult)))


def kernel(x, c, ctx, c_ctx, mod_w, mod_b, norm_mix_g, norm_ffn_g, ffn_w_gu, ffn_w_down, final_norm_g,
           ab_w_in, ab_w_out, ml_ig_b, ml_fg_b, ml_norm_g, dn_conv_w, dn_a_log, dn_dt_bias, dn_norm_g,
           cd_w_in, cd_w_out, rg_conv_w, rg_conv_b, rg_w_r, rg_b_r, rg_w_i, rg_b_i, rg_lambda,
           gla_w_up, gla_b_up, gla_norm_g):
    B, T, D = x.shape
    depth = mod_w.shape[0]
    mlh, dnh = ml_ig_b.shape[-1], dn_a_log.shape[-1]
    rgw = rg_conv_b.shape[-1]
    glh = gla_w_up.shape[-1] // GLA_DK

    cond = jnp.concatenate([c, c_ctx[None, :], jnp.zeros((8 - B - 1, D), F32)], axis=0)

    m_all = adaln_all(cond, mod_w, mod_b)

    xc, xl = ctx, x
    for layer in range(depth):
        last = layer == depth - 1
        j = layer // 2
        m = m_all[layer]
        lat = [t[:, None, :] for t in jnp.split(m[:B], 6, axis=-1)]
        cx = [t[:, None, :] for t in jnp.split(m[B:B + 1], 6, axis=-1)]

        if layer % 2 == 0:
            w_in = ab_w_in[j]
            s = np.cumsum((0, mlh * ML_DK, mlh * ML_DK, mlh * ML_DV, mlh * ML_DV, 2 * mlh, 2 * mlh,
                           dnh * DN_DK, dnh * DN_DK, dnh * DN_DV, dnh * DN_DV, 2 * dnh, 2 * dnh))
            w_main = jnp.concatenate([w_in[:, s[0]:s[4]], w_in[:, s[6]:s[10]]], axis=1)
            w_small = jnp.concatenate([w_in[:, s[4]:s[6]], w_in[:, s[10]:s[12]]], axis=1)
            w_out = ab_w_out[j]
            o = np.cumsum((0, mlh * ML_DK, mlh * ML_DK, mlh * ML_DV, mlh * ML_DV,
                           dnh * DN_DK, dnh * DN_DK, dnh * DN_DV))
        else:
            w_in = cd_w_in[j]
            o = np.cumsum((0, rgw, rgw, glh * GLA_DK, glh * GLA_DK, glh * GLA_DV))
            nmain = int(o[5]) + glh * GLA_DV
            w_main, w_small = w_in[:, :nmain], w_in[:, nmain:]
            w_out = cd_w_out[j]
        o = [int(v) for v in o]
        w_main = w_main.astype(BF16)
        w_small = _pad_cols(w_small, LANES).astype(BF16)
        w_out = w_out.astype(BF16)

        if layer % 2 == 0:
            conv = lambda seg: dict(conv_w=dn_conv_w[j], conv_off=o[4], norm_cols=2 * dnh * DN_DK, seg=seg,
                                    norm_group=DN_DK)
        else:
            conv = lambda seg: {}
        pm_l, ps_l = inproj(xl, norm_mix_g[layer], lat[1], lat[0], w_main, w_small, **conv(GRID_W))
        pm_c, ps_c = inproj(xc, norm_mix_g[layer], cx[1], cx[0], w_main, w_small, **conv(xc.shape[1]))

        if layer % 2 == 0:
            mix_a = mlstm_bidir(pm_c, ps_c, pm_l, ps_l, ml_ig_b[j], ml_fg_b[j], o[0], o[1], o[2])
            mix_b = delta_bidir(pm_c, ps_c, pm_l, ps_l, dn_a_log[j], dn_dt_bias[j], o[4], o[5], o[6], 4 * mlh)
            margs = (0, o[3], o[7], ml_norm_g[j], dn_norm_g[j], ML_DV, DN_DV)
        else:
            mix_a = rglru_bidir(pm_c, pm_l, rg_conv_w[j], rg_conv_b[j], rg_w_r[j], rg_b_r[j], rg_w_i[j],
                                rg_b_i[j], rg_lambda[j])
            mix_b = gla_bidir(pm_c, ps_c, pm_l, ps_l, gla_w_up[j], gla_b_up[j], o[2], o[3], o[4])
            margs = (1, o[1], o[5], jnp.ones((rgw,), F32), gla_norm_g[j], rgw, GLA_DV)

        def merged(which, pm, xres, gate):
            kind, ga_off, gb_off, na, nb, da, db = margs
            return merge_outproj(kind, mix_a[which], mix_b[which], pm, ga_off, gb_off, na, nb, da, db, xres, gate,
                                 w_out)

        w_gu = ffn_w_gu[layer].astype(BF16)
        w_dn = ffn_w_down[layer].astype(BF16)
        xl = merged(1, pm_l, xl, lat[2])
        xl = ffn(xl, norm_ffn_g[layer], lat[4], lat[3], lat[5], w_gu, w_dn, final_norm_g if last else None)
        if not last:
            xc = merged(0, pm_c, xc, cx[2])
            xc = ffn(xc, norm_ffn_g[layer], cx[4], cx[3], cx[5], w_gu, w_dn)
    return xl
```

```python
import functools

import jax
import jax.numpy as jnp
import numpy as np
from jax import lax
from jax.experimental import pallas as pl
from jax.experimental.pallas import tpu as pltpu

F32 = jnp.float32
BF16 = jnp.bfloat16

GRID_W = 64
CHUNK = 64
EPS = 1e-6
ML_DK, ML_DV = 128, 256
DN_DK, DN_DV = 128, 128
GLA_DK, GLA_DV = 128, 256
RG_C = 8.0
GLA_TAU = 16.0

LANES = 128
DELTA_GROUP = 4
GLA_GROUP = 4
ML_GROUP = 4
FFN_SUB = 256
VMEM_LIMIT = 48 * 1024 * 1024

NT = (((1,), (1,)), ((), ()))
TN = (((0,), (0,)), ((), ()))


def _tile(n, pref):
    t = min(n, pref)
    while n % t:
        t //= 2
    return t


def _params(sem, vmem=VMEM_LIMIT):
    return pltpu.CompilerParams(dimension_semantics=sem, vmem_limit_bytes=vmem)


def _dot(a, b, dims=None):
    if dims is None:
        return jnp.dot(a, b, preferred_element_type=F32)
    return lax.dot_general(a, b, dims, preferred_element_type=F32)


def _modulated_norm(x, g, sc, sh):
    y = x * lax.rsqrt(jnp.mean(x * x, axis=-1, keepdims=True) + EPS) * g
    return y * (1.0 + sc) + sh


def _adaln_kernel(s_ref, w_ref, b_ref, o_ref):
    s = s_ref[...]
    s = (s * jax.nn.sigmoid(s)).astype(BF16)
    o_ref[0] = _dot(s, w_ref[0].astype(BF16)) + b_ref[0]


def adaln_all(cond, w_mod, b_mod):
    R, D = cond.shape
    depth, _, C = w_mod.shape
    tn = _tile(C, 1024)
    return pl.pallas_call(
        _adaln_kernel,
        grid=(depth, C // tn),
        in_specs=[pl.BlockSpec((R, D), lambda l, j: (0, 0)),
                  pl.BlockSpec((1, D, tn), lambda l, j: (l, 0, j)),
                  pl.BlockSpec((1, 1, tn), lambda l, j: (l, 0, j))],
        out_specs=pl.BlockSpec((1, R, tn), lambda l, j: (l, 0, j)),
        out_shape=jax.ShapeDtypeStruct((depth, R, C), F32),
        name="adaln_all",
        compiler_params=_params(("parallel", "parallel")),
    )(cond, w_mod, b_mod.reshape(depth, 1, C))


def _inproj_kernel(x_ref, g_ref, sc_ref, sh_ref, w_ref, ws_ref, cw_ref, o_ref, os_ref, h_ref,
                   *, conv_tiles, norm_tiles, seg, norm_group):
    j = pl.program_id(2)

    @pl.when(j == 0)
    def _():
        h = _modulated_norm(x_ref[0], g_ref[...], sc_ref[0], sh_ref[0]).astype(BF16)
        h_ref[...] = h
        os_ref[0] = _dot(h, ws_ref[...])

    o_ref[0] = _dot(h_ref[...], w_ref[...])

    @pl.when((j >= conv_tiles[0]) & (j < conv_tiles[1]))
    def _():
        o_ref[0] = jax.nn.silu(_dwconv(o_ref[0], cw_ref, seg))

    @pl.when((j >= norm_tiles[0]) & (j < norm_tiles[1]))
    def _():
        for lo in range(0, o_ref.shape[-1], norm_group):
            y = o_ref[0, :, lo:lo + norm_group]
            o_ref[0, :, lo:lo + norm_group] = y * lax.rsqrt(jnp.sum(y * y, axis=-1, keepdims=True) + EPS)


def inproj(x, g, sc, sh, w_main, w_small, conv_w=None, conv_off=0, norm_cols=0, seg=1, norm_group=LANES):
    B, T, D = x.shape
    C = w_main.shape[1]
    Cs = w_small.shape[1]
    tm, tn = _tile(T, 1024), _tile(C, 1024)
    if conv_w is None:
        conv_w, conv_tiles, norm_tiles = jnp.zeros((1, tn), F32), (0, 0), (0, 0)
    else:
        assert conv_off % tn == 0 and conv_w.shape[1] % tn == 0 and norm_cols % tn == 0 and tn % norm_group == 0
        assert tm % seg == 0 or (tm == T and seg == T)
        conv_tiles = (conv_off // tn, (conv_off + conv_w.shape[1]) // tn)
        norm_tiles = (conv_off // tn, (conv_off + norm_cols) // tn)
    n_conv = conv_w.shape[1] // tn
    mod = (lambda b, i, j: (b, 0, 0)) if sc.shape[0] == B else (lambda b, i, j: (0, 0, 0))
    return pl.pallas_call(
        functools.partial(_inproj_kernel, conv_tiles=conv_tiles, norm_tiles=norm_tiles, seg=seg,
                          norm_group=norm_group),
        grid=(B, T // tm, C // tn),
        in_specs=[pl.BlockSpec((1, tm, D), lambda b, i, j: (b, i, 0)),
                  pl.BlockSpec((1, D), lambda b, i, j: (0, 0)),
                  pl.BlockSpec((1, 1, D), mod),
                  pl.BlockSpec((1, 1, D), mod),
                  pl.BlockSpec((D, tn), lambda b, i, j: (0, j)),
                  pl.BlockSpec((D, Cs), lambda b, i, j: (0, 0)),
                  pl.BlockSpec((conv_w.shape[0], tn),
                               lambda b, i, j: (0, jnp.clip(j - conv_tiles[0], 0, n_conv - 1)))],
        out_specs=[pl.BlockSpec((1, tm, tn), lambda b, i, j: (b, i, j)),
                   pl.BlockSpec((1, tm, Cs), lambda b, i, j: (b, i, 0))],
        out_shape=[jax.ShapeDtypeStruct((B, T, C), F32), jax.ShapeDtypeStruct((B, T, Cs), F32)],
        scratch_shapes=[pltpu.VMEM((tm, D), BF16)],
        name="inproj",
        compiler_params=_params(("parallel", "parallel", "arbitrary")),
    )(x, g.reshape(1, D), sc, sh, w_main, w_small, conv_w)


def _head_norm(o, d, g):
    parts = []
    for h in range(o.shape[-1] // d):
        oh = o[:, h * d:(h + 1) * d]
        parts.append(oh * lax.rsqrt(jnp.mean(oh * oh, axis=-1, keepdims=True) + EPS))
    return jnp.concatenate(parts, axis=-1) * g


def _merge_kernel(a_ref, b_ref, ga_ref, gb_ref, na_ref, nb_ref, x_ref, gate_ref, w_ref, o_ref,
                  *, kind, da, db):
    a, b = a_ref[0], b_ref[0]
    if kind == 0:
        ya = _head_norm(a, da, na_ref[...]) * jax.nn.sigmoid(ga_ref[0])
    else:
        ya = jax.nn.gelu(ga_ref[0]) * a
    gb = gb_ref[0]
    yb = _head_norm(b, db, nb_ref[...]) * (gb * jax.nn.sigmoid(gb))
    wa = a.shape[-1]
    y = _dot(ya.astype(BF16), w_ref[0:wa, :]) + _dot(yb.astype(BF16), w_ref[wa:, :])
    o_ref[0] = x_ref[0] + gate_ref[0] * y


def merge_outproj(kind, a, b, pm, ga_off, gb_off, na, nb, da, db, x, gate, w):
    B, T, D = x.shape
    wa, wb = a.shape[-1], b.shape[-1]
    assert ga_off % wa == 0 and gb_off % wb == 0
    tm = _tile(T, 256)
    mod = (lambda b, i: (b, 0, 0)) if gate.shape[0] == B else (lambda b, i: (0, 0, 0))
    tok = lambda width, col: pl.BlockSpec((1, tm, width), lambda b, i: (b, i, col))
    return pl.pallas_call(
        functools.partial(_merge_kernel, kind=kind, da=da, db=db),
        grid=(B, T // tm),
        in_specs=[tok(wa, 0), tok(wb, 0), tok(wa, ga_off // wa), tok(wb, gb_off // wb),
                  pl.BlockSpec((1, wa), lambda b, i: (0, 0)),
                  pl.BlockSpec((1, wb), lambda b, i: (0, 0)),
                  tok(D, 0),
                  pl.BlockSpec((1, 1, D), mod),
                  pl.BlockSpec((wa + wb, D), lambda b, i: (0, 0))],
        out_specs=tok(D, 0),
        out_shape=jax.ShapeDtypeStruct((B, T, D), F32),
        name="merge_outproj",
        compiler_params=_params(("parallel", "parallel")),
    )(a, b, pm, pm, na.reshape(1, wa), nb.reshape(1, wb), x, gate, w)


def _ffn_kernel(x_ref, g_ref, sc_ref, sh_ref, gate_ref, wg_ref, wu_ref, wd_ref, fg_ref, o_ref, h_ref,
                *, final_norm):
    k = pl.program_id(2)

    @pl.when(k == 0)
    def _():
        h_ref[...] = _modulated_norm(x_ref[0], g_ref[...], sc_ref[0], sh_ref[0]).astype(BF16)
        o_ref[...] = jnp.zeros_like(o_ref)

    h = h_ref[...]
    acts = []
    for lo in range(0, wg_ref.shape[1], FFN_SUB):
        a = _dot(h, wg_ref[:, lo:lo + FFN_SUB])
        u = _dot(h, wu_ref[:, lo:lo + FFN_SUB])
        acts.append((a * jax.nn.sigmoid(a) * u).astype(BF16))
    o_ref[0] += _dot(jnp.concatenate(acts, axis=1), wd_ref[...])

    @pl.when(k == pl.num_programs(2) - 1)
    def _():
        r = x_ref[0] + gate_ref[0] * o_ref[0]
        if final_norm:
            r = r * lax.rsqrt(jnp.mean(r * r, axis=-1, keepdims=True) + EPS) * fg_ref[...]
        o_ref[0] = r


def ffn(x, g, sc, sh, gate, w_gu, w_down, final_g=None):
    B, T, D = x.shape
    Fh = w_down.shape[0]
    tm, tf = _tile(T, 512), _tile(Fh, 512)
    nf = Fh // tf
    mod = (lambda b, i, k: (b, 0, 0)) if sc.shape[0] == B else (lambda b, i, k: (0, 0, 0))
    fg = jnp.ones((1, D), F32) if final_g is None else final_g.reshape(1, D)
    return pl.pallas_call(
        functools.partial(_ffn_kernel, final_norm=final_g is not None),
        grid=(B, T // tm, nf),
        in_specs=[pl.BlockSpec((1, tm, D), lambda b, i, k: (b, i, 0)),
                  pl.BlockSpec((1, D), lambda b, i, k: (0, 0)),
                  pl.BlockSpec((1, 1, D), mod),
                  pl.BlockSpec((1, 1, D), mod),
                  pl.BlockSpec((1, 1, D), mod),
                  pl.BlockSpec((D, tf), lambda b, i, k: (0, k)),
                  pl.BlockSpec((D, tf), lambda b, i, k: (0, k + nf)),
                  pl.BlockSpec((tf, D), lambda b, i, k: (k, 0)),
                  pl.BlockSpec((1, D), lambda b, i, k: (0, 0))],
        out_specs=pl.BlockSpec((1, tm, D), lambda b, i, k: (b, i, 0)),
        out_shape=jax.ShapeDtypeStruct((B, T, D), F32),
        scratch_shapes=[pltpu.VMEM((tm, D), BF16)],
        name="ffn",
        compiler_params=_params(("parallel", "parallel", "arbitrary")),
    )(x, g.reshape(1, D), sc, sh, gate, w_gu, w_gu, w_down, fg)


def _incl_mask(rev):
    r = lax.broadcasted_iota(jnp.int32, (CHUNK, CHUNK), 0)
    c = lax.broadcasted_iota(jnp.int32, (CHUNK, CHUNK), 1)
    return (r <= c) if rev else (r >= c)


def _split_bf16(x):
    hi = x.astype(BF16)
    return hi, (x - hi.astype(F32)).astype(BF16)


def _chunk_cumsum(mask_bf16, x):
    hi, lo = _split_bf16(x)
    return _dot(mask_bf16, hi) + _dot(mask_bf16, lo)


def _log_sigmoid(z):
    return jnp.minimum(z, 0.0) - jnp.log1p(jnp.exp(-jnp.abs(z)))


def _softplus(x):
    return jnp.maximum(x, 0.0) + jnp.log1p(jnp.exp(-jnp.abs(x)))


def _time_block(T, pref=512):
    return _tile(T, pref)


def _acc_input(acc, block, index_map):
    if acc is None:
        return jnp.zeros((1, 8, LANES), F32), pl.BlockSpec((1, 8, LANES), lambda *_: (0, 0, 0))
    return acc, pl.BlockSpec(block, index_map)


def _dwconv(x, w_ref, seg):
    n = x.shape[0]
    taps = w_ref.shape[0]
    pos = lax.broadcasted_iota(jnp.int32, (n, 1), 0) % seg
    out = None
    for j in range(taps):
        sh = j - taps // 2
        xs = x if sh == 0 else pltpu.roll(x, (-sh) % n, axis=0)
        ok = (pos + sh >= 0) & (pos + sh < seg)
        term = jnp.where(ok, xs, 0.0) * w_ref[j:j + 1, :]
        out = term if out is None else out + term
    return out


def _gla_kernel(q_ref, k_ref, v_ref, ps_ref, wup_ref, bup_ref, s0_ref, acc_ref, o_ref, st_ref, lg_ref,
                *, rev, heads, nc, has_acc):
    @pl.when(pl.program_id(1) == 0)
    def _():
        st_ref[...] = s0_ref[...]

    mask = _incl_mask(rev)
    mask_bf = jnp.where(mask, 1.0, 0.0).astype(BF16)
    z = _dot(ps_ref[0].astype(BF16), wup_ref[...]) + bup_ref[...]
    lg_ref[...] = _log_sigmoid(z) * (1.0 / GLA_TAU)

    hs = range(heads)
    ks = [slice(h * GLA_DK, (h + 1) * GLA_DK) for h in hs]
    vs = [slice(h * GLA_DV, (h + 1) * GLA_DV) for h in hs]

    def prepare(rows):
        lg = lg_ref[rows, :]
        b = _chunk_cumsum(mask_bf, lg)
        b_end = jnp.sum(lg, axis=0, keepdims=True)
        k = k_ref[0, rows, :]
        qe = (q_ref[0, rows, :] * (GLA_DK ** -0.5) * jnp.exp(b)).astype(BF16)
        ke = (k * jnp.exp(-b)).astype(BF16)
        kd = (k * jnp.exp(b_end - b)).astype(BF16)
        v = [v_ref[0, rows, vs[h]].astype(BF16) for h in hs]
        att = [_dot(qe[:, ks[h]], ke[:, ks[h]], NT) for h in hs]
        upd = [_dot(v[h], kd[:, ks[h]], TN) for h in hs]
        intra = [_dot(jnp.where(mask, att[h], 0.0).astype(BF16), v[h]) for h in hs]
        return qe, intra, upd, jnp.exp(b_end)

    def body(gi, carry):
        rows_l = []
        for j in range(GLA_GROUP):
            ci = gi * GLA_GROUP + j
            c = (nc - 1 - ci) if rev else ci
            rows_l.append(pl.ds(pl.multiple_of(c * CHUNK, CHUNK), CHUNK))
        preps = [prepare(rows) for rows in rows_l]
        sts = [st_ref[0, h] for h in hs]
        for rows, (qe, intra, upd, e_end) in zip(rows_l, preps):
            inter = [_dot(qe[:, ks[h]], sts[h].astype(BF16), NT) for h in hs]
            for h in hs:
                out = intra[h] + inter[h]
                o_ref[0, rows, vs[h]] = out + acc_ref[0, rows, vs[h]] if has_acc else out
            sts = [sts[h] * e_end[:, ks[h]] + upd[h] for h in hs]
        for h in hs:
            st_ref[0, h] = sts[h]
        return carry

    assert nc % GLA_GROUP == 0
    lax.fori_loop(0, nc // GLA_GROUP, body, 0)


def gla_scan(pm, ps, wup, bup, s0, rev, q_off, k_off, v_off, heads, acc=None):
    B, T, _ = pm.shape
    Tb = _time_block(T, 1024)
    NB = T // Tb
    kw, vw = heads * GLA_DK, heads * GLA_DV
    assert q_off % kw == 0 and k_off % kw == 0 and v_off % vw == 0
    blk = (lambda i: NB - 1 - i) if rev else (lambda i: i)
    out_map = lambda b, i: (b, blk(i), 0)
    acc_arr, acc_spec = _acc_input(acc, (1, Tb, vw), out_map)
    return pl.pallas_call(
        functools.partial(_gla_kernel, rev=rev, heads=heads, nc=Tb // CHUNK, has_acc=acc is not None),
        grid=(B, NB),
        in_specs=[pl.BlockSpec((1, Tb, kw), lambda b, i: (b, blk(i), q_off // kw)),
                  pl.BlockSpec((1, Tb, kw), lambda b, i: (b, blk(i), k_off // kw)),
                  pl.BlockSpec((1, Tb, vw), lambda b, i: (b, blk(i), v_off // vw)),
                  pl.BlockSpec((1, Tb, ps.shape[-1]), lambda b, i: (b, blk(i), 0)),
                  pl.BlockSpec(wup.shape, lambda b, i: (0, 0)),
                  pl.BlockSpec(bup.shape, lambda b, i: (0, 0)),
                  pl.BlockSpec((1,) + s0.shape[1:], lambda b, i: (b, 0, 0, 0)),
                  acc_spec],
        out_specs=[pl.BlockSpec((1, Tb, vw), out_map),
                   pl.BlockSpec((1,) + s0.shape[1:], lambda b, i: (b, 0, 0, 0))],
        out_shape=[jax.ShapeDtypeStruct((B, T, vw), F32), jax.ShapeDtypeStruct(s0.shape, F32)],
        scratch_shapes=[pltpu.VMEM((Tb, kw), F32)],
        name="gla_scan",
        compiler_params=_params(("parallel", "arbitrary")),
    )(pm, pm, pm, ps, wup, bup, s0, acc_arr)


def gla_bidir(pm_c, ps_c, pm_l, ps_l, w_up, b_up, q_off, k_off, v_off):
    B = pm_l.shape[0]
    rank, hk = w_up.shape[1], w_up.shape[2]
    heads = hk // GLA_DK
    oc = ol = None
    for d in (0, 1):
        wup = jnp.zeros((ps_l.shape[-1], hk), F32).at[d * rank:(d + 1) * rank].set(w_up[d]).astype(BF16)
        bup = b_up[d].reshape(1, hk)
        s0 = jnp.zeros((B, heads, GLA_DV, GLA_DK), F32)
        oc, s1 = gla_scan(pm_c, ps_c, wup, bup, s0, bool(d), q_off, k_off, v_off, heads, oc)
        ol, _ = gla_scan(pm_l, ps_l, wup, bup, s1, bool(d), q_off, k_off, v_off, heads, ol)
    return oc, ol


def _mlstm_kernel(q_ref, k_ref, v_ref, ps_ref, gb_ref, c0_ref, nm0_ref, acc_ref, o_ref, c_ref, nm_ref,
                  *, rev, heads, nc, ig_col, fg_col, has_acc):
    @pl.when(pl.program_id(1) == 0)
    def _():
        c_ref[...] = c0_ref[...]
        nm_ref[...] = nm0_ref[...]

    mask = _incl_mask(rev)
    mask_bf = jnp.where(mask, 1.0, 0.0).astype(BF16)

    hs = range(heads)
    cols = [(ig_col + h, fg_col + h) for h in hs]

    def prepare(rows):
        g = ps_ref[0, rows, :] + gb_ref[...]
        lf = _log_sigmoid(g)
        bcum = _chunk_cumsum(mask_bf, lf)
        b_end = jnp.sum(lf, axis=0, keepdims=True)
        bcum_t, g_t = bcum.T, g.T
        b_col = [bcum[:, fc:fc + 1] for _, fc in cols]
        b_l = [b_end[:, fc:fc + 1] for _, fc in cols]
        q = [q_ref[0, rows, h * ML_DK:(h + 1) * ML_DK] * (ML_DK ** -0.5) for h in hs]
        k = [k_ref[0, rows, h * ML_DK:(h + 1) * ML_DK] for h in hs]
        v = [v_ref[0, rows, h * ML_DV:(h + 1) * ML_DV] for h in hs]
        qb = [t.astype(BF16) for t in q]
        kb = [t.astype(BF16) for t in k]
        qk = [_dot(qb[h], kb[h], NT) for h in hs]
        d_log = [jnp.where(mask, b_col[h] - bcum_t[fc:fc + 1, :] + g_t[ic:ic + 1, :], -jnp.inf)
                 for h, (ic, fc) in enumerate(cols)]
        m_intra = [jnp.max(t, axis=-1, keepdims=True) for t in d_log]
        s = [qk[h] * jnp.exp(d_log[h] - m_intra[h]) for h in hs]
        num = [_dot(s[h].astype(BF16), v[h].astype(BF16)) for h in hs]
        den = [jnp.sum(t, axis=-1, keepdims=True) for t in s]
        w_log = [b_l[h] - b_col[h] + g[:, ic:ic + 1] for h, (ic, _) in enumerate(cols)]
        m_loc = [jnp.max(t, axis=0, keepdims=True) for t in w_log]
        w = [jnp.exp(w_log[h] - m_loc[h]) for h in hs]
        c_inc = [_dot((w[h] * v[h]).astype(BF16), kb[h], TN) for h in hs]
        n_inc = [jnp.sum(w[h] * k[h], axis=0, keepdims=True) for h in hs]
        return q, qb, b_col, b_l, m_intra, num, den, m_loc, c_inc, n_inc

    def body(gi, carry):
        rows_l = []
        for j in range(ML_GROUP):
            ci = gi * ML_GROUP + j
            c = (nc - 1 - ci) if rev else ci
            rows_l.append(pl.ds(pl.multiple_of(c * CHUNK, CHUNK), CHUNK))
        preps = [prepare(rows) for rows in rows_l]
        cst = [c_ref[0, h] for h in hs]
        n = [nm_ref[0, h, 0:1, :] for h in hs]
        m = [nm_ref[0, h, 1:2, 0:1] for h in hs]
        for rows, (q, qb, b_col, b_l, m_intra, num, den, m_loc, c_inc, n_inc) in zip(rows_l, preps):
            q_c = [_dot(qb[h], cst[h].astype(BF16), NT) for h in hs]
            q_n = [jnp.sum(q[h] * n[h], axis=-1, keepdims=True) for h in hs]
            inter_log = [b_col[h] + m[h] for h in hs]
            m_out = [jnp.maximum(m_intra[h], inter_log[h]) for h in hs]
            e_intra = [jnp.exp(m_intra[h] - m_out[h]) for h in hs]
            e_inter = [jnp.exp(inter_log[h] - m_out[h]) for h in hs]
            top = [e_intra[h] * num[h] + e_inter[h] * q_c[h] for h in hs]
            bot = [e_intra[h] * den[h] + e_inter[h] * q_n[h] for h in hs]
            for h in hs:
                vs = slice(h * ML_DV, (h + 1) * ML_DV)
                out = top[h] / jnp.maximum(jnp.abs(bot[h]), jnp.exp(-m_out[h]))
                o_ref[0, rows, vs] = out + acc_ref[0, rows, vs] if has_acc else out
            m_new = [jnp.maximum(b_l[h] + m[h], m_loc[h]) for h in hs]
            a = [jnp.exp(b_l[h] + m[h] - m_new[h]) for h in hs]
            sc = [jnp.exp(m_loc[h] - m_new[h]) for h in hs]
            cst = [a[h] * cst[h] + sc[h] * c_inc[h] for h in hs]
            n = [a[h] * n[h] + sc[h] * n_inc[h] for h in hs]
            m = m_new
        for h in hs:
            c_ref[0, h] = cst[h]
            nm_ref[0, h, 0:1, :] = n[h]
            nm_ref[0, h, 1:2, :] = jnp.broadcast_to(m[h], (1, ML_DK))
        return carry

    assert nc % ML_GROUP == 0
    lax.fori_loop(0, nc // ML_GROUP, body, 0)


def mlstm_scan(pm, ps, gbias, state, rev, q_off, k_off, v_off, heads, ig_col, fg_col, acc=None):
    B, T, _ = pm.shape
    Tb = _time_block(T, 1024)
    NB = T // Tb
    kw, vw = heads * ML_DK, heads * ML_DV
    assert q_off % kw == 0 and k_off % kw == 0 and v_off % vw == 0
    c0, nm0 = state
    blk = (lambda i: NB - 1 - i) if rev else (lambda i: i)
    st_spec = lambda a: pl.BlockSpec((1,) + a.shape[1:], lambda b, i: (b, 0, 0, 0))
    out_map = lambda b, i: (b, blk(i), 0)
    acc_arr, acc_spec = _acc_input(acc, (1, Tb, vw), out_map)
    o, c1, nm1 = pl.pallas_call(
        functools.partial(_mlstm_kernel, rev=rev, heads=heads, nc=Tb // CHUNK, ig_col=ig_col, fg_col=fg_col,
                          has_acc=acc is not None),
        grid=(B, NB),
        in_specs=[pl.BlockSpec((1, Tb, kw), lambda b, i: (b, blk(i), q_off // kw)),
                  pl.BlockSpec((1, Tb, kw), lambda b, i: (b, blk(i), k_off // kw)),
                  pl.BlockSpec((1, Tb, vw), lambda b, i: (b, blk(i), v_off // vw)),
                  pl.BlockSpec((1, Tb, ps.shape[-1]), lambda b, i: (b, blk(i), 0)),
                  pl.BlockSpec(gbias.shape, lambda b, i: (0, 0)),
                  st_spec(c0), st_spec(nm0), acc_spec],
        out_specs=[pl.BlockSpec((1, Tb, vw), out_map), st_spec(c0), st_spec(nm0)],
        out_shape=[jax.ShapeDtypeStruct((B, T, vw), F32), jax.ShapeDtypeStruct(c0.shape, F32),
                   jax.ShapeDtypeStruct(nm0.shape, F32)],
        name="mlstm_scan",
        compiler_params=_params(("parallel", "arbitrary")),
    )(pm, pm, pm, ps, gbias, c0, nm0, acc_arr)
    return o, (c1, nm1)


def mlstm_bidir(pm_c, ps_c, pm_l, ps_l, ig_b, fg_b, q_off, k_off, v_off):
    B = pm_l.shape[0]
    heads = ig_b.shape[-1]
    ncol = ps_l.shape[-1]
    gbias = jnp.zeros((1, ncol), F32).at[0, :4 * heads].set(jnp.concatenate([ig_b.reshape(-1), fg_b.reshape(-1)]))
    oc = ol = None
    for d in (0, 1):
        st = (jnp.zeros((B, heads, ML_DV, ML_DK), F32), jnp.zeros((B, heads, 8, ML_DK), F32))
        args = (bool(d), q_off, k_off, v_off, heads, d * heads, 2 * heads + d * heads)
        oc, st = mlstm_scan(pm_c, ps_c, gbias, st, *args, oc)
        ol, _ = mlstm_scan(pm_l, ps_l, gbias, st, *args, ol)
    return oc, ol


def _block_join_masks():
    r = lax.broadcasted_iota(jnp.int32, (CHUNK, CHUNK), 0)
    c = lax.broadcasted_iota(jnp.int32, (CHUNK, CHUNK), 1)
    masks, s = [], 1
    while s < CHUNK:
        masks.append(jnp.where((r // (2 * s) == c // (2 * s)) & (r // s != c // s), 1.0, 0.0))
        s *= 2
    return masks


def _unit_tri_inverses(a_list, join_masks):
    r = lax.broadcasted_iota(jnp.int32, (CHUNK, CHUNK), 0)
    c = lax.broadcasted_iota(jnp.int32, (CHUNK, CHUNK), 1)
    eye = jnp.where(r == c, 1.0, 0.0)
    ts = [eye - a * join_masks[0] for a in a_list]
    for m in join_masks[1:]:
        tb = [t.astype(BF16) for t in ts]
        xs = [_dot(t, (a * m).astype(BF16)) for t, a in zip(tb, a_list)]
        ts = [t - _dot(x.astype(BF16), t_b) for t, x, t_b in zip(ts, xs, tb)]
    return ts


def _delta_kernel(q_ref, k_ref, v_ref, ps_ref, gp_ref, s0_ref, acc_ref, o_ref, st_ref,
                  *, rev, heads, nc, g_col, beta_col, has_acc):
    @pl.when(pl.program_id(1) == 0)
    def _():
        st_ref[...] = s0_ref[...]

    mask = _incl_mask(rev)
    mask_bf = jnp.where(mask, 1.0, 0.0).astype(BF16)
    r = lax.broadcasted_iota(jnp.int32, (CHUNK, CHUNK), 0)
    c = lax.broadcasted_iota(jnp.int32, (CHUNK, CHUNK), 1)
    off_diag = jnp.where(r != c, 1.0, 0.0)
    join_masks = _block_join_masks()
    L = CHUNK

    def prepare(rows):
        pre = ps_ref[0, rows, :]
        g = gp_ref[1:2, :] * _softplus(pre + gp_ref[0:1, :])
        beta = jax.nn.sigmoid(pre)
        gcum = _chunk_cumsum(mask_bf, g)
        g_end = jnp.sum(g, axis=0, keepdims=True)
        gcum_t = gcum.T
        e_cum, e_rest, e_end = jnp.exp(gcum), jnp.exp(g_end - gcum), jnp.exp(g_end)
        a_l, rhs_l, qe_l, kd_l, qk_l, end_l = [], [], [], [], [], []
        for h in range(heads):
            gc, bc = g_col + h, beta_col + h
            ks = slice(h * DN_DK, (h + 1) * DN_DK)
            q, k = q_ref[0, rows, ks] * (DN_DK ** -0.5), k_ref[0, rows, ks]
            b_c = beta[:, bc:bc + 1]
            decay = jnp.exp(jnp.where(mask, gcum[:, gc:gc + 1] - gcum_t[gc:gc + 1, :], -jnp.inf))
            kbeta = k * b_c
            both = _dot(jnp.concatenate([kbeta, q], axis=0).astype(BF16), k.astype(BF16), NT)
            a_l.append(both[:L] * decay * off_diag)
            qk_l.append((both[L:] * decay).astype(BF16))
            vbeta = v_ref[0, rows, h * DN_DV:(h + 1) * DN_DV] * b_c
            rhs_l.append(jnp.concatenate([vbeta, kbeta * e_cum[:, gc:gc + 1]], axis=1).astype(BF16))
            qe_l.append((q * e_cum[:, gc:gc + 1]).astype(BF16))
            kd_l.append((k * e_rest[:, gc:gc + 1]).astype(BF16))
            end_l.append(e_end[:, gc:gc + 1])
        return a_l, rhs_l, qe_l, kd_l, qk_l, end_l

    def body(gi, carry):
        rows_l = []
        for j in range(DELTA_GROUP):
            ci = gi * DELTA_GROUP + j
            cidx = (nc - 1 - ci) if rev else ci
            rows_l.append(pl.ds(pl.multiple_of(cidx * CHUNK, CHUNK), CHUNK))
        preps = [prepare(rows) for rows in rows_l]
        t_l = _unit_tri_inverses([a for p in preps for a in p[0]], join_masks)
        uw_l = [_dot(t.astype(BF16), rhs) for t, rhs in zip(t_l, [r for p in preps for r in p[1]])]
        hs = range(heads)
        sts = [st_ref[0, h] for h in hs]
        for j, (rows, (_, _, qe_l, kd_l, qk_l, end_l)) in enumerate(zip(rows_l, preps)):
            uws = uw_l[j * heads:(j + 1) * heads]
            lhs = [jnp.concatenate([uws[h][:, DN_DV:].astype(BF16), qe_l[h]], axis=0) for h in hs]
            ws = [_dot(lhs[h], sts[h].astype(BF16)) for h in hs]
            v_new = [(uws[h][:, :DN_DV] - ws[h][:L]).astype(BF16) for h in hs]
            outs = [ws[h][L:] + _dot(qk_l[h], v_new[h]) for h in hs]
            incs = [_dot(kd_l[h], v_new[h], TN) for h in hs]
            for h in hs:
                vs = slice(h * DN_DV, (h + 1) * DN_DV)
                o_ref[0, rows, vs] = outs[h] + acc_ref[0, rows, vs] if has_acc else outs[h]
            sts = [end_l[h] * sts[h] + incs[h] for h in hs]
        for h in hs:
            st_ref[0, h] = sts[h]
        return carry

    assert nc % DELTA_GROUP == 0
    lax.fori_loop(0, nc // DELTA_GROUP, body, 0)


def delta_scan(pm, ps, gparams, s0, rev, q_off, k_off, v_off, heads, g_col, beta_col, acc=None):
    B, T, _ = pm.shape
    Tb = _time_block(T)
    NB = T // Tb
    kw, vw = heads * DN_DK, heads * DN_DV
    assert q_off % kw == 0 and k_off % kw == 0 and v_off % vw == 0
    blk = (lambda i: NB - 1 - i) if rev else (lambda i: i)
    st_spec = pl.BlockSpec((1,) + s0.shape[1:], lambda b, i: (b, 0, 0, 0))
    out_map = lambda b, i: (b, blk(i), 0)
    acc_arr, acc_spec = _acc_input(acc, (1, Tb, vw), out_map)
    return pl.pallas_call(
        functools.partial(_delta_kernel, rev=rev, heads=heads, nc=Tb // CHUNK, g_col=g_col, beta_col=beta_col,
                          has_acc=acc is not None),
        grid=(B, NB),
        in_specs=[pl.BlockSpec((1, Tb, kw), lambda b, i: (b, blk(i), q_off // kw)),
                  pl.BlockSpec((1, Tb, kw), lambda b, i: (b, blk(i), k_off // kw)),
                  pl.BlockSpec((1, Tb, vw), lambda b, i: (b, blk(i), v_off // vw)),
                  pl.BlockSpec((1, Tb, ps.shape[-1]), lambda b, i: (b, blk(i), 0)),
                  pl.BlockSpec(gparams.shape, lambda b, i: (0, 0)),
                  st_spec, acc_spec],
        out_specs=[pl.BlockSpec((1, Tb, vw), out_map), st_spec],
        out_shape=[jax.ShapeDtypeStruct((B, T, vw), F32), jax.ShapeDtypeStruct(s0.shape, F32)],
        name="delta_scan",
        compiler_params=_params(("parallel", "arbitrary")),
    )(pm, pm, pm, ps, gparams, s0, acc_arr)


def delta_bidir(pm_c, ps_c, pm_l, ps_l, a_log, dt_bias, q_off, k_off, v_off, col0):
    B = pm_l.shape[0]
    heads = a_log.shape[-1]
    ncol = ps_l.shape[-1]
    gparams = jnp.zeros((8, ncol), F32)
    gparams = gparams.at[0, col0:col0 + 2 * heads].set(dt_bias.reshape(-1))
    gparams = gparams.at[1, col0:col0 + 2 * heads].set(-jnp.exp(a_log).reshape(-1))
    oc = ol = None
    for d in (0, 1):
        s0 = jnp.zeros((B, heads, DN_DK, DN_DV), F32)
        args = (q_off, k_off, v_off, heads, col0 + d * heads, col0 + 2 * heads + d * heads)
        oc, s1 = delta_scan(pm_c, ps_c, gparams, s0, bool(d), *args, oc)
        ol, _ = delta_scan(pm_l, ps_l, gparams, s1, bool(d), *args, ol)
    return oc, ol


def _rglru_kernel(x_ref, cw_ref, cb_ref, wr_ref, br_ref, wi_ref, bi_ref, lam_ref, h0_ref, acc_ref, o_ref, hT_ref,
                  *, rev, nc, seg, has_acc):
    @pl.when(pl.program_id(2) == 0)
    def _():
        hT_ref[...] = h0_ref[...]

    n = x_ref.shape[1]
    xc = _dwconv(x_ref[0], cw_ref, seg) + cb_ref[...]
    xb = xc.astype(BF16)
    r = jax.nn.sigmoid(_dot(xb, wr_ref[0].astype(BF16)) + br_ref[...])
    i = jax.nn.sigmoid(_dot(xb, wi_ref[0].astype(BF16)) + bi_ref[...])
    log_a = -RG_C * r * _softplus(-lam_ref[...])
    a = jnp.exp(log_a)
    bx = jnp.sqrt(1.0 - jnp.exp(2.0 * log_a)) * (i * xc)

    pos = lax.broadcasted_iota(jnp.int32, (n, 1), 0) % CHUNK
    s = 1
    while s < CHUNK:
        ok = (pos < CHUNK - s) if rev else (pos >= s)
        shift = (n - s) if rev else s
        a_sh = pltpu.roll(a, shift, axis=0)
        b_sh = pltpu.roll(bx, shift, axis=0)
        bx = jnp.where(ok, a * b_sh + bx, bx)
        a = jnp.where(ok, a * a_sh, a)
        s *= 2

    h = hT_ref[0]
    for ci in range(nc):
        c = (nc - 1 - ci) if rev else ci
        rows = slice(c * CHUNK, (c + 1) * CHUNK)
        hc = bx[rows] + a[rows] * h
        o_ref[0, rows, :] = hc + acc_ref[0, rows, :] if has_acc else hc
        h = hc[0:1] if rev else hc[CHUNK - 1:CHUNK]
    hT_ref[0] = h


def rglru_scan(pm, conv_w, conv_b, w_r, b_r, w_i, b_i, lam, h0, rev, seg, width, acc=None):
    B, T, _ = pm.shape
    nblk, bw = w_r.shape[0], w_r.shape[1]
    assert bw % LANES == 0 and nblk * bw == width
    Tb = _tile(T, 2048)
    NB = T // Tb
    assert Tb % seg == 0 or (NB == 1 and seg == T)
    blk = (lambda i: NB - 1 - i) if rev else (lambda i: i)
    row = pl.BlockSpec((1, bw), lambda b, g, i: (0, g))
    out_map = lambda b, g, i: (b, blk(i), g)
    acc_arr, acc_spec = _acc_input(acc, (1, Tb, bw), out_map)
    return pl.pallas_call(
        functools.partial(_rglru_kernel, rev=rev, nc=Tb // CHUNK, seg=seg, has_acc=acc is not None),
        grid=(B, nblk, NB),
        in_specs=[pl.BlockSpec((1, Tb, bw), lambda b, g, i: (b, blk(i), g)),
                  pl.BlockSpec((conv_w.shape[0], bw), lambda b, g, i: (0, g)),
                  row,
                  pl.BlockSpec((1, bw, bw), lambda b, g, i: (g, 0, 0)), row,
                  pl.BlockSpec((1, bw, bw), lambda b, g, i: (g, 0, 0)), row,
                  row,
                  pl.BlockSpec((1, 1, bw), lambda b, g, i: (b, 0, g)),
                  acc_spec],
        out_specs=[pl.BlockSpec((1, Tb, bw), out_map),
                   pl.BlockSpec((1, 1, bw), lambda b, g, i: (b, 0, g))],
        out_shape=[jax.ShapeDtypeStruct((B, T, width), F32), jax.ShapeDtypeStruct(h0.shape, F32)],
        name="rglru_scan",
        compiler_params=_params(("parallel", "parallel", "arbitrary")),
    )(pm, conv_w, conv_b, w_r, b_r, w_i, b_i, lam, h0, acc_arr)


def rglru_bidir(pm_c, pm_l, conv_w, conv_b, w_r, b_r, w_i, b_i, lam):
    B = pm_l.shape[0]
    width = conv_b.shape[-1]
    oc = ol = None
    for d in (0, 1):
        h0 = jnp.zeros((B, 1, width), F32)
        args = (conv_w, conv_b.reshape(1, width), w_r[d], b_r[d].reshape(1, width), w_i[d],
                b_i[d].reshape(1, width), lam[d].reshape(1, width))
        oc, h1 = rglru_scan(pm_c, *args, h0, bool(d), pm_c.shape[1], width, oc)
        ol, _ = rglru_scan(pm_l, *args, h1, bool(d), GRID_W, width, ol)
    return oc, ol


def _pad_cols(w, mult):
    return jnp.pad(w, ((0, 0), (0, (-w.shape[-1]) % mult)))


def kernel(x, c, ctx, c_ctx, mod_w, mod_b, norm_mix_g, norm_ffn_g, ffn_w_gu, ffn_w_down, final_norm_g,
           ab_w_in, ab_w_out, ml_ig_b, ml_fg_b, ml_norm_g, dn_conv_w, dn_a_log, dn_dt_bias, dn_norm_g,
           cd_w_in, cd_w_out, rg_conv_w, rg_conv_b, rg_w_r, rg_b_r, rg_w_i, rg_b_i, rg_lambda,
           gla_w_up, gla_b_up, gla_norm_g):
    B, T, D = x.shape
    depth = mod_w.shape[0]
    mlh, dnh = ml_ig_b.shape[-1], dn_a_log.shape[-1]
    rgw = rg_conv_b.shape[-1]
    glh = gla_w_up.shape[-1] // GLA_DK

    cond = jnp.concatenate([c, c_ctx[None, :], jnp.zeros((8 - B - 1, D), F32)], axis=0)

    m_all = adaln_all(cond, mod_w, mod_b)

    xc, xl = ctx, x
    for layer in range(depth):
        last = layer == depth - 1
        j = layer // 2
        m = m_all[layer]
        lat = [t[:, None, :] for t in jnp.split(m[:B], 6, axis=-1)]
        cx = [t[:, None, :] for t in jnp.split(m[B:B + 1], 6, axis=-1)]

        if layer % 2 == 0:
            w_in = ab_w_in[j]
            s = np.cumsum((0, mlh * ML_DK, mlh * ML_DK, mlh * ML_DV, mlh * ML_DV, 2 * mlh, 2 * mlh,
                           dnh * DN_DK, dnh * DN_DK, dnh * DN_DV, dnh * DN_DV, 2 * dnh, 2 * dnh))
            w_main = jnp.concatenate([w_in[:, s[0]:s[4]], w_in[:, s[6]:s[10]]], axis=1)
            w_small = jnp.concatenate([w_in[:, s[4]:s[6]], w_in[:, s[10]:s[12]]], axis=1)
            w_out = ab_w_out[j]
            o = np.cumsum((0, mlh * ML_DK, mlh * ML_DK, mlh * ML_DV, mlh * ML_DV,
                           dnh * DN_DK, dnh * DN_DK, dnh * DN_DV))
        else:
            w_in = cd_w_in[j]
            o = np.cumsum((0, rgw, rgw, glh * GLA_DK, glh * GLA_DK, glh * GLA_DV))
            nmain = int(o[5]) + glh * GLA_DV
            w_main, w_small = w_in[:, :nmain], w_in[:, nmain:]
            w_out = cd_w_out[j]
        o = [int(v) for v in o]
        w_main = w_main.astype(BF16)
        w_small = _pad_cols(w_small, LANES).astype(BF16)
        w_out = w_out.astype(BF16)

        if layer % 2 == 0:
            conv = lambda seg: dict(conv_w=dn_conv_w[j], conv_off=o[4], norm_cols=2 * dnh * DN_DK, seg=seg,
                                    norm_group=DN_DK)
        else:
            conv = lambda seg: {}
        pm_l, ps_l = inproj(xl, norm_mix_g[layer], lat[1], lat[0], w_main, w_small, **conv(GRID_W))
        pm_c, ps_c = inproj(xc, norm_mix_g[layer], cx[1], cx[0], w_main, w_small, **conv(xc.shape[1]))

        if layer % 2 == 0:
            mix_a = mlstm_bidir(pm_c, ps_c, pm_l, ps_l, ml_ig_b[j], ml_fg_b[j], o[0], o[1], o[2])
            mix_b = delta_bidir(pm_c, ps_c, pm_l, ps_l, dn_a_log[j], dn_dt_bias[j], o[4], o[5], o[6], 4 * mlh)
            margs = (0, o[3], o[7], ml_norm_g[j], dn_norm_g[j], ML_DV, DN_DV)
        else:
            mix_a = rglru_bidir(pm_c, pm_l, rg_conv_w[j], rg_conv_b[j], rg_w_r[j], rg_b_r[j], rg_w_i[j],
                                rg_b_i[j], rg_lambda[j])
            mix_b = gla_bidir(pm_c, ps_c, pm_l, ps_l, gla_w_up[j], gla_b_up[j], o[2], o[3], o[4])
            margs = (1, o[1], o[5], jnp.ones((rgw,), F32), gla_norm_g[j], rgw, GLA_DV)

        def merged(which, pm, xres, gate):
            kind, ga_off, gb_off, na, nb, da, db = margs
            return merge_outproj(kind, mix_a[which], mix_b[which], pm, ga_off, gb_off, na, nb, da, db, xres, gate,
                                 w_out)

        w_gu = ffn_w_gu[layer].astype(BF16)
        w_dn = ffn_w_down[layer].astype(BF16)
        xl = merged(1, pm_l, xl, lat[2])
        xl = ffn(xl, norm_ffn_g[layer], lat[4], lat[3], lat[5], w_gu, w_dn, final_norm_g if last else None)
        if not last:
            xc = merged(0, pm_c, xc, cx[2])
            xc = ffn(xc, norm_ffn_g[layer], cx[4], cx[3], cx[5], w_gu, w_dn)
    return xl
```
